```python
import jax, jax.numpy as jnp
from jax import lax
import numpy as np

D_MODEL = 1024
BATCH = 8
SEQ = 8192
DEPTH = 1

CHUNK = 64
EPS = 1e-6

MLA_HEADS = D_MODEL // 128
MLA_NOPE = 64
MLA_ROPE = 32
MLA_V = 64
Q_LORA = D_MODEL // 4
KV_LORA = D_MODEL // 8
ROPE_THETA = 10000.0
Q_BLOCK = 128

CA_HEADS = D_MODEL // 128
CA_DIM = 64
CA_LEFT = 8
CA_BAND = (CA_LEFT + 1) * CHUNK
REL_MAX = 256
REL_SIZE = CHUNK + REL_MAX

MIX_WIDTH = MLA_HEADS * MLA_V + CA_HEADS * CA_DIM
IN_WIDTH = Q_LORA + KV_LORA + MLA_ROPE + 3 * CA_HEADS * CA_DIM

N_EXPERTS = 32
TOP_K = 4
D_EXPERT = D_MODEL
SWIGLU_LIMIT = 7.0
SWIGLU_ALPHA = 1.702
ROW_BLOCK = 256

kernel_name = 'hybrid_mla_chunkattn_moe'


def _rmsnorm(x, g):
    xf = x.astype(jnp.float32)
    y = xf * lax.rsqrt(jnp.mean(xf * xf, axis=-1, keepdims=True) + EPS)
    return (y * g.astype(jnp.float32)).astype(x.dtype)


def _rope(x, cos, sin):
    xf = x.astype(jnp.float32)
    x1, x2 = jnp.split(xf, 2, axis=-1)
    return jnp.concatenate([x1 * cos - x2 * sin, x2 * cos + x1 * sin], axis=-1).astype(x.dtype)


def _mla_attention(q_nope, q_rope, k_nope, k_rope, v):
    bsz, seq, heads, _ = q_nope.shape
    nq = seq // Q_BLOCK
    qn = q_nope.reshape(bsz, nq, Q_BLOCK, heads, MLA_NOPE).transpose(1, 0, 2, 3, 4)
    qr = q_rope.reshape(bsz, nq, Q_BLOCK, heads, MLA_ROPE).transpose(1, 0, 2, 3, 4)
    k_chunk = jnp.arange(seq) // CHUNK
    scale = (MLA_NOPE + MLA_ROPE) ** -0.5

    def block(args):
        qn_b, qr_b, bi = args
        s = (jnp.einsum('bqhd,bkhd->bhqk', qn_b, k_nope)
             + jnp.einsum('bqhr,bkr->bhqk', qr_b, k_rope)).astype(jnp.float32) * scale
        q_chunk = (bi * Q_BLOCK + jnp.arange(Q_BLOCK)) // CHUNK
        mask = k_chunk[None, :] <= q_chunk[:, None]
        s = jnp.where(mask, s, -1e30)
        p = jax.nn.softmax(s, axis=-1).astype(v.dtype)
        return jnp.einsum('bhqk,bkhd->bqhd', p, v)

    o = lax.map(block, (qn, qr, jnp.arange(nq)))
    return o.transpose(1, 0, 2, 3, 4).reshape(bsz, seq, heads * MLA_V)


def _chunk_attention(q, k, v, rel_bias):
    bsz, seq, heads, dh = q.shape
    nc = seq // CHUNK
    qi = jnp.arange(CHUNK)[:, None]
    kk = jnp.arange(CA_BAND)[None, :]
    rel = qi - kk % CHUNK + (CA_LEFT - kk // CHUNK) * CHUNK
    idx = jnp.clip(rel, -(CHUNK - 1), REL_MAX) + (CHUNK - 1)
    bias = rel_bias.astype(jnp.float32)[:, idx]
    chunk_valid = (jnp.arange(nc)[:, None] - CA_LEFT + jnp.arange(CA_LEFT + 1)[None, :]) >= 0
    key_valid = jnp.repeat(chunk_valid, CHUNK, axis=1)
    scale = dh ** -0.5

    def per_seq(args):
        qs, ks, vs = args
        qc = qs.reshape(nc, CHUNK, heads, dh)
        pad = jnp.zeros((CA_LEFT * CHUNK, heads, dh), ks.dtype)
        kp = jnp.concatenate([pad, ks], axis=0).reshape(nc + CA_LEFT, CHUNK, heads, dh)
        vp = jnp.concatenate([pad, vs], axis=0).reshape(nc + CA_LEFT, CHUNK, heads, dh)
        kb = jnp.stack([kp[j:j + nc] for j in range(CA_LEFT + 1)], axis=1).reshape(nc, CA_BAND, heads, dh)
        vb = jnp.stack([vp[j:j + nc] for j in range(CA_LEFT + 1)], axis=1).reshape(nc, CA_BAND, heads, dh)
        s = jnp.einsum('cqhd,ckhd->chqk', qc, kb).astype(jnp.float32) * scale + bias[None]
        s = jnp.where(key_valid[:, None, None, :], s, -1e30)
        p = jax.nn.softmax(s, axis=-1).astype(vs.dtype)
        o = jnp.einsum('chqk,ckhd->cqhd', p, vb)
        return o.reshape(seq, heads * dh)

    return lax.map(per_seq, (q, k, v))


def _moe(h, w_router, b_router, w_gu, b_gu, w_down, b_down):
    bsz, seq, d = h.shape
    t = bsz * seq
    xt = h.reshape(t, d)
    logits = (xt @ w_router + b_router).astype(jnp.float32)
    top_val, top_idx = lax.top_k(logits, TOP_K)
    gates = jax.nn.softmax(top_val, axis=-1)
    m = t * TOP_K
    e = top_idx.reshape(m).astype(jnp.int32)
    tok = jnp.repeat(jnp.arange(t, dtype=jnp.int32), TOP_K)
    g = gates.reshape(m)
    order = jnp.argsort(e, stable=True)
    e_s, tok_s, g_s = e[order], tok[order], g[order]
    counts = jax.ops.segment_sum(jnp.ones((m,), jnp.int32), e, num_segments=N_EXPERTS)
    padded = ((counts + ROW_BLOCK - 1) // ROW_BLOCK) * ROW_BLOCK
    start = jnp.cumsum(counts) - counts
    pend = jnp.cumsum(padded)
    pstart = pend - padded
    dest = pstart[e_s] + jnp.arange(m, dtype=jnp.int32) - start[e_s]
    p_rows = m + N_EXPERTS * ROW_BLOCK
    nb = p_rows // ROW_BLOCK
    row_tok = jnp.full((p_rows,), t, jnp.int32).at[dest].set(tok_s)
    row_gate = jnp.zeros((p_rows,), g_s.dtype).at[dest].set(g_s)
    block_exp = jnp.minimum(jnp.searchsorted(pend, jnp.arange(nb, dtype=jnp.int32) * ROW_BLOCK, side='right'),
                            N_EXPERTS - 1)
    x_pad = jnp.concatenate([xt, jnp.zeros((1, d), xt.dtype)], axis=0)
    xin = x_pad[row_tok].reshape(nb, ROW_BLOCK, d)

    def expert_block(args):
        xb, eb = args
        gu = xb @ w_gu[eb] + b_gu[eb]
        gate = jnp.minimum(gu[:, :D_EXPERT], SWIGLU_LIMIT)
        up = jnp.clip(gu[:, D_EXPERT:], -SWIGLU_LIMIT, SWIGLU_LIMIT)
        glu = gate * jax.nn.sigmoid(gate * SWIGLU_ALPHA)
        return ((up + 1) * glu) @ w_down[eb] + b_down[eb]

    y = lax.map(expert_block, (xin, block_exp)).reshape(p_rows, d)
    y = y * row_gate[:, None].astype(y.dtype)
    out = jax.ops.segment_sum(y, row_tok, num_segments=t + 1)[:t]
    return out.reshape(bsz, seq, d)


def setup_inputs(seed: int = 0) -> dict:
    key = jax.random.key(seed)
    ks = jax.random.split(key, 24)
    L, D, E, F = DEPTH, D_MODEL, N_EXPERTS, D_EXPERT
    nrm = lambda k, shape, s: jax.random.normal(k, shape, jnp.float32) * s
    gain = lambda k, shape: 1.0 + 0.02 * jax.random.normal(k, shape, jnp.float32)
    return {
        'x': nrm(ks[0], (BATCH, SEQ, D), 1.0),
        'c': nrm(ks[1], (BATCH, D), 1.0),
        'w_ada': nrm(ks[2], (L, D, 6 * D), D ** -0.5),
        'b_ada': nrm(ks[3], (L, 6 * D), 0.02),
        'g_pre_mix': gain(ks[4], (L, D)),
        'g_post_mix': gain(ks[5], (L, D)),
        'g_pre_ffn': gain(ks[6], (L, D)),
        'g_post_ffn': gain(ks[7], (L, D)),
        'w_in': nrm(ks[8], (L, D, IN_WIDTH), D ** -0.5),
        'g_q': gain(ks[9], (L, Q_LORA)),
        'w_qb': nrm(ks[10], (L, Q_LORA, MLA_HEADS * (MLA_NOPE + MLA_ROPE)), Q_LORA ** -0.5),
        'g_kv': gain(ks[11], (L, KV_LORA)),
        'w_kvb': nrm(ks[12], (L, KV_LORA, MLA_HEADS * (MLA_NOPE + MLA_V)), KV_LORA ** -0.5),
        'rel_bias': nrm(ks[13], (L, CA_HEADS, REL_SIZE), 0.5),
        'w_o': nrm(ks[14], (L, MIX_WIDTH, D), MIX_WIDTH ** -0.5),
        'w_router': nrm(ks[15], (L, D, E), D ** -0.5),
        'b_router': nrm(ks[16], (L, E), 0.01),
        'w_gu': nrm(ks[17], (L, E, D, 2 * F), D ** -0.5),
        'b_gu': nrm(ks[18], (L, E, 2 * F), 0.02),
        'w_down': nrm(ks[19], (L, E, F, D), F ** -0.5),
        'b_down': nrm(ks[20], (L, E, D), 0.02),
    }


def reference(x, c, w_ada, b_ada, g_pre_mix, g_post_mix, g_pre_ffn, g_post_ffn, w_in, g_q, w_qb,
              g_kv, w_kvb, rel_bias, w_o, w_router, b_router, w_gu, b_gu, w_down, b_down):
    bsz, seq, _ = x.shape
    half = MLA_ROPE // 2
    inv_freq = ROPE_THETA ** (-jnp.arange(half, dtype=jnp.float32) / half)
    ang = jnp.arange(seq, dtype=jnp.float32)[:, None] * inv_freq[None, :]
    cos, sin = jnp.cos(ang), jnp.sin(ang)
    cut = np.cumsum([Q_LORA, KV_LORA, MLA_ROPE, CA_HEADS * CA_DIM, CA_HEADS * CA_DIM]).tolist()
    ca_shape = (bsz, seq, CA_HEADS, CA_DIM)
    for l in range(DEPTH):
        mod = jax.nn.silu(c) @ w_ada[l] + b_ada[l]
        sh1, sc1, gt1, sh2, sc2, gt2 = jnp.split(mod, 6, axis=-1)
        h = _rmsnorm(x, g_pre_mix[l]) * (1 + sc1[:, None, :]) + sh1[:, None, :]
        z = h @ w_in[l]
        cq, ckv, kr, qc, kc, vc = jnp.split(z, cut, axis=-1)
        q = (_rmsnorm(cq, g_q[l]) @ w_qb[l]).reshape(bsz, seq, MLA_HEADS, MLA_NOPE + MLA_ROPE)
        q_nope = q[..., :MLA_NOPE]
        q_rope = _rope(q[..., MLA_NOPE:], cos[:, None, :], sin[:, None, :])
        kv = (_rmsnorm(ckv, g_kv[l]) @ w_kvb[l]).reshape(bsz, seq, MLA_HEADS, MLA_NOPE + MLA_V)
        k_nope, v_mla = kv[..., :MLA_NOPE], kv[..., MLA_NOPE:]
        k_rope = _rope(kr, cos, sin)
        o_a = _mla_attention(q_nope, q_rope, k_nope, k_rope, v_mla)
        o_b = _chunk_attention(qc.reshape(ca_shape), kc.reshape(ca_shape), vc.reshape(ca_shape), rel_bias[l])
        o = jnp.concatenate([o_a, o_b], axis=-1) @ w_o[l]
        x = x + gt1[:, None, :] * _rmsnorm(o, g_post_mix[l])
        h = _rmsnorm(x, g_pre_ffn[l]) * (1 + sc2[:, None, :]) + sh2[:, None, :]
        f = _moe(h, w_router[l], b_router[l], w_gu[l], b_gu[l], w_down[l], b_down[l])
        x = x + gt2[:, None, :] * _rmsnorm(f, g_post_ffn[l])
    return x
```

```python
import functools
import math

import jax
import jax.numpy as jnp
from jax import lax
from jax.experimental import pallas as pl
from jax.experimental.pallas import tpu as pltpu

F32 = jnp.float32
BF16 = jnp.bfloat16

D_MODEL = 1024
CHUNK = 64
EPS = 1e-6
MLA_HEADS = 8
MLA_NOPE = 64
MLA_ROPE = 32
MLA_V = 64
Q_LORA = 256
KV_LORA = 128
ROPE_THETA = 10000.0
CA_HEADS = 8
CA_DIM = 64
CA_LEFT = 8
REL_MAX = 256
N_EXPERTS = 32
TOP_K = 4
D_EXPERT = 1024
SWIGLU_LIMIT = 7.0
SWIGLU_ALPHA = 1.702

LANES = 128
HEAD_PAD = 128
LOG2E = math.log2(math.e)
NEG = -1e30
VMEM_LIMIT = 56 * 1024 * 1024

TM = 512
TQ = 512
CQ = 256
CBAND = 3 * CQ
ROWS = 512
CA_W = CA_HEADS * CA_DIM
W_IN_COLS = Q_LORA + KV_LORA + 2 * LANES + 3 * CA_W


def _cparams(sem):
    return pltpu.CompilerParams(dimension_semantics=sem, vmem_limit_bytes=VMEM_LIMIT)


def _nt_dot(a, b):
    return lax.dot_general(a, b, (((1,), (1,)), ((), ())), preferred_element_type=F32)


def _rms(x, g):
    return x * lax.rsqrt(jnp.mean(x * x, axis=-1, keepdims=True) + EPS) * g


def _ada_kernel(c_ref, w_ref, b_ref, o_ref):
    c = c_ref[...]
    a = (c / (1.0 + jnp.exp(-c))).astype(BF16)
    o_ref[...] = jnp.dot(a, w_ref[...].astype(BF16), preferred_element_type=F32) + b_ref[...]


def _ada(c, w, b):
    bsz, d = c.shape
    n = w.shape[1]
    tn = 1024
    return pl.pallas_call(
        _ada_kernel,
        grid=(n // tn,),
        in_specs=[pl.BlockSpec((bsz, d), lambda j: (0, 0)),
                  pl.BlockSpec((d, tn), lambda j: (0, j)),
                  pl.BlockSpec((1, tn), lambda j: (0, j))],
        out_specs=pl.BlockSpec((bsz, tn), lambda j: (0, j)),
        out_shape=jax.ShapeDtypeStruct((bsz, n), F32),
        compiler_params=_cparams(("arbitrary",)),
        name="ada",
    )(c, w, b.reshape(1, n))


def _proj_kernel(x_ref, sc_ref, sh_ref, g_ref, win_ref, gq_ref, wq1_ref, wq2_ref, gkv_ref,
                 wk_ref, wv_ref, cq_ref, sq_ref, ck_ref, sk_ref,
                 q_out, k_out, v_out, qc_out, kc_out, vc_out):
    h = _rms(x_ref[...], g_ref[...]) * (1.0 + sc_ref[0]) + sh_ref[0]
    z = jnp.dot(h.astype(BF16), win_ref[...], preferred_element_type=F32)
    o = 0
    cq = z[:, o:o + Q_LORA]; o += Q_LORA
    ckv = z[:, o:o + KV_LORA]; o += KV_LORA
    kr1 = z[:, o:o + LANES]; o += LANES
    kr2 = z[:, o:o + LANES]; o += LANES
    qc = z[:, o:o + CA_W]; o += CA_W
    kc = z[:, o:o + CA_W]; o += CA_W
    vc = z[:, o:o + CA_W]

    cqn = _rms(cq, gq_ref[...]).astype(BF16)
    q1 = jnp.dot(cqn, wq1_ref[...], preferred_element_type=F32)
    q2 = jnp.dot(cqn, wq2_ref[...], preferred_element_type=F32)
    ckvn = _rms(ckv, gkv_ref[...]).astype(BF16)
    kn = jnp.dot(ckvn, wk_ref[...], preferred_element_type=F32)
    v_out[...] = jnp.dot(ckvn, wv_ref[...], preferred_element_type=F32).astype(BF16)

    cq_t, sq_t = cq_ref[...], sq_ref[...]
    krope = kr1 * ck_ref[...] + kr2 * sk_ref[...]
    for hd in range(MLA_HEADS):
        sl = slice(hd * HEAD_PAD, (hd + 1) * HEAD_PAD)
        q_out[:, sl] = (q1[:, sl] * cq_t + q2[:, sl] * sq_t).astype(BF16)
        k_out[:, sl] = (kn[:, sl] + krope).astype(BF16)

    qc_out[...] = (qc * (CA_DIM ** -0.5 * LOG2E)).astype(BF16)
    kc_out[...] = kc.astype(BF16)
    vc_out[...] = vc.astype(BF16)


def _proj(x2, sc1, sh1, g_pre, w, tabs, seq):
    t, d = x2.shape
    tpb = seq // TM
    row = lambda i: (i, 0)
    full = lambda i: (0, 0)
    per_b = lambda i: (i // tpb, 0, 0)
    pos = lambda i: (i % tpb, 0)
    hw = MLA_HEADS * HEAD_PAD
    in_specs = [
        pl.BlockSpec((TM, d), row),
        pl.BlockSpec((1, 1, d), per_b), pl.BlockSpec((1, 1, d), per_b),
        pl.BlockSpec((1, d), full),
        pl.BlockSpec((d, W_IN_COLS), full),
        pl.BlockSpec((1, Q_LORA), full),
        pl.BlockSpec((Q_LORA, hw), full), pl.BlockSpec((Q_LORA, hw), full),
        pl.BlockSpec((1, KV_LORA), full),
        pl.BlockSpec((KV_LORA, hw), full), pl.BlockSpec((KV_LORA, MLA_HEADS * MLA_V), full),
        pl.BlockSpec((TM, LANES), pos), pl.BlockSpec((TM, LANES), pos),
        pl.BlockSpec((TM, LANES), pos), pl.BlockSpec((TM, LANES), pos),
    ]
    widths = (hw, hw, MLA_HEADS * MLA_V, CA_W, CA_W, CA_W)
    return pl.pallas_call(
        _proj_kernel,
        grid=(t // TM,),
        in_specs=in_specs,
        out_specs=[pl.BlockSpec((TM, n), row) for n in widths],
        out_shape=[jax.ShapeDtypeStruct((t, n), BF16) for n in widths],
        compiler_params=_cparams(("arbitrary",)),
        name="proj",
    )(x2, sc1, sh1, g_pre, w["w_in"], w["g_q"], w["wq1"], w["wq2"], w["g_kv"], w["wk"], w["wv"],
      tabs["cq"], tabs["sq"], tabs["ck"], tabs["sk"])


def _mla_kernel(q_ref, k_ref, v_ref, o_ref):
    i = pl.program_id(2)
    r = lax.broadcasted_iota(jnp.int32, (TQ, TQ), 0) // CHUNK
    c = lax.broadcasted_iota(jnp.int32, (TQ, TQ), 1) // CHUNK
    diag_mask = c <= r
    outs = []
    for hh in range(2):
        hs = slice(hh * HEAD_PAD, (hh + 1) * HEAD_PAD)
        q = q_ref[0, :, hs]

        def step(j, carry, masked, q=q, hs=hs):
            m, l, acc = carry
            off = pl.multiple_of(j * TQ, TQ)
            k = k_ref[0, pl.ds(off, TQ), hs]
            v = v_ref[0, pl.ds(off, TQ), :]
            s = _nt_dot(q, k)
            if masked:
                s = jnp.where(diag_mask, s, NEG)
            m_new = jnp.maximum(m, jnp.max(s, axis=-1, keepdims=True))
            alpha = jnp.exp2(m - m_new)
            p = jnp.exp2(s - m_new)
            l = alpha * l + jnp.sum(p, axis=-1, keepdims=True)
            acc = alpha * acc + jnp.dot(p.astype(BF16), v, preferred_element_type=F32)
            return m_new, l, acc

        init = (jnp.full((TQ, 1), NEG, F32), jnp.zeros((TQ, 1), F32), jnp.zeros((TQ, LANES), F32))
        carry = lax.fori_loop(0, i, functools.partial(step, masked=False), init)
        _, l, acc = step(i, carry, True)
        outs.append(acc / l)
    lane = lax.broadcasted_iota(jnp.int32, (TQ, LANES), 1)
    o_ref[0] = jnp.where(lane < MLA_V, outs[0], outs[1]).astype(BF16)


def _mla(q, k, v):
    bsz, seq, _ = q.shape
    pairs = MLA_HEADS // 2
    return pl.pallas_call(
        _mla_kernel,
        grid=(bsz, pairs, seq // TQ),
        in_specs=[pl.BlockSpec((1, TQ, 2 * HEAD_PAD), lambda b, p, i: (b, i, p)),
                  pl.BlockSpec((1, seq, 2 * HEAD_PAD), lambda b, p, i: (b, 0, p)),
                  pl.BlockSpec((1, seq, 2 * MLA_V), lambda b, p, i: (b, 0, p))],
        out_specs=pl.BlockSpec((1, TQ, 2 * MLA_V), lambda b, p, i: (b, i, p)),
        out_shape=jax.ShapeDtypeStruct((bsz, seq, MLA_HEADS * MLA_V), BF16),
        compiler_params=_cparams(("arbitrary", "arbitrary", "arbitrary")),
        name="mla",
    )(q, k, v)


def _chunk_kernel(q_ref, k0_ref, k1_ref, k2_ref, v0_ref, v1_ref, v2_ref, bias_ref, o_ref):
    i = pl.program_id(1)
    lane = lax.broadcasted_iota(jnp.int32, (CQ, LANES), 1)
    lo = lane < CA_DIM
    k_refs = (k0_ref, k1_ref, k2_ref)
    v_refs = (v0_ref, v1_ref, v2_ref)
    for p in range(CA_HEADS // 2):
        sl = slice(p * LANES, (p + 1) * LANES)
        q = q_ref[0, :, sl]
        ks = [kr[0, :, sl] for kr in k_refs]
        vs = [vr[0, :, sl] for vr in v_refs]
        outs = []
        for hh in range(2):
            qm = jnp.where(lo if hh == 0 else jnp.logical_not(lo), q, jnp.zeros_like(q))
            parts = []
            for cb in range(3):
                s = _nt_dot(qm, ks[cb])
                if cb < 2:
                    s = jnp.where(i - 2 + cb >= 0, s, NEG)
                parts.append(s)
            s = jnp.concatenate(parts, axis=1) + bias_ref[2 * p + hh]
            m = jnp.max(s, axis=-1, keepdims=True)
            e = jnp.exp2(s - m)
            l = jnp.sum(e, axis=-1, keepdims=True)
            pb = e.astype(BF16)
            o = jnp.dot(pb[:, 0:CQ], vs[0], preferred_element_type=F32)
            o += jnp.dot(pb[:, CQ:2 * CQ], vs[1], preferred_element_type=F32)
            o += jnp.dot(pb[:, 2 * CQ:3 * CQ], vs[2], preferred_element_type=F32)
            outs.append(o / l)
        o_ref[0, :, sl] = jnp.where(lo, outs[0], outs[1]).astype(BF16)


def _chunk_attn(qc, kc, vc, bias):
    bsz, seq, w = qc.shape
    blk = lambda off: pl.BlockSpec((1, CQ, w), lambda b, i: (b, jnp.maximum(i + off, 0), 0))
    return pl.pallas_call(
        _chunk_kernel,
        grid=(bsz, seq // CQ),
        in_specs=[blk(0), blk(-2), blk(-1), blk(0), blk(-2), blk(-1), blk(0),
                  pl.BlockSpec((CA_HEADS, CQ, CBAND), lambda b, i: (0, 0, 0))],
        out_specs=pl.BlockSpec((1, CQ, w), lambda b, i: (b, i, 0)),
        out_shape=jax.ShapeDtypeStruct((bsz, seq, w), BF16),
        compiler_params=_cparams(("arbitrary", "arbitrary")),
        name="chunk_attn",
    )(qc, kc, kc, kc, vc, vc, vc, bias)


def _post_kernel(oa_ref, ob_ref, x_ref, gt_ref, sc_ref, sh_ref, gpost_ref, gpre_ref,
                 woa_ref, wob_ref, wr_ref, br_ref,
                 x1_out, h2_out, idx_out, gate_out, rank_out, cnt_out, carry_ref):
    t = pl.program_id(0)

    @pl.when(t == 0)
    def _():
        carry_ref[...] = jnp.zeros_like(carry_ref)

    o = jnp.dot(oa_ref[...], woa_ref[...], preferred_element_type=F32)
    o += jnp.dot(ob_ref[...], wob_ref[...], preferred_element_type=F32)
    x1 = x_ref[...] + gt_ref[0] * _rms(o, gpost_ref[...])
    x1_out[...] = x1
    h2 = (_rms(x1, gpre_ref[...]) * (1.0 + sc_ref[0]) + sh_ref[0]).astype(BF16)
    h2_out[...] = h2

    logits = _nt_dot(wr_ref[...], h2) + br_ref[...]
    eid = lax.broadcasted_iota(jnp.int32, (N_EXPERTS, TM), 0)
    vals, idxs = [], []
    work = logits
    for _k in range(TOP_K):
        m = jnp.max(work, axis=0, keepdims=True)
        ix = jnp.min(jnp.where(work == m, eid, N_EXPERTS), axis=0, keepdims=True)
        work = jnp.where(eid == ix, -jnp.inf, work)
        vals.append(m)
        idxs.append(ix)
    es = [jnp.exp(v - vals[0]) for v in vals]
    den = es[0] + es[1] + es[2] + es[3]
    gate_out[...] = jnp.concatenate([e / den for e in es], axis=0)
    idx_out[...] = jnp.concatenate(idxs, axis=0)

    sel = (eid == idxs[0]) | (eid == idxs[1]) | (eid == idxs[2]) | (eid == idxs[3])
    self32 = sel.astype(F32)
    rr = lax.broadcasted_iota(jnp.int32, (TM, TM), 0)
    cc = lax.broadcasted_iota(jnp.int32, (TM, TM), 1)
    upper = (rr < cc).astype(BF16)
    before = jnp.dot(self32.astype(BF16), upper, preferred_element_type=F32)
    before = before + carry_ref[:, 0:1]
    ranks = [jnp.sum(jnp.where(eid == ix, before, 0.0), axis=0, keepdims=True) for ix in idxs]
    rank_out[...] = jnp.concatenate(ranks, axis=0).astype(jnp.int32)
    carry_ref[...] = carry_ref[...] + jnp.sum(self32, axis=1, keepdims=True)
    cnt_out[...] = carry_ref[...]


def _post(oa, ob, x2, gt1, sc2, sh2, g_post, g_pre, w, seq):
    t, d = x2.shape
    tpb = seq // TM
    row = lambda i: (i, 0)
    col = lambda i: (0, i)
    full = lambda i: (0, 0)
    per_b = lambda i: (i // tpb, 0, 0)
    hw = oa.shape[1]
    in_specs = [
        pl.BlockSpec((TM, hw), row), pl.BlockSpec((TM, hw), row), pl.BlockSpec((TM, d), row),
        pl.BlockSpec((1, 1, d), per_b), pl.BlockSpec((1, 1, d), per_b), pl.BlockSpec((1, 1, d), per_b),
        pl.BlockSpec((1, d), full), pl.BlockSpec((1, d), full),
        pl.BlockSpec((hw, d), full), pl.BlockSpec((hw, d), full),
        pl.BlockSpec((N_EXPERTS, d), full), pl.BlockSpec((N_EXPERTS, 1), full),
    ]
    out_specs = [
        pl.BlockSpec((TM, d), row), pl.BlockSpec((TM, d), row),
        pl.BlockSpec((TOP_K, TM), col), pl.BlockSpec((TOP_K, TM), col), pl.BlockSpec((TOP_K, TM), col),
        pl.BlockSpec((N_EXPERTS, LANES), full),
    ]
    out_shape = [
        jax.ShapeDtypeStruct((t, d), F32), jax.ShapeDtypeStruct((t, d), BF16),
        jax.ShapeDtypeStruct((TOP_K, t), jnp.int32), jax.ShapeDtypeStruct((TOP_K, t), F32),
        jax.ShapeDtypeStruct((TOP_K, t), jnp.int32),
        jax.ShapeDtypeStruct((N_EXPERTS, LANES), F32),
    ]
    return pl.pallas_call(
        _post_kernel,
        grid=(t // TM,),
        in_specs=in_specs,
        out_specs=out_specs,
        out_shape=out_shape,
        scratch_shapes=[pltpu.VMEM((N_EXPERTS, LANES), F32)],
        compiler_params=_cparams(("arbitrary",)),
        name="post",
    )(oa, ob, x2, gt1, sc2, sh2, g_post, g_pre, w["wo_a"], w["wo_b"], w["wr_t"], w["b_r"])


def _expert_kernel(be_ref, nu_ref, x_ref, wgu_ref, bgu_ref, wd_ref, bd_ref, y_ref):
    j = pl.program_id(0)

    @pl.when(j < nu_ref[0])
    def _():
        gu = jnp.dot(x_ref[...], wgu_ref[0], preferred_element_type=F32) + bgu_ref[0]
        gate = jnp.minimum(gu[:, :D_EXPERT], SWIGLU_LIMIT)
        up = jnp.clip(gu[:, D_EXPERT:], -SWIGLU_LIMIT, SWIGLU_LIMIT)
        glu = gate / (1.0 + jnp.exp(-SWIGLU_ALPHA * gate))
        act = ((up + 1.0) * glu).astype(BF16)
        y = jnp.dot(act, wd_ref[0], preferred_element_type=F32) + bd_ref[0]
        y_ref[...] = y.astype(y_ref.dtype)

    @pl.when(j >= nu_ref[0])
    def _():
        y_ref[...] = jnp.zeros_like(y_ref)


def _experts(xin, block_exp, n_used, wgu, bgu, wd, bd):
    p_rows, d = xin.shape
    nb = p_rows // ROWS
    f2 = wgu.shape[2]
    grid_spec = pltpu.PrefetchScalarGridSpec(
        num_scalar_prefetch=2,
        grid=(nb,),
        in_specs=[
            pl.BlockSpec((ROWS, d), lambda j, be, nu: (jnp.minimum(j, nu[0] - 1), 0)),
            pl.BlockSpec((1, d, f2), lambda j, be, nu: (be[j], 0, 0)),
            pl.BlockSpec((1, 1, f2), lambda j, be, nu: (be[j], 0, 0)),
            pl.BlockSpec((1, f2 // 2, d), lambda j, be, nu: (be[j], 0, 0)),
            pl.BlockSpec((1, 1, d), lambda j, be, nu: (be[j], 0, 0)),
        ],
        out_specs=pl.BlockSpec((ROWS, d), lambda j, be, nu: (j, 0)),
    )
    return pl.pallas_call(
        _expert_kernel,
        grid_spec=grid_spec,
        out_shape=jax.ShapeDtypeStruct((p_rows, d), BF16),
        compiler_params=_cparams(("arbitrary",)),
        name="experts",
    )(block_exp, n_used, xin, wgu, bgu, wd, bd)


def _final_kernel(yg_ref, g_ref, x1_ref, gt_ref, gpost_ref, o_ref):
    g = g_ref[...]
    f = yg_ref[0].astype(F32) * g[:, 0:1]
    for k in range(1, TOP_K):
        f += yg_ref[k].astype(F32) * g[:, k:k + 1]
    o_ref[...] = x1_ref[...] + gt_ref[0] * _rms(f, gpost_ref[...])


def _final(yg, gates_t, x1, gt2, g_post, seq):
    t, d = x1.shape
    tpb = seq // TM
    row = lambda i: (i, 0)
    return pl.pallas_call(
        _final_kernel,
        grid=(t // TM,),
        in_specs=[pl.BlockSpec((TOP_K, TM, d), lambda i: (0, i, 0)),
                  pl.BlockSpec((TM, TOP_K), row),
                  pl.BlockSpec((TM, d), row),
                  pl.BlockSpec((1, 1, d), lambda i: (i // tpb, 0, 0)),
                  pl.BlockSpec((1, d), lambda i: (0, 0))],
        out_specs=pl.BlockSpec((TM, d), row),
        out_shape=jax.ShapeDtypeStruct((t, d), F32),
        compiler_params=_cparams(("arbitrary",)),
        name="final",
    )(yg, gates_t, x1, gt2, g_post)


def _dispatch(h2, dest, p_rows):
    t, d = h2.shape
    src = jnp.broadcast_to(h2[None], (TOP_K, t, d)).reshape(TOP_K * t, d)
    return jnp.zeros((p_rows, d), h2.dtype).at[dest.reshape(-1)].set(src)


def _gather_rows(y, dest):
    return y[dest]


def _prep_weights(w_in, g_q, w_qb, g_kv, w_kvb, w_o, w_router, b_router):
    d = w_in.shape[0]
    o = 0
    w_cq = w_in[:, o:o + Q_LORA]; o += Q_LORA
    w_ckv = w_in[:, o:o + KV_LORA]; o += KV_LORA
    w_kr = w_in[:, o:o + MLA_ROPE]; o += MLA_ROPE
    w_ca = w_in[:, o:]
    half = MLA_ROPE // 2
    zpad = lambda n: jnp.zeros((d, n), w_in.dtype)
    x1, x2 = w_kr[:, :half], w_kr[:, half:]
    tail = HEAD_PAD - MLA_NOPE - MLA_ROPE
    w_kr1 = jnp.concatenate([zpad(MLA_NOPE), x1, x2, zpad(tail)], axis=1)
    w_kr2 = jnp.concatenate([zpad(MLA_NOPE), -x2, x1, zpad(tail)], axis=1)
    w_in_all = jnp.concatenate([w_cq, w_ckv, w_kr1, w_kr2, w_ca], axis=1).astype(BF16)

    wq = w_qb.reshape(Q_LORA, MLA_HEADS, MLA_NOPE + MLA_ROPE)
    qn, q1, q2 = wq[..., :MLA_NOPE], wq[..., MLA_NOPE:MLA_NOPE + half], wq[..., MLA_NOPE + half:]
    zq = jnp.zeros((Q_LORA, MLA_HEADS, tail), w_qb.dtype)
    wq1 = jnp.concatenate([qn, q1, q2, zq], axis=-1).reshape(Q_LORA, -1).astype(BF16)
    wq2 = jnp.concatenate([jnp.zeros_like(qn), -q2, q1, zq], axis=-1).reshape(Q_LORA, -1).astype(BF16)

    wkv = w_kvb.reshape(KV_LORA, MLA_HEADS, MLA_NOPE + MLA_V)
    kn = wkv[..., :MLA_NOPE]
    wk = jnp.concatenate([kn, jnp.zeros((KV_LORA, MLA_HEADS, HEAD_PAD - MLA_NOPE), w_kvb.dtype)],
                         axis=-1).reshape(KV_LORA, -1).astype(BF16)
    wv = wkv[..., MLA_NOPE:].reshape(KV_LORA, -1).astype(BF16)
    mla_w = MLA_HEADS * MLA_V
    return {
        "w_in": w_in_all, "g_q": g_q.reshape(1, -1), "wq1": wq1, "wq2": wq2,
        "g_kv": g_kv.reshape(1, -1), "wk": wk, "wv": wv,
        "wo_a": w_o[:mla_w].astype(BF16), "wo_b": w_o[mla_w:].astype(BF16),
        "wr_t": w_router.T.astype(BF16), "b_r": b_router.reshape(-1, 1),
    }


def _rope_tables(seq):
    half = MLA_ROPE // 2
    inv_freq = ROPE_THETA ** (-jnp.arange(half, dtype=F32) / half)
    ang = jnp.arange(seq, dtype=F32)[:, None] * inv_freq[None, :]
    cos, sin = jnp.cos(ang), jnp.sin(ang)
    tail = HEAD_PAD - MLA_NOPE - MLA_ROPE
    ones = jnp.ones((seq, MLA_NOPE), F32)
    zn = jnp.zeros((seq, MLA_NOPE), F32)
    zt = jnp.zeros((seq, tail), F32)
    qs = (MLA_NOPE + MLA_ROPE) ** -0.5 * LOG2E
    return {
        "cq": jnp.concatenate([ones, cos, cos, zt], axis=1) * qs,
        "sq": jnp.concatenate([zn, sin, sin, zt], axis=1) * qs,
        "ck": jnp.concatenate([zn, cos, cos, zt], axis=1),
        "sk": jnp.concatenate([zn, sin, sin, zt], axis=1),
    }


def _bias_table(rel_bias):
    r = jnp.arange(CQ)[:, None]
    c = jnp.arange(CBAND)[None, :]
    rel = r - c + 2 * CQ
    dchunk = r // CHUNK - (c // CHUNK - CA_LEFT)
    visible = (dchunk >= 0) & (dchunk <= CA_LEFT)
    idx = jnp.clip(rel, -(CHUNK - 1), REL_MAX) + (CHUNK - 1)
    b = rel_bias.astype(F32)[:, idx] * LOG2E
    return jnp.where(visible[None], b, NEG)


def _layer(x, c, w_ada, b_ada, g_pre_mix, g_post_mix, g_pre_ffn, g_post_ffn, w_in, g_q, w_qb,
           g_kv, w_kvb, rel_bias, w_o, w_router, b_router, w_gu, b_gu, w_down, b_down):
    bsz, seq, d = x.shape
    t = bsz * seq
    mod = _ada(c, w_ada, b_ada).reshape(bsz, 6, 1, d)
    sh1, sc1, gt1, sh2, sc2, gt2 = [mod[:, k] for k in range(6)]
    w = _prep_weights(w_in, g_q, w_qb, g_kv, w_kvb, w_o, w_router, b_router)
    tabs = _rope_tables(seq)
    x2 = x.reshape(t, d)

    q, k, v, qc, kc, vc = _proj(x2, sc1, sh1, g_pre_mix.reshape(1, d), w, tabs, seq)
    shp = lambda a: a.reshape(bsz, seq, a.shape[-1])
    oa = _mla(shp(q), shp(k), shp(v)).reshape(t, -1)
    ob = _chunk_attn(shp(qc), shp(kc), shp(vc), _bias_table(rel_bias)).reshape(t, -1)

    x1, h2, top_idx, gates, rank, cnt = _post(oa, ob, x2, gt1, sc2, sh2, g_post_mix.reshape(1, d),
                                              g_pre_ffn.reshape(1, d), w, seq)

    counts = cnt[:, 0].astype(jnp.int32)
    padded = ((counts + ROWS - 1) // ROWS) * ROWS
    pend = jnp.cumsum(padded)
    pstart = pend - padded
    dest = pstart[top_idx] + rank
    p_rows = t * TOP_K + N_EXPERTS * ROWS
    nb = p_rows // ROWS
    block_exp = jnp.minimum(jnp.searchsorted(pend, jnp.arange(nb, dtype=jnp.int32) * ROWS, side="right"),
                            N_EXPERTS - 1).astype(jnp.int32)
    n_used = (pend[-1:] // ROWS).astype(jnp.int32)

    xin = _dispatch(h2, dest, p_rows)
    y = _experts(xin, block_exp, n_used, w_gu.astype(BF16), b_gu.reshape(N_EXPERTS, 1, -1),
                 w_down.astype(BF16), b_down.reshape(N_EXPERTS, 1, -1))
    yg = _gather_rows(y, dest)
    out = _final(yg, gates.T, x1, gt2, g_post_ffn.reshape(1, d), seq)
    return out.reshape(bsz, seq, d)


def kernel(x, c, w_ada, b_ada, g_pre_mix, g_post_mix, g_pre_ffn, g_post_ffn, w_in, g_q, w_qb,
           g_kv, w_kvb, rel_bias, w_o, w_router, b_router, w_gu, b_gu, w_down, b_down):
    for l in range(w_ada.shape[0]):
        x = _layer(x, c, w_ada[l], b_ada[l], g_pre_mix[l], g_post_mix[l], g_pre_ffn[l], g_post_ffn[l],
                   w_in[l], g_q[l], w_qb[l], g_kv[l], w_kvb[l], rel_bias[l], w_o[l], w_router[l],
                   b_router[l], w_gu[l], b_gu[l], w_down[l], b_down[l])
    return x
```

```python
import functools
import math

import jax
import jax.numpy as jnp
from jax import lax
from jax.experimental import pallas as pl
from jax.experimental.pallas import tpu as pltpu
from jax.experimental.pallas import tpu_sc as plsc

F32 = jnp.float32
BF16 = jnp.bfloat16
U32 = jnp.uint32

D_MODEL = 1024
CHUNK = 64
EPS = 1e-6
MLA_HEADS = 8
MLA_NOPE = 64
MLA_ROPE = 32
MLA_V = 64
Q_LORA = 256
KV_LORA = 128
ROPE_THETA = 10000.0
CA_HEADS = 8
CA_DIM = 64
CA_LEFT = 8
REL_MAX = 256
N_EXPERTS = 32
TOP_K = 4
D_EXPERT = 1024
SWIGLU_LIMIT = 7.0
SWIGLU_ALPHA = 1.702

LANES = 128
HEAD_PAD = 128
LOG2E = math.log2(math.e)
NEG = -1e30
VMEM_LIMIT = 56 * 1024 * 1024

TM = 512
TQ = 512
CQ = 256
CBAND = 3 * CQ
ROWS = 512
SC_CORES = 2
SC_WORKERS = SC_CORES * 16
SC_WIN = 128
CA_W = CA_HEADS * CA_DIM
W_IN_COLS = Q_LORA + KV_LORA + 2 * LANES + 3 * CA_W


def _cparams(sem):
    return pltpu.CompilerParams(dimension_semantics=sem, vmem_limit_bytes=VMEM_LIMIT)


def _nt_dot(a, b):
    return lax.dot_general(a, b, (((1,), (1,)), ((), ())), preferred_element_type=F32)


def _rms(x, g):
    return x * lax.rsqrt(jnp.mean(x * x, axis=-1, keepdims=True) + EPS) * g


def _pack_rows(x):
    n = x.shape[1] // 2
    lo = lax.bitcast_convert_type(x[:, :n].astype(BF16).astype(F32), U32)
    hi = lax.bitcast_convert_type(x[:, n:].astype(BF16).astype(F32), U32)
    return (lo >> 16) | hi


def _unpack_rows(p):
    lo = lax.bitcast_convert_type(p << 16, F32)
    hi = lax.bitcast_convert_type(p & jnp.uint32(0xFFFF0000), F32)
    return lo, hi


def _ada_kernel(c_ref, w_ref, b_ref, o_ref):
    c = c_ref[...]
    a = (c / (1.0 + jnp.exp(-c))).astype(BF16)
    o_ref[...] = jnp.dot(a, w_ref[...].astype(BF16), preferred_element_type=F32) + b_ref[...]


def _ada(c, w, b):
    bsz, d = c.shape
    n = w.shape[1]
    tn = 1024
    return pl.pallas_call(
        _ada_kernel,
        grid=(n // tn,),
        in_specs=[pl.BlockSpec((bsz, d), lambda j: (0, 0)),
                  pl.BlockSpec((d, tn), lambda j: (0, j)),
                  pl.BlockSpec((1, tn), lambda j: (0, j))],
        out_specs=pl.BlockSpec((bsz, tn), lambda j: (0, j)),
        out_shape=jax.ShapeDtypeStruct((bsz, n), F32),
        compiler_params=_cparams(("arbitrary",)),
        name="ada",
    )(c, w, b.reshape(1, n))


def _proj_kernel(x_ref, sc_ref, sh_ref, g_ref, win_ref, gq_ref, wq1_ref, wq2_ref, gkv_ref,
                 wk_ref, wv_ref, cq_ref, sq_ref, ck_ref, sk_ref,
                 q_out, k_out, v_out, qc_out, kc_out, vc_out):
    h = _rms(x_ref[...], g_ref[...]) * (1.0 + sc_ref[0]) + sh_ref[0]
    z = jnp.dot(h.astype(BF16), win_ref[...], preferred_element_type=F32)
    o = 0
    cq = z[:, o:o + Q_LORA]; o += Q_LORA
    ckv = z[:, o:o + KV_LORA]; o += KV_LORA
    kr1 = z[:, o:o + LANES]; o += LANES
    kr2 = z[:, o:o + LANES]; o += LANES
    qc = z[:, o:o + CA_W]; o += CA_W
    kc = z[:, o:o + CA_W]; o += CA_W
    vc = z[:, o:o + CA_W]

    cqn = _rms(cq, gq_ref[...]).astype(BF16)
    q1 = jnp.dot(cqn, wq1_ref[...], preferred_element_type=F32)
    q2 = jnp.dot(cqn, wq2_ref[...], preferred_element_type=F32)
    ckvn = _rms(ckv, gkv_ref[...]).astype(BF16)
    kn = jnp.dot(ckvn, wk_ref[...], preferred_element_type=F32)
    v_out[...] = jnp.dot(ckvn, wv_ref[...], preferred_element_type=F32).astype(BF16)

    cq_t, sq_t = cq_ref[...], sq_ref[...]
    krope = kr1 * ck_ref[...] + kr2 * sk_ref[...]
    for hd in range(MLA_HEADS):
        sl = slice(hd * HEAD_PAD, (hd + 1) * HEAD_PAD)
        q_out[:, sl] = (q1[:, sl] * cq_t + q2[:, sl] * sq_t).astype(BF16)
        k_out[:, sl] = (kn[:, sl] + krope).astype(BF16)

    qc_out[...] = (qc * (CA_DIM ** -0.5 * LOG2E)).astype(BF16)
    kc_out[...] = kc.astype(BF16)
    vc_out[...] = vc.astype(BF16)


def _proj(x2, sc1, sh1, g_pre, w, tabs, seq):
    t, d = x2.shape
    tpb = seq // TM
    row = lambda i: (i, 0)
    full = lambda i: (0, 0)
    per_b = lambda i: (i // tpb, 0, 0)
    pos = lambda i: (i % tpb, 0)
    hw = MLA_HEADS * HEAD_PAD
    in_specs = [
        pl.BlockSpec((TM, d), row),
        pl.BlockSpec((1, 1, d), per_b), pl.BlockSpec((1, 1, d), per_b),
        pl.BlockSpec((1, d), full),
        pl.BlockSpec((d, W_IN_COLS), full),
        pl.BlockSpec((1, Q_LORA), full),
        pl.BlockSpec((Q_LORA, hw), full), pl.BlockSpec((Q_LORA, hw), full),
        pl.BlockSpec((1, KV_LORA), full),
        pl.BlockSpec((KV_LORA, hw), full), pl.BlockSpec((KV_LORA, MLA_HEADS * MLA_V), full),
        pl.BlockSpec((TM, LANES), pos), pl.BlockSpec((TM, LANES), pos),
        pl.BlockSpec((TM, LANES), pos), pl.BlockSpec((TM, LANES), pos),
    ]
    widths = (hw, hw, MLA_HEADS * MLA_V, CA_W, CA_W, CA_W)
    return pl.pallas_call(
        _proj_kernel,
        grid=(t // TM,),
        in_specs=in_specs,
        out_specs=[pl.BlockSpec((TM, n), row) for n in widths],
        out_shape=[jax.ShapeDtypeStruct((t, n), BF16) for n in widths],
        compiler_params=_cparams(("arbitrary",)),
        name="proj",
    )(x2, sc1, sh1, g_pre, w["w_in"], w["g_q"], w["wq1"], w["wq2"], w["g_kv"], w["wk"], w["wv"],
      tabs["cq"], tabs["sq"], tabs["ck"], tabs["sk"])


def _mla_kernel(q_ref, k_ref, v_ref, o_ref):
    i = pl.program_id(2)
    r = lax.broadcasted_iota(jnp.int32, (TQ, TQ), 0) // CHUNK
    c = lax.broadcasted_iota(jnp.int32, (TQ, TQ), 1) // CHUNK
    diag_mask = c <= r
    outs = []
    for hh in range(2):
        hs = slice(hh * HEAD_PAD, (hh + 1) * HEAD_PAD)
        q = q_ref[0, :, hs]

        def step(j, carry, masked, q=q, hs=hs):
            m, l, acc = carry
            off = pl.multiple_of(j * TQ, TQ)
            k = k_ref[0, pl.ds(off, TQ), hs]
            v = v_ref[0, pl.ds(off, TQ), :]
            s = _nt_dot(q, k)
            if masked:
                s = jnp.where(diag_mask, s, NEG)
            m_new = jnp.maximum(m, jnp.max(s, axis=-1, keepdims=True))
            alpha = jnp.exp2(m - m_new)
            p = jnp.exp2(s - m_new)
            l = alpha * l + jnp.sum(p, axis=-1, keepdims=True)
            acc = alpha * acc + jnp.dot(p.astype(BF16), v, preferred_element_type=F32)
            return m_new, l, acc

        init = (jnp.full((TQ, 1), NEG, F32), jnp.zeros((TQ, 1), F32), jnp.zeros((TQ, LANES), F32))
        carry = lax.fori_loop(0, i, functools.partial(step, masked=False), init)
        _, l, acc = step(i, carry, True)
        outs.append(acc / l)
    lane = lax.broadcasted_iota(jnp.int32, (TQ, LANES), 1)
    o_ref[0] = jnp.where(lane < MLA_V, outs[0], outs[1]).astype(BF16)


def _mla(q, k, v):
    bsz, seq, _ = q.shape
    pairs = MLA_HEADS // 2
    return pl.pallas_call(
        _mla_kernel,
        grid=(bsz, pairs, seq // TQ),
        in_specs=[pl.BlockSpec((1, TQ, 2 * HEAD_PAD), lambda b, p, i: (b, i, p)),
                  pl.BlockSpec((1, seq, 2 * HEAD_PAD), lambda b, p, i: (b, 0, p)),
                  pl.BlockSpec((1, seq, 2 * MLA_V), lambda b, p, i: (b, 0, p))],
        out_specs=pl.BlockSpec((1, TQ, 2 * MLA_V), lambda b, p, i: (b, i, p)),
        out_shape=jax.ShapeDtypeStruct((bsz, seq, MLA_HEADS * MLA_V), BF16),
        compiler_params=_cparams(("arbitrary", "arbitrary", "arbitrary")),
        name="mla",
    )(q, k, v)


def _chunk_kernel(q_ref, k0_ref, k1_ref, k2_ref, v0_ref, v1_ref, v2_ref, bias_ref, o_ref):
    i = pl.program_id(1)
    lane = lax.broadcasted_iota(jnp.int32, (CQ, LANES), 1)
    lo = lane < CA_DIM
    k_refs = (k0_ref, k1_ref, k2_ref)
    v_refs = (v0_ref, v1_ref, v2_ref)
    for p in range(CA_HEADS // 2):
        sl = slice(p * LANES, (p + 1) * LANES)
        q = q_ref[0, :, sl]
        ks = [kr[0, :, sl] for kr in k_refs]
        vs = [vr[0, :, sl] for vr in v_refs]
        outs = []
        for hh in range(2):
            qm = jnp.where(lo if hh == 0 else jnp.logical_not(lo), q, jnp.zeros_like(q))
            parts = []
            for cb in range(3):
                s = _nt_dot(qm, ks[cb])
                if cb < 2:
                    s = jnp.where(i - 2 + cb >= 0, s, NEG)
                parts.append(s)
            s = jnp.concatenate(parts, axis=1) + bias_ref[2 * p + hh]
            m = jnp.max(s, axis=-1, keepdims=True)
            e = jnp.exp2(s - m)
            l = jnp.sum(e, axis=-1, keepdims=True)
            pb = e.astype(BF16)
            o = jnp.dot(pb[:, 0:CQ], vs[0], preferred_element_type=F32)
            o += jnp.dot(pb[:, CQ:2 * CQ], vs[1], preferred_element_type=F32)
            o += jnp.dot(pb[:, 2 * CQ:3 * CQ], vs[2], preferred_element_type=F32)
            outs.append(o / l)
        o_ref[0, :, sl] = jnp.where(lo, outs[0], outs[1]).astype(BF16)


def _chunk_attn(qc, kc, vc, bias):
    bsz, seq, w = qc.shape
    blk = lambda off: pl.BlockSpec((1, CQ, w), lambda b, i: (b, jnp.maximum(i + off, 0), 0))
    return pl.pallas_call(
        _chunk_kernel,
        grid=(bsz, seq // CQ),
        in_specs=[blk(0), blk(-2), blk(-1), blk(0), blk(-2), blk(-1), blk(0),
                  pl.BlockSpec((CA_HEADS, CQ, CBAND), lambda b, i: (0, 0, 0))],
        out_specs=pl.BlockSpec((1, CQ, w), lambda b, i: (b, i, 0)),
        out_shape=jax.ShapeDtypeStruct((bsz, seq, w), BF16),
        compiler_params=_cparams(("arbitrary", "arbitrary")),
        name="chunk_attn",
    )(qc, kc, kc, kc, vc, vc, vc, bias)


def _post_kernel(oa_ref, ob_ref, x_ref, gt_ref, sc_ref, sh_ref, gpost_ref, gpre_ref,
                 woa_ref, wob_ref, wr_ref, br_ref,
                 x1_out, h2_out, idx_out, gate_out, rank_out, cnt_out, carry_ref):
    t = pl.program_id(0)

    @pl.when(t == 0)
    def _():
        carry_ref[...] = jnp.zeros_like(carry_ref)

    o = jnp.dot(oa_ref[...], woa_ref[...], preferred_element_type=F32)
    o += jnp.dot(ob_ref[...], wob_ref[...], preferred_element_type=F32)
    x1 = x_ref[...] + gt_ref[0] * _rms(o, gpost_ref[...])
    x1_out[...] = x1
    h2f = _rms(x1, gpre_ref[...]) * (1.0 + sc_ref[0]) + sh_ref[0]
    h2_out[...] = _pack_rows(h2f)
    h2 = h2f.astype(BF16)

    logits = _nt_dot(wr_ref[...], h2) + br_ref[...]
    eid = lax.broadcasted_iota(jnp.int32, (N_EXPERTS, TM), 0)
    vals, idxs = [], []
    work = logits
    for _k in range(TOP_K):
        m = jnp.max(work, axis=0, keepdims=True)
        ix = jnp.min(jnp.where(work == m, eid, N_EXPERTS), axis=0, keepdims=True)
        work = jnp.where(eid == ix, -jnp.inf, work)
        vals.append(m)
        idxs.append(ix)
    es = [jnp.exp(v - vals[0]) for v in vals]
    den = es[0] + es[1] + es[2] + es[3]
    gate_out[...] = jnp.concatenate([e / den for e in es], axis=0)
    idx_out[...] = jnp.concatenate(idxs, axis=0)

    sel = (eid == idxs[0]) | (eid == idxs[1]) | (eid == idxs[2]) | (eid == idxs[3])
    self32 = sel.astype(F32)
    rr = lax.broadcasted_iota(jnp.int32, (TM, TM), 0)
    cc = lax.broadcasted_iota(jnp.int32, (TM, TM), 1)
    upper = (rr < cc).astype(BF16)
    before = jnp.dot(self32.astype(BF16), upper, preferred_element_type=F32)
    before = before + carry_ref[:, 0:1]
    ranks = [jnp.sum(jnp.where(eid == ix, before, 0.0), axis=0, keepdims=True) for ix in idxs]
    rank_out[...] = jnp.concatenate(ranks, axis=0).astype(jnp.int32)
    carry_ref[...] = carry_ref[...] + jnp.sum(self32, axis=1, keepdims=True)
    cnt_out[...] = carry_ref[...]


def _post(oa, ob, x2, gt1, sc2, sh2, g_post, g_pre, w, seq):
    t, d = x2.shape
    tpb = seq // TM
    row = lambda i: (i, 0)
    col = lambda i: (0, i)
    full = lambda i: (0, 0)
    per_b = lambda i: (i // tpb, 0, 0)
    hw = oa.shape[1]
    in_specs = [
        pl.BlockSpec((TM, hw), row), pl.BlockSpec((TM, hw), row), pl.BlockSpec((TM, d), row),
        pl.BlockSpec((1, 1, d), per_b), pl.BlockSpec((1, 1, d), per_b), pl.BlockSpec((1, 1, d), per_b),
        pl.BlockSpec((1, d), full), pl.BlockSpec((1, d), full),
        pl.BlockSpec((hw, d), full), pl.BlockSpec((hw, d), full),
        pl.BlockSpec((N_EXPERTS, d), full), pl.BlockSpec((N_EXPERTS, 1), full),
    ]
    out_specs = [
        pl.BlockSpec((TM, d), row), pl.BlockSpec((TM, d // 2), row),
        pl.BlockSpec((TOP_K, TM), col), pl.BlockSpec((TOP_K, TM), col), pl.BlockSpec((TOP_K, TM), col),
        pl.BlockSpec((N_EXPERTS, LANES), full),
    ]
    out_shape = [
        jax.ShapeDtypeStruct((t, d), F32), jax.ShapeDtypeStruct((t, d // 2), U32),
        jax.ShapeDtypeStruct((TOP_K, t), jnp.int32), jax.ShapeDtypeStruct((TOP_K, t), F32),
        jax.ShapeDtypeStruct((TOP_K, t), jnp.int32),
        jax.ShapeDtypeStruct((N_EXPERTS, LANES), F32),
    ]
    return pl.pallas_call(
        _post_kernel,
        grid=(t // TM,),
        in_specs=in_specs,
        out_specs=out_specs,
        out_shape=out_shape,
        scratch_shapes=[pltpu.VMEM((N_EXPERTS, LANES), F32)],
        compiler_params=_cparams(("arbitrary",)),
        name="post",
    )(oa, ob, x2, gt1, sc2, sh2, g_post, g_pre, w["wo_a"], w["wo_b"], w["wr_t"], w["b_r"])


def _expert_kernel(be_ref, nu_ref, x_ref, wgu_ref, bgu_ref, wd_ref, bd_ref, y_ref):
    j = pl.program_id(0)

    @pl.when(j < nu_ref[0])
    def _():
        x_lo, x_hi = _unpack_rows(x_ref[...])
        half = x_lo.shape[1]
        gu = jnp.dot(x_lo.astype(BF16), wgu_ref[0, :half], preferred_element_type=F32)
        gu += jnp.dot(x_hi.astype(BF16), wgu_ref[0, half:], preferred_element_type=F32)
        gu += bgu_ref[0]
        gate = jnp.minimum(gu[:, :D_EXPERT], SWIGLU_LIMIT)
        up = jnp.clip(gu[:, D_EXPERT:], -SWIGLU_LIMIT, SWIGLU_LIMIT)
        glu = gate / (1.0 + jnp.exp(-SWIGLU_ALPHA * gate))
        act = ((up + 1.0) * glu).astype(BF16)
        y = jnp.dot(act, wd_ref[0], preferred_element_type=F32) + bd_ref[0]
        y_ref[...] = _pack_rows(y)

    @pl.when(j >= nu_ref[0])
    def _():
        y_ref[...] = jnp.zeros_like(y_ref)


def _experts(xin, block_exp, n_used, wgu, bgu, wd, bd):
    p_rows, dw = xin.shape
    d = 2 * dw
    nb = p_rows // ROWS
    f2 = wgu.shape[2]
    grid_spec = pltpu.PrefetchScalarGridSpec(
        num_scalar_prefetch=2,
        grid=(nb,),
        in_specs=[
            pl.BlockSpec((ROWS, dw), lambda j, be, nu: (jnp.minimum(j, nu[0] - 1), 0)),
            pl.BlockSpec((1, d, f2), lambda j, be, nu: (be[j], 0, 0)),
            pl.BlockSpec((1, 1, f2), lambda j, be, nu: (be[j], 0, 0)),
            pl.BlockSpec((1, f2 // 2, d), lambda j, be, nu: (be[j], 0, 0)),
            pl.BlockSpec((1, 1, d), lambda j, be, nu: (be[j], 0, 0)),
        ],
        out_specs=pl.BlockSpec((ROWS, dw), lambda j, be, nu: (j, 0)),
    )
    return pl.pallas_call(
        _expert_kernel,
        grid_spec=grid_spec,
        out_shape=jax.ShapeDtypeStruct((p_rows, dw), U32),
        compiler_params=_cparams(("arbitrary",)),
        name="experts",
    )(block_exp, n_used, xin, wgu, bgu, wd, bd)


def _final_kernel(yg_ref, g_ref, x1_ref, gt_ref, gpost_ref, o_ref):
    g = g_ref[...]
    f_lo, f_hi = None, None
    for k in range(TOP_K):
        lo, hi = _unpack_rows(yg_ref[k])
        gk = g[:, k:k + 1]
        f_lo = lo * gk if f_lo is None else f_lo + lo * gk
        f_hi = hi * gk if f_hi is None else f_hi + hi * gk
    f = jnp.concatenate([f_lo, f_hi], axis=1)
    o_ref[...] = x1_ref[...] + gt_ref[0] * _rms(f, gpost_ref[...])


def _final(yg, gates_t, x1, gt2, g_post, seq):
    t, d = x1.shape
    tpb = seq // TM
    row = lambda i: (i, 0)
    return pl.pallas_call(
        _final_kernel,
        grid=(t // TM,),
        in_specs=[pl.BlockSpec((TOP_K, TM, d // 2), lambda i: (0, i, 0)),
                  pl.BlockSpec((TM, TOP_K), row),
                  pl.BlockSpec((TM, d), row),
                  pl.BlockSpec((1, 1, d), lambda i: (i // tpb, 0, 0)),
                  pl.BlockSpec((1, d), lambda i: (0, 0))],
        out_specs=pl.BlockSpec((TM, d), row),
        out_shape=jax.ShapeDtypeStruct((t, d), F32),
        compiler_params=_cparams(("arbitrary",)),
        name="final",
    )(yg, gates_t, x1, gt2, g_post)


def _sc_mesh():
    return plsc.VectorSubcoreMesh(core_axis_name="c", subcore_axis_name="s")


def _sc_worker():
    return lax.axis_index("s") * SC_CORES + lax.axis_index("c")


def _dispatch(h2, dest, p_rows):
    t, dw = h2.shape
    per_w = t // SC_WORKERS
    n_win = per_w // SC_WIN

    @functools.partial(
        pl.kernel, mesh=_sc_mesh(),
        out_type=jax.ShapeDtypeStruct((p_rows, dw), h2.dtype),
        scratch_types=[pltpu.VMEM((TOP_K, SC_WIN), jnp.int32),
                       pltpu.VMEM((SC_WIN, dw), h2.dtype),
                       pltpu.SemaphoreType.DMA],
        name="dispatch",
    )
    def run(h_hbm, d_hbm, o_hbm, idx_v, rows_v, sem):
        wid = _sc_worker()

        @pl.loop(0, n_win)
        def _(wi):
            base = pl.multiple_of(wid * per_w + wi * SC_WIN, SC_WIN)
            for k in range(TOP_K):
                pltpu.sync_copy(d_hbm.at[pl.ds(k * t + base, SC_WIN)], idx_v.at[k])
            pltpu.sync_copy(h_hbm.at[pl.ds(base, SC_WIN)], rows_v)
            for k in range(TOP_K):
                pltpu.async_copy(rows_v, o_hbm.at[idx_v.at[k]], sem).wait()

    return run(h2, dest)


def _gather_rows(y, dest, t):
    _, dw = y.shape
    per_w = t // SC_WORKERS
    n_win = per_w // SC_WIN

    @functools.partial(
        pl.kernel, mesh=_sc_mesh(),
        out_type=jax.ShapeDtypeStruct((TOP_K, t, dw), y.dtype),
        scratch_types=[pltpu.VMEM((SC_WIN,), jnp.int32),
                       pltpu.VMEM((SC_WIN, dw), y.dtype),
                       pltpu.SemaphoreType.DMA],
        name="gather_rows",
    )
    def run(y_hbm, d_hbm, o_hbm, idx_v, rows_v, sem):
        wid = _sc_worker()

        @pl.loop(0, n_win)
        def _(wi):
            base = pl.multiple_of(wid * per_w + wi * SC_WIN, SC_WIN)
            for k in range(TOP_K):
                pltpu.sync_copy(d_hbm.at[pl.ds(k * t + base, SC_WIN)], idx_v)
                pltpu.async_copy(y_hbm.at[idx_v], rows_v, sem).wait()
                pltpu.sync_copy(rows_v, o_hbm.at[k, pl.ds(base, SC_WIN)])

    return run(y, dest)


def _prep_weights(w_in, g_q, w_qb, g_kv, w_kvb, w_o, w_router, b_router):
    d = w_in.shape[0]
    o = 0
    w_cq = w_in[:, o:o + Q_LORA]; o += Q_LORA
    w_ckv = w_in[:, o:o + KV_LORA]; o += KV_LORA
    w_kr = w_in[:, o:o + MLA_ROPE]; o += MLA_ROPE
    w_ca = w_in[:, o:]
    half = MLA_ROPE // 2
    zpad = lambda n: jnp.zeros((d, n), w_in.dtype)
    x1, x2 = w_kr[:, :half], w_kr[:, half:]
    tail = HEAD_PAD - MLA_NOPE - MLA_ROPE
    w_kr1 = jnp.concatenate([zpad(MLA_NOPE), x1, x2, zpad(tail)], axis=1)
    w_kr2 = jnp.concatenate([zpad(MLA_NOPE), -x2, x1, zpad(tail)], axis=1)
    w_in_all = jnp.concatenate([w_cq, w_ckv, w_kr1, w_kr2, w_ca], axis=1).astype(BF16)

    wq = w_qb.reshape(Q_LORA, MLA_HEADS, MLA_NOPE + MLA_ROPE)
    qn, q1, q2 = wq[..., :MLA_NOPE], wq[..., MLA_NOPE:MLA_NOPE + half], wq[..., MLA_NOPE + half:]
    zq = jnp.zeros((Q_LORA, MLA_HEADS, tail), w_qb.dtype)
    wq1 = jnp.concatenate([qn, q1, q2, zq], axis=-1).reshape(Q_LORA, -1).astype(BF16)
    wq2 = jnp.concatenate([jnp.zeros_like(qn), -q2, q1, zq], axis=-1).reshape(Q_LORA, -1).astype(BF16)

    wkv = w_kvb.reshape(KV_LORA, MLA_HEADS, MLA_NOPE + MLA_V)
    kn = wkv[..., :MLA_NOPE]
    wk = jnp.concatenate([kn, jnp.zeros((KV_LORA, MLA_HEADS, HEAD_PAD - MLA_NOPE), w_kvb.dtype)],
                         axis=-1).reshape(KV_LORA, -1).astype(BF16)
    wv = wkv[..., MLA_NOPE:].reshape(KV_LORA, -1).astype(BF16)
    mla_w = MLA_HEADS * MLA_V
    return {
        "w_in": w_in_all, "g_q": g_q.reshape(1, -1), "wq1": wq1, "wq2": wq2,
        "g_kv": g_kv.reshape(1, -1), "wk": wk, "wv": wv,
        "wo_a": w_o[:mla_w].astype(BF16), "wo_b": w_o[mla_w:].astype(BF16),
        "wr_t": w_router.T.astype(BF16), "b_r": b_router.reshape(-1, 1),
    }


def _rope_tables(seq):
    half = MLA_ROPE // 2
    inv_freq = ROPE_THETA ** (-jnp.arange(half, dtype=F32) / half)
    ang = jnp.arange(seq, dtype=F32)[:, None] * inv_freq[None, :]
    cos, sin = jnp.cos(ang), jnp.sin(ang)
    tail = HEAD_PAD - MLA_NOPE - MLA_ROPE
    ones = jnp.ones((seq, MLA_NOPE), F32)
    zn = jnp.zeros((seq, MLA_NOPE), F32)
    zt = jnp.zeros((seq, tail), F32)
    qs = (MLA_NOPE + MLA_ROPE) ** -0.5 * LOG2E
    return {
        "cq": jnp.concatenate([ones, cos, cos, zt], axis=1) * qs,
        "sq": jnp.concatenate([zn, sin, sin, zt], axis=1) * qs,
        "ck": jnp.concatenate([zn, cos, cos, zt], axis=1),
        "sk": jnp.concatenate([zn, sin, sin, zt], axis=1),
    }


def _bias_table(rel_bias):
    r = jnp.arange(CQ)[:, None]
    c = jnp.arange(CBAND)[None, :]
    rel = r - c + 2 * CQ
    dchunk = r // CHUNK - (c // CHUNK - CA_LEFT)
    visible = (dchunk >= 0) & (dchunk <= CA_LEFT)
    idx = jnp.clip(rel, -(CHUNK - 1), REL_MAX) + (CHUNK - 1)
    b = rel_bias.astype(F32)[:, idx] * LOG2E
    return jnp.where(visible[None], b, NEG)


def _layer(x, c, w_ada, b_ada, g_pre_mix, g_post_mix, g_pre_ffn, g_post_ffn, w_in, g_q, w_qb,
           g_kv, w_kvb, rel_bias, w_o, w_router, b_router, w_gu, b_gu, w_down, b_down):
    bsz, seq, d = x.shape
    t = bsz * seq
    mod = _ada(c, w_ada, b_ada).reshape(bsz, 6, 1, d)
    sh1, sc1, gt1, sh2, sc2, gt2 = [mod[:, k] for k in range(6)]
    w = _prep_weights(w_in, g_q, w_qb, g_kv, w_kvb, w_o, w_router, b_router)
    tabs = _rope_tables(seq)
    x2 = x.reshape(t, d)

    q, k, v, qc, kc, vc = _proj(x2, sc1, sh1, g_pre_mix.reshape(1, d), w, tabs, seq)
    shp = lambda a: a.reshape(bsz, seq, a.shape[-1])
    oa = _mla(shp(q), shp(k), shp(v)).reshape(t, -1)
    ob = _chunk_attn(shp(qc), shp(kc), shp(vc), _bias_table(rel_bias)).reshape(t, -1)

    x1, h2, top_idx, gates, rank, cnt = _post(oa, ob, x2, gt1, sc2, sh2, g_post_mix.reshape(1, d),
                                              g_pre_ffn.reshape(1, d), w, seq)

    counts = cnt[:, 0].astype(jnp.int32)
    padded = ((counts + ROWS - 1) // ROWS) * ROWS
    pend = jnp.cumsum(padded)
    pstart = pend - padded
    eids = jnp.arange(N_EXPERTS, dtype=jnp.int32)[:, None, None]
    dest = (jnp.sum(jnp.where(top_idx[None] == eids, pstart[:, None, None], 0), axis=0) + rank).reshape(-1)
    p_rows = t * TOP_K + N_EXPERTS * ROWS
    nb = p_rows // ROWS
    block_start = jnp.arange(nb, dtype=jnp.int32) * ROWS
    block_exp = jnp.minimum(jnp.sum(pend[None, :] <= block_start[:, None], axis=1), N_EXPERTS - 1).astype(jnp.int32)
    n_used = (pend[-1:] // ROWS).astype(jnp.int32)

    xin = _dispatch(h2, dest, p_rows)
    y = _experts(xin, block_exp, n_used, w_gu.astype(BF16), b_gu.reshape(N_EXPERTS, 1, -1),
                 w_down.astype(BF16), b_down.reshape(N_EXPERTS, 1, -1))
    yg = _gather_rows(y, dest, t)
    out = _final(yg, gates.T, x1, gt2, g_post_ffn.reshape(1, d), seq)
    return out.reshape(bsz, seq, d)


def kernel(x, c, w_ada, b_ada, g_pre_mix, g_post_mix, g_pre_ffn, g_post_ffn, w_in, g_q, w_qb,
           g_kv, w_kvb, rel_bias, w_o, w_router, b_router, w_gu, b_gu, w_down, b_down):
    for l in range(w_ada.shape[0]):
        x = _layer(x, c, w_ada[l], b_ada[l], g_pre_mix[l], g_post_mix[l], g_pre_ffn[l], g_post_ffn[l],
                   w_in[l], g_q[l], w_qb[l], g_kv[l], w_kvb[l], rel_bias[l], w_o[l], w_router[l],
                   b_router[l], w_gu[l], b_gu[l], w_down[l], b_down[l])
    return x
```

```python
import functools
import math

import jax
import jax.numpy as jnp
from jax import lax
from jax.experimental import pallas as pl
from jax.experimental.pallas import tpu as pltpu
from jax.experimental.pallas import tpu_sc as plsc

F32 = jnp.float32
BF16 = jnp.bfloat16
U32 = jnp.uint32

D_MODEL = 1024
CHUNK = 64
EPS = 1e-6
MLA_HEADS = 8
MLA_NOPE = 64
MLA_ROPE = 32
MLA_V = 64
Q_LORA = 256
KV_LORA = 128
ROPE_THETA = 10000.0
CA_HEADS = 8
CA_DIM = 64
CA_LEFT = 8
REL_MAX = 256
N_EXPERTS = 32
TOP_K = 4
D_EXPERT = 1024
SWIGLU_LIMIT = 7.0
SWIGLU_ALPHA = 1.702

LANES = 128
HEAD_PAD = 128
LOG2E = math.log2(math.e)
NEG = -1e30
VMEM_LIMIT = 56 * 1024 * 1024

TM = 512
TQ = 512
TKW = 2 * TQ
MLA_HPS = 4
CQ = 256
CBAND = 3 * CQ
ROWS = 512
SC_CORES = 2
SC_WORKERS = SC_CORES * 16
SC_WIN = 128
CA_W = CA_HEADS * CA_DIM
W_IN_COLS = Q_LORA + KV_LORA + 2 * LANES + 3 * CA_W


def _cparams(sem, flags=None):
    return pltpu.CompilerParams(dimension_semantics=sem, vmem_limit_bytes=VMEM_LIMIT, flags=flags)


def _nt_dot(a, b):
    return lax.dot_general(a, b, (((1,), (1,)), ((), ())), preferred_element_type=F32)


def _rms(x, g):
    return x * lax.rsqrt(jnp.mean(x * x, axis=-1, keepdims=True) + EPS) * g


def _pack_rows(x):
    n = x.shape[1] // 2
    lo = lax.bitcast_convert_type(x[:, :n].astype(BF16).astype(F32), U32)
    hi = lax.bitcast_convert_type(x[:, n:].astype(BF16).astype(F32), U32)
    return (lo >> 16) | hi


def _unpack_rows(p):
    lo = lax.bitcast_convert_type(p << 16, F32)
    hi = lax.bitcast_convert_type(p & jnp.uint32(0xFFFF0000), F32)
    return lo, hi


def _ada_kernel(c_ref, w_ref, b_ref, o_ref):
    c = c_ref[...]
    a = (c / (1.0 + jnp.exp(-c))).astype(BF16)
    o_ref[...] = jnp.dot(a, w_ref[...].astype(BF16), preferred_element_type=F32) + b_ref[...]


def _ada(c, w, b):
    bsz, d = c.shape
    n = w.shape[1]
    tn = 1024
    return pl.pallas_call(
        _ada_kernel,
        grid=(n // tn,),
        in_specs=[pl.BlockSpec((bsz, d), lambda j: (0, 0)),
                  pl.BlockSpec((d, tn), lambda j: (0, j)),
                  pl.BlockSpec((1, tn), lambda j: (0, j))],
        out_specs=pl.BlockSpec((bsz, tn), lambda j: (0, j)),
        out_shape=jax.ShapeDtypeStruct((bsz, n), F32),
        compiler_params=_cparams(("arbitrary",)),
        name="ada",
    )(c, w, b.reshape(1, n))


def _proj_kernel(x_ref, sc_ref, sh_ref, g_ref, win_ref, gq_ref, wq1_ref, wq2_ref, gkv_ref,
                 wk_ref, wv_ref, vone_ref, cq_ref, sq_ref, ck_ref, sk_ref,
                 q_out, k_out, v_out, qc_out, kc_out, vc_out):
    h = _rms(x_ref[...], g_ref[...]) * (1.0 + sc_ref[0]) + sh_ref[0]
    z = jnp.dot(h.astype(BF16), win_ref[...], preferred_element_type=F32)
    o = 0
    cq = z[:, o:o + Q_LORA]; o += Q_LORA
    ckv = z[:, o:o + KV_LORA]; o += KV_LORA
    kr1 = z[:, o:o + LANES]; o += LANES
    kr2 = z[:, o:o + LANES]; o += LANES
    qc = z[:, o:o + CA_W]; o += CA_W
    kc = z[:, o:o + CA_W]; o += CA_W
    vc = z[:, o:o + CA_W]

    cqn = _rms(cq, gq_ref[...]).astype(BF16)
    q1 = jnp.dot(cqn, wq1_ref[...], preferred_element_type=F32)
    q2 = jnp.dot(cqn, wq2_ref[...], preferred_element_type=F32)
    ckvn = _rms(ckv, gkv_ref[...]).astype(BF16)
    kn = jnp.dot(ckvn, wk_ref[...], preferred_element_type=F32)
    v_out[...] = (jnp.dot(ckvn, wv_ref[...], preferred_element_type=F32) + vone_ref[...]).astype(BF16)

    cq_t, sq_t = cq_ref[...], sq_ref[...]
    krope = kr1 * ck_ref[...] + kr2 * sk_ref[...]
    for hd in range(MLA_HEADS):
        sl = slice(hd * HEAD_PAD, (hd + 1) * HEAD_PAD)
        q_out[:, sl] = (q1[:, sl] * cq_t + q2[:, sl] * sq_t).astype(BF16)
        k_out[:, sl] = (kn[:, sl] + krope).astype(BF16)

    qc_out[...] = (qc * (CA_DIM ** -0.5 * LOG2E)).astype(BF16)
    kc_out[...] = kc.astype(BF16)
    vc_out[...] = vc.astype(BF16)


def _proj(x2, sc1, sh1, g_pre, w, tabs, seq):
    t, d = x2.shape
    tpb = seq // TM
    row = lambda i: (i, 0)
    full = lambda i: (0, 0)
    per_b = lambda i: (i // tpb, 0, 0)
    pos = lambda i: (i % tpb, 0)
    hw = MLA_HEADS * HEAD_PAD
    in_specs = [
        pl.BlockSpec((TM, d), row),
        pl.BlockSpec((1, 1, d), per_b), pl.BlockSpec((1, 1, d), per_b),
        pl.BlockSpec((1, d), full),
        pl.BlockSpec((d, W_IN_COLS), full),
        pl.BlockSpec((1, Q_LORA), full),
        pl.BlockSpec((Q_LORA, hw), full), pl.BlockSpec((Q_LORA, hw), full),
        pl.BlockSpec((1, KV_LORA), full),
        pl.BlockSpec((KV_LORA, hw), full), pl.BlockSpec((KV_LORA, hw), full), pl.BlockSpec((1, hw), full),
        pl.BlockSpec((TM, LANES), pos), pl.BlockSpec((TM, LANES), pos),
        pl.BlockSpec((TM, LANES), pos), pl.BlockSpec((TM, LANES), pos),
    ]
    widths = (hw, hw, hw, CA_W, CA_W, CA_W)
    return pl.pallas_call(
        _proj_kernel,
        grid=(t // TM,),
        in_specs=in_specs,
        out_specs=[pl.BlockSpec((TM, n), row) for n in widths],
        out_shape=[jax.ShapeDtypeStruct((t, n), BF16) for n in widths],
        compiler_params=_cparams(("arbitrary",)),
        name="proj",
    )(x2, sc1, sh1, g_pre, w["w_in"], w["g_q"], w["wq1"], w["wq2"], w["g_kv"], w["wk"], w["wv"], w["v_one"],
      tabs["cq"], tabs["sq"], tabs["ck"], tabs["sk"])


def _mla_kernel(q_ref, k_ref, v_ref, o_ref):
    i = pl.program_id(2)
    heads = [slice(hh * HEAD_PAD, (hh + 1) * HEAD_PAD) for hh in range(MLA_HPS)]
    qs = [q_ref[0, :, hs] for hs in heads]

    def step(off, width, carry, masked=False):
        new = []
        for hh, hs in enumerate(heads):
            m, acc = carry[hh]
            s = _nt_dot(qs[hh], k_ref[0, pl.ds(off, width), hs])
            if masked:
                r = lax.broadcasted_iota(jnp.int32, (TQ, width), 0) // CHUNK
                c = lax.broadcasted_iota(jnp.int32, (TQ, width), 1) // CHUNK
                s = jnp.where(c <= r, s, NEG)
            m_new = jnp.maximum(m, jnp.max(s, axis=-1, keepdims=True))
            p = jnp.exp2(s - m_new).astype(BF16)
            pv = jnp.dot(p, v_ref[0, pl.ds(off, width), hs], preferred_element_type=F32)
            new.append((m_new, jnp.exp2(m - m_new) * acc + pv))
        return tuple(new)

    init = tuple((jnp.full((TQ, 1), NEG, F32), jnp.zeros((TQ, HEAD_PAD), F32)) for _ in heads)
    carry = lax.fori_loop(0, i // 2, lambda j, c: step(pl.multiple_of(j * TKW, TKW), TKW, c), init)
    carry = lax.fori_loop(0, i % 2, lambda _, c: step(pl.multiple_of((i - 1) * TQ, TQ), TQ, c), carry)
    carry = step(pl.multiple_of(i * TQ, TQ), TQ, carry, True)
    lane = lax.broadcasted_iota(jnp.int32, (TQ, LANES), 1)
    for pp in range(MLA_HPS // 2):
        acc0, acc1 = carry[2 * pp][1], carry[2 * pp + 1][1]
        o0 = acc0 / acc0[:, MLA_V:MLA_V + 1]
        o1 = acc1 / acc1[:, MLA_V:MLA_V + 1]
        o_ref[0, :, pp * LANES:(pp + 1) * LANES] = jnp.where(
            lane < MLA_V, o0, pltpu.roll(o1, MLA_V, axis=1)).astype(BF16)


def _mla(q, k, v):
    bsz, seq, _ = q.shape
    groups = MLA_HEADS // MLA_HPS
    return pl.pallas_call(
        _mla_kernel,
        grid=(bsz, groups, seq // TQ),
        in_specs=[pl.BlockSpec((1, TQ, MLA_HPS * HEAD_PAD), lambda b, p, i: (b, i, p)),
                  pl.BlockSpec((1, seq, MLA_HPS * HEAD_PAD), lambda b, p, i: (b, 0, p)),
                  pl.BlockSpec((1, seq, MLA_HPS * HEAD_PAD), lambda b, p, i: (b, 0, p))],
        out_specs=pl.BlockSpec((1, TQ, MLA_HPS * MLA_V), lambda b, p, i: (b, i, p)),
        out_shape=jax.ShapeDtypeStruct((bsz, seq, MLA_HEADS * MLA_V), BF16),
        compiler_params=_cparams(("arbitrary", "arbitrary", "arbitrary")),
        name="mla",
    )(q, k, v)


def _chunk_kernel(q_ref, k0_ref, k1_ref, k2_ref, v0_ref, v1_ref, v2_ref, bias_ref, o_ref):
    lane = lax.broadcasted_iota(jnp.int32, (CQ, LANES), 1)
    lo = lane < CA_DIM
    k_refs = (k0_ref, k1_ref, k2_ref)
    v_refs = (v0_ref, v1_ref, v2_ref)
    for p in range(CA_HEADS // 2):
        sl = slice(p * LANES, (p + 1) * LANES)
        q = q_ref[0, :, sl]
        ks = [kr[0, :, sl] for kr in k_refs]
        vs = [vr[0, :, sl] for vr in v_refs]
        outs = []
        for hh in range(2):
            mine = lo if hh == 0 else jnp.logical_not(lo)
            den_lane = CA_DIM if hh == 0 else 0
            ones_col = (lane == den_lane).astype(BF16)
            qm = jnp.where(mine, q, jnp.zeros_like(q))
            s = jnp.concatenate([_nt_dot(qm, kb) for kb in ks], axis=1) + bias_ref[0, 2 * p + hh]
            m = jnp.max(s, axis=-1, keepdims=True)
            pb = jnp.exp2(s - m).astype(BF16)
            o = None
            for cb in range(3):
                part = jnp.dot(pb[:, cb * CQ:(cb + 1) * CQ], jnp.where(mine, vs[cb], ones_col),
                               preferred_element_type=F32)
                o = part if o is None else o + part
            outs.append(o / o[:, den_lane:den_lane + 1])
        o_ref[0, :, sl] = jnp.where(lo, outs[0], outs[1]).astype(BF16)


def _chunk_attn(qc, kc, vc, bias):
    bsz, seq, w = qc.shape
    blk = lambda off: pl.BlockSpec((1, CQ, w), lambda b, i: (b, jnp.maximum(i + off, 0), 0))
    return pl.pallas_call(
        _chunk_kernel,
        grid=(bsz, seq // CQ),
        in_specs=[blk(0), blk(-2), blk(-1), blk(0), blk(-2), blk(-1), blk(0),
                  pl.BlockSpec((1, CA_HEADS, CQ, CBAND), lambda b, i: (jnp.minimum(i, 2), 0, 0, 0))],
        out_specs=pl.BlockSpec((1, CQ, w), lambda b, i: (b, i, 0)),
        out_shape=jax.ShapeDtypeStruct((bsz, seq, w), BF16),
        compiler_params=_cparams(("arbitrary", "arbitrary")),
        name="chunk_attn",
    )(qc, kc, kc, kc, vc, vc, vc, bias)


def _post_kernel(oa_ref, ob_ref, x_ref, gt_ref, sc_ref, sh_ref, gpost_ref, gpre_ref,
                 woa_ref, wob_ref, wr_ref, br_ref,
                 x1_out, h2_out, idx_out, gate_out, rank_out, cnt_out, carry_ref):
    t = pl.program_id(0)

    @pl.when(t == 0)
    def _():
        carry_ref[...] = jnp.zeros_like(carry_ref)

    o = jnp.dot(oa_ref[...], woa_ref[...], preferred_element_type=F32)
    o += jnp.dot(ob_ref[...], wob_ref[...], preferred_element_type=F32)
    x1 = x_ref[...] + gt_ref[0] * _rms(o, gpost_ref[...])
    x1_out[...] = x1
    h2f = _rms(x1, gpre_ref[...]) * (1.0 + sc_ref[0]) + sh_ref[0]
    h2_out[...] = _pack_rows(h2f)
    h2 = h2f.astype(BF16)

    logits = _nt_dot(wr_ref[...], h2) + br_ref[...]
    eid = lax.broadcasted_iota(jnp.int32, (N_EXPERTS, TM), 0)
    vals, idxs = [], []
    work = logits
    for _k in range(TOP_K):
        m = jnp.max(work, axis=0, keepdims=True)
        ix = jnp.min(jnp.where(work == m, eid, N_EXPERTS), axis=0, keepdims=True)
        work = jnp.where(eid == ix, -jnp.inf, work)
        vals.append(m)
        idxs.append(ix)
    es = [jnp.exp(v - vals[0]) for v in vals]
    den = es[0] + es[1] + es[2] + es[3]
    gate_out[...] = jnp.concatenate([e / den for e in es], axis=0)
    idx_out[...] = jnp.concatenate(idxs, axis=0)

    sel = (eid == idxs[0]) | (eid == idxs[1]) | (eid == idxs[2]) | (eid == idxs[3])
    self32 = sel.astype(F32)
    rr = lax.broadcasted_iota(jnp.int32, (TM, TM), 0)
    cc = lax.broadcasted_iota(jnp.int32, (TM, TM), 1)
    upper = (rr < cc).astype(BF16)
    before = jnp.dot(self32.astype(BF16), upper, preferred_element_type=F32)
    before = before + carry_ref[:, 0:1]
    ranks = [jnp.sum(jnp.where(eid == ix, before, 0.0), axis=0, keepdims=True) for ix in idxs]
    rank_out[...] = jnp.concatenate(ranks, axis=0).astype(jnp.int32)
    carry_ref[...] = carry_ref[...] + jnp.sum(self32, axis=1, keepdims=True)
    cnt_out[...] = carry_ref[...]


def _post(oa, ob, x2, gt1, sc2, sh2, g_post, g_pre, w, seq):
    t, d = x2.shape
    tpb = seq // TM
    row = lambda i: (i, 0)
    col = lambda i: (0, i)
    full = lambda i: (0, 0)
    per_b = lambda i: (i // tpb, 0, 0)
    hw = oa.shape[1]
    in_specs = [
        pl.BlockSpec((TM, hw), row), pl.BlockSpec((TM, hw), row), pl.BlockSpec((TM, d), row),
        pl.BlockSpec((1, 1, d), per_b), pl.BlockSpec((1, 1, d), per_b), pl.BlockSpec((1, 1, d), per_b),
        pl.BlockSpec((1, d), full), pl.BlockSpec((1, d), full),
        pl.BlockSpec((hw, d), full), pl.BlockSpec((hw, d), full),
        pl.BlockSpec((N_EXPERTS, d), full), pl.BlockSpec((N_EXPERTS, 1), full),
    ]
    out_specs = [
        pl.BlockSpec((TM, d), row), pl.BlockSpec((TM, d // 2), row),
        pl.BlockSpec((TOP_K, TM), col), pl.BlockSpec((TOP_K, TM), col), pl.BlockSpec((TOP_K, TM), col),
        pl.BlockSpec((N_EXPERTS, LANES), full),
    ]
    out_shape = [
        jax.ShapeDtypeStruct((t, d), F32), jax.ShapeDtypeStruct((t, d // 2), U32),
        jax.ShapeDtypeStruct((TOP_K, t), jnp.int32), jax.ShapeDtypeStruct((TOP_K, t), F32),
        jax.ShapeDtypeStruct((TOP_K, t), jnp.int32),
        jax.ShapeDtypeStruct((N_EXPERTS, LANES), F32),
    ]
    return pl.pallas_call(
        _post_kernel,
        grid=(t // TM,),
        in_specs=in_specs,
        out_specs=out_specs,
        out_shape=out_shape,
        scratch_shapes=[pltpu.VMEM((N_EXPERTS, LANES), F32)],
        compiler_params=_cparams(("arbitrary",)),
        name="post",
    )(oa, ob, x2, gt1, sc2, sh2, g_post, g_pre, w["wo_a"], w["wo_b"], w["wr_t"], w["b_r"])


def _expert_kernel(be_ref, nu_ref, x_ref, wgu_ref, bgu_ref, wd_ref, bd_ref, y_ref, wgu_bf, wd_bf):
    j = pl.program_id(0)
    used = j < nu_ref[0]

    @pl.when(used & ((j == 0) | (be_ref[j] != be_ref[jnp.maximum(j - 1, 0)])))
    def _():
        wgu_bf[...] = wgu_ref[0].astype(BF16)
        wd_bf[...] = wd_ref[0].astype(BF16)

    @pl.when(used)
    def _():
        x_lo, x_hi = _unpack_rows(x_ref[...])
        half = x_lo.shape[1]
        gu = jnp.dot(x_lo.astype(BF16), wgu_bf[:half], preferred_element_type=F32)
        gu += jnp.dot(x_hi.astype(BF16), wgu_bf[half:], preferred_element_type=F32)
        gu += bgu_ref[0]
        gate = jnp.minimum(gu[:, :D_EXPERT], SWIGLU_LIMIT)
        up = jnp.clip(gu[:, D_EXPERT:], -SWIGLU_LIMIT, SWIGLU_LIMIT)
        glu = gate / (1.0 + jnp.exp(-SWIGLU_ALPHA * gate))
        act = ((up + 1.0) * glu).astype(BF16)
        y = jnp.dot(act, wd_bf[...], preferred_element_type=F32) + bd_ref[0]
        y_ref[...] = _pack_rows(y)

    @pl.when(j >= nu_ref[0])
    def _():
        y_ref[...] = jnp.zeros_like(y_ref)


def _experts(xin, block_exp, n_used, wgu, bgu, wd, bd):
    p_rows, dw = xin.shape
    d = 2 * dw
    nb = p_rows // ROWS
    f2 = wgu.shape[2]
    grid_spec = pltpu.PrefetchScalarGridSpec(
        num_scalar_prefetch=2,
        grid=(nb,),
        in_specs=[
            pl.BlockSpec((ROWS, dw), lambda j, be, nu: (jnp.minimum(j, nu[0] - 1), 0)),
            pl.BlockSpec((1, d, f2), lambda j, be, nu: (be[j], 0, 0)),
            pl.BlockSpec((1, 1, f2), lambda j, be, nu: (be[j], 0, 0)),
            pl.BlockSpec((1, f2 // 2, d), lambda j, be, nu: (be[j], 0, 0)),
            pl.BlockSpec((1, 1, d), lambda j, be, nu: (be[j], 0, 0)),
        ],
        out_specs=pl.BlockSpec((ROWS, dw), lambda j, be, nu: (j, 0)),
        scratch_shapes=[pltpu.VMEM((d, f2), BF16), pltpu.VMEM((f2 // 2, d), BF16)],
    )
    return pl.pallas_call(
        _expert_kernel,
        grid_spec=grid_spec,
        out_shape=jax.ShapeDtypeStruct((p_rows, dw), U32),
        compiler_params=_cparams(("arbitrary",)),
        name="experts",
    )(block_exp, n_used, xin, wgu, bgu, wd, bd)


def _final_kernel(yg_ref, g_ref, x1_ref, gt_ref, gpost_ref, o_ref):
    g = g_ref[...]
    f_lo, f_hi = None, None
    for k in range(TOP_K):
        lo, hi = _unpack_rows(yg_ref[k])
        gk = g[:, k:k + 1]
        f_lo = lo * gk if f_lo is None else f_lo + lo * gk
        f_hi = hi * gk if f_hi is None else f_hi + hi * gk
    f = jnp.concatenate([f_lo, f_hi], axis=1)
    o_ref[...] = x1_ref[...] + gt_ref[0] * _rms(f, gpost_ref[...])


def _final(yg, gates_t, x1, gt2, g_post, seq):
    t, d = x1.shape
    tpb = seq // TM
    row = lambda i: (i, 0)
    return pl.pallas_call(
        _final_kernel,
        grid=(t // TM,),
        in_specs=[pl.BlockSpec((TOP_K, TM, d // 2), lambda i: (0, i, 0)),
                  pl.BlockSpec((TM, TOP_K), row),
                  pl.BlockSpec((TM, d), row),
                  pl.BlockSpec((1, 1, d), lambda i: (i // tpb, 0, 0)),
                  pl.BlockSpec((1, d), lambda i: (0, 0))],
        out_specs=pl.BlockSpec((TM, d), row),
        out_shape=jax.ShapeDtypeStruct((t, d), F32),
        compiler_params=_cparams(("arbitrary",)),
        name="final",
    )(yg, gates_t, x1, gt2, g_post)


def _sc_mesh():
    return plsc.VectorSubcoreMesh(core_axis_name="c", subcore_axis_name="s")


def _sc_worker():
    return lax.axis_index("s") * SC_CORES + lax.axis_index("c")


def _dispatch(h2, dest, p_rows):
    t, dw = h2.shape
    per_w = t // SC_WORKERS
    n_win = per_w // SC_WIN

    @functools.partial(
        pl.kernel, mesh=_sc_mesh(),
        out_type=jax.ShapeDtypeStruct((p_rows, dw), h2.dtype),
        scratch_types=[pltpu.VMEM((TOP_K, SC_WIN), jnp.int32),
                       pltpu.VMEM((SC_WIN, dw), h2.dtype),
                       pltpu.SemaphoreType.DMA],
        name="dispatch",
    )
    def run(h_hbm, d_hbm, o_hbm, idx_v, rows_v, sem):
        wid = _sc_worker()

        @pl.loop(0, n_win)
        def _(wi):
            base = pl.multiple_of(wid * per_w + wi * SC_WIN, SC_WIN)
            for k in range(TOP_K):
                pltpu.sync_copy(d_hbm.at[pl.ds(k * t + base, SC_WIN)], idx_v.at[k])
            pltpu.sync_copy(h_hbm.at[pl.ds(base, SC_WIN)], rows_v)
            for k in range(TOP_K):
                pltpu.async_copy(rows_v, o_hbm.at[idx_v.at[k]], sem).wait()

    return run(h2, dest)


def _gather_rows(y, dest, t):
    _, dw = y.shape
    per_w = t // SC_WORKERS
    n_win = per_w // SC_WIN

    @functools.partial(
        pl.kernel, mesh=_sc_mesh(),
        out_type=jax.ShapeDtypeStruct((TOP_K, t, dw), y.dtype),
        scratch_types=[pltpu.VMEM((SC_WIN,), jnp.int32),
                       pltpu.VMEM((SC_WIN, dw), y.dtype),
                       pltpu.SemaphoreType.DMA],
        name="gather_rows",
    )
    def run(y_hbm, d_hbm, o_hbm, idx_v, rows_v, sem):
        wid = _sc_worker()

        @pl.loop(0, n_win)
        def _(wi):
            base = pl.multiple_of(wid * per_w + wi * SC_WIN, SC_WIN)
            for k in range(TOP_K):
                pltpu.sync_copy(d_hbm.at[pl.ds(k * t + base, SC_WIN)], idx_v)
                pltpu.async_copy(y_hbm.at[idx_v], rows_v, sem).wait()
                pltpu.sync_copy(rows_v, o_hbm.at[k, pl.ds(base, SC_WIN)])

    return run(y, dest)


def _prep_weights(w_in, g_q, w_qb, g_kv, w_kvb, w_o, w_router, b_router):
    d = w_in.shape[0]
    o = 0
    w_cq = w_in[:, o:o + Q_LORA]; o += Q_LORA
    w_ckv = w_in[:, o:o + KV_LORA]; o += KV_LORA
    w_kr = w_in[:, o:o + MLA_ROPE]; o += MLA_ROPE
    w_ca = w_in[:, o:]
    half = MLA_ROPE // 2
    zpad = lambda n: jnp.zeros((d, n), w_in.dtype)
    x1, x2 = w_kr[:, :half], w_kr[:, half:]
    tail = HEAD_PAD - MLA_NOPE - MLA_ROPE
    w_kr1 = jnp.concatenate([zpad(MLA_NOPE), x1, x2, zpad(tail)], axis=1)
    w_kr2 = jnp.concatenate([zpad(MLA_NOPE), -x2, x1, zpad(tail)], axis=1)
    w_in_all = jnp.concatenate([w_cq, w_ckv, w_kr1, w_kr2, w_ca], axis=1).astype(BF16)

    wq = w_qb.reshape(Q_LORA, MLA_HEADS, MLA_NOPE + MLA_ROPE)
    qn, q1, q2 = wq[..., :MLA_NOPE], wq[..., MLA_NOPE:MLA_NOPE + half], wq[..., MLA_NOPE + half:]
    zq = jnp.zeros((Q_LORA, MLA_HEADS, tail), w_qb.dtype)
    wq1 = jnp.concatenate([qn, q1, q2, zq], axis=-1).reshape(Q_LORA, -1).astype(BF16)
    wq2 = jnp.concatenate([jnp.zeros_like(qn), -q2, q1, zq], axis=-1).reshape(Q_LORA, -1).astype(BF16)

    wkv = w_kvb.reshape(KV_LORA, MLA_HEADS, MLA_NOPE + MLA_V)
    kn = wkv[..., :MLA_NOPE]
    wk = jnp.concatenate([kn, jnp.zeros((KV_LORA, MLA_HEADS, HEAD_PAD - MLA_NOPE), w_kvb.dtype)],
                         axis=-1).reshape(KV_LORA, -1).astype(BF16)
    wv = jnp.concatenate([wkv[..., MLA_NOPE:], jnp.zeros((KV_LORA, MLA_HEADS, HEAD_PAD - MLA_V), w_kvb.dtype)],
                         axis=-1).reshape(KV_LORA, -1).astype(BF16)
    v_one = jnp.tile((jnp.arange(HEAD_PAD) == MLA_V).astype(F32), MLA_HEADS).reshape(1, -1)
    mla_w = MLA_HEADS * MLA_V
    return {
        "w_in": w_in_all, "g_q": g_q.reshape(1, -1), "wq1": wq1, "wq2": wq2,
        "g_kv": g_kv.reshape(1, -1), "wk": wk, "wv": wv, "v_one": v_one,
        "wo_a": w_o[:mla_w].astype(BF16), "wo_b": w_o[mla_w:].astype(BF16),
        "wr_t": w_router.T.astype(BF16), "b_r": b_router.reshape(-1, 1),
    }


def _rope_tables(seq):
    half = MLA_ROPE // 2
    inv_freq = ROPE_THETA ** (-jnp.arange(half, dtype=F32) / half)
    ang = jnp.arange(seq, dtype=F32)[:, None] * inv_freq[None, :]
    cos, sin = jnp.cos(ang), jnp.sin(ang)
    tail = HEAD_PAD - MLA_NOPE - MLA_ROPE
    ones = jnp.ones((seq, MLA_NOPE), F32)
    zn = jnp.zeros((seq, MLA_NOPE), F32)
    zt = jnp.zeros((seq, tail), F32)
    qs = (MLA_NOPE + MLA_ROPE) ** -0.5 * LOG2E
    return {
        "cq": jnp.concatenate([ones, cos, cos, zt], axis=1) * qs,
        "sq": jnp.concatenate([zn, sin, sin, zt], axis=1) * qs,
        "ck": jnp.concatenate([zn, cos, cos, zt], axis=1),
        "sk": jnp.concatenate([zn, sin, sin, zt], axis=1),
    }


def _bias_table(rel_bias):
    n = CQ + CBAND - 1
    rel = (CBAND - 1) - jnp.arange(n)
    diag = rel_bias.astype(F32)[:, jnp.clip(rel, -(CHUNK - 1), REL_MAX) + (CHUNK - 1)] * LOG2E
    diag = jnp.concatenate([diag, jnp.zeros((diag.shape[0], 1), F32)], axis=1)
    b = jnp.tile(diag, (1, CQ))[:, :CQ * n].reshape(-1, CQ, n)[:, :, CQ - 1:]
    r = jnp.arange(CQ)[:, None]
    c = jnp.arange(CBAND)[None, :]
    dchunk = r // CHUNK - (c // CHUNK - CA_LEFT)
    visible = (dchunk >= 0) & (dchunk <= CA_LEFT)
    exists = (c // CQ)[None] >= (2 - jnp.arange(3))[:, None, None]
    return jnp.where((visible[None] & exists)[:, None], b[None], NEG)


def _layer(x, c, w_ada, b_ada, g_pre_mix, g_post_mix, g_pre_ffn, g_post_ffn, w_in, g_q, w_qb,
           g_kv, w_kvb, rel_bias, w_o, w_router, b_router, w_gu, b_gu, w_down, b_down):
    bsz, seq, d = x.shape
    t = bsz * seq
    mod = _ada(c, w_ada, b_ada).reshape(bsz, 6, 1, d)
    sh1, sc1, gt1, sh2, sc2, gt2 = [mod[:, k] for k in range(6)]
    w = _prep_weights(w_in, g_q, w_qb, g_kv, w_kvb, w_o, w_router, b_router)
    tabs = _rope_tables(seq)
    x2 = x.reshape(t, d)

    q, k, v, qc, kc, vc = _proj(x2, sc1, sh1, g_pre_mix.reshape(1, d), w, tabs, seq)
    shp = lambda a: a.reshape(bsz, seq, a.shape[-1])
    oa = _mla(shp(q), shp(k), shp(v)).reshape(t, -1)
    ob = _chunk_attn(shp(qc), shp(kc), shp(vc), _bias_table(rel_bias)).reshape(t, -1)

    x1, h2, top_idx, gates, rank, cnt = _post(oa, ob, x2, gt1, sc2, sh2, g_post_mix.reshape(1, d),
                                              g_pre_ffn.reshape(1, d), w, seq)

    counts = cnt[:, 0].astype(jnp.int32)
    padded = ((counts + ROWS - 1) // ROWS) * ROWS
    pend = jnp.cumsum(padded)
    pstart = pend - padded
    eids = jnp.arange(N_EXPERTS, dtype=jnp.int32)[:, None, None]
    dest = (jnp.sum(jnp.where(top_idx[None] == eids, pstart[:, None, None], 0), axis=0) + rank).reshape(-1)
    p_rows = t * TOP_K + N_EXPERTS * ROWS
    nb = p_rows // ROWS
    block_start = jnp.arange(nb, dtype=jnp.int32) * ROWS
    block_exp = jnp.minimum(jnp.sum(pend[None, :] <= block_start[:, None], axis=1), N_EXPERTS - 1).astype(jnp.int32)
    n_used = (pend[-1:] // ROWS).astype(jnp.int32)

    xin = _dispatch(h2, dest, p_rows)
    y = _experts(xin, block_exp, n_used, w_gu, b_gu.reshape(N_EXPERTS, 1, -1),
                 w_down, b_down.reshape(N_EXPERTS, 1, -1))
    yg = _gather_rows(y, dest, t)
    out = _final(yg, gates.T, x1, gt2, g_post_ffn.reshape(1, d), seq)
    return out.reshape(bsz, seq, d)


def kernel(x, c, w_ada, b_ada, g_pre_mix, g_post_mix, g_pre_ffn, g_post_ffn, w_in, g_q, w_qb,
           g_kv, w_kvb, rel_bias, w_o, w_router, b_router, w_gu, b_gu, w_down, b_down):
    for l in range(w_ada.shape[0]):
        x = _layer(x, c, w_ada[l], b_ada[l], g_pre_mix[l], g_post_mix[l], g_pre_ffn[l], g_post_ffn[l],
                   w_in[l], g_q[l], w_qb[l], g_kv[l], w_kvb[l], rel_bias[l], w_o[l], w_router[l],
                   b_router[l], w_gu[l], b_gu[l], w_down[l], b_down[l])
    return x
```

```python
import functools
import math

import jax
import jax.numpy as jnp
from jax import lax
from jax.experimental import pallas as pl
from jax.experimental.pallas import tpu as pltpu
from jax.experimental.pallas import tpu_sc as plsc

F32 = jnp.float32
BF16 = jnp.bfloat16
U32 = jnp.uint32

D_MODEL = 1024
CHUNK = 64
EPS = 1e-6
MLA_HEADS = 8
MLA_NOPE = 64
MLA_ROPE = 32
MLA_V = 64
Q_LORA = 256
KV_LORA = 128
ROPE_THETA = 10000.0
CA_HEADS = 8
CA_DIM = 64
CA_LEFT = 8
REL_MAX = 256
N_EXPERTS = 32
TOP_K = 4
D_EXPERT = 1024
SWIGLU_LIMIT = 7.0
SWIGLU_ALPHA = 1.702

LANES = 128
HEAD_PAD = 128
LOG2E = math.log2(math.e)
NEG = -1e30
VMEM_LIMIT = 56 * 1024 * 1024

TM = 512
TQ = 512
TKW = 2 * TQ
MLA_HPS = 4
CQ = 256
CBAND = 3 * CQ
ROWS = 512
EXPERT_SUB = 256
MOE_PARTS = 2
SC_CORES = 2
SC_WORKERS = SC_CORES * 16
SC_WIN = 128
CA_W = CA_HEADS * CA_DIM
W_IN_COLS = Q_LORA + KV_LORA + 2 * LANES + 3 * CA_W


def _cparams(sem, flags=None):
    return pltpu.CompilerParams(dimension_semantics=sem, vmem_limit_bytes=VMEM_LIMIT, flags=flags)


def _nt_dot(a, b):
    return lax.dot_general(a, b, (((1,), (1,)), ((), ())), preferred_element_type=F32)


def _rms(x, g):
    return x * lax.rsqrt(jnp.mean(x * x, axis=-1, keepdims=True) + EPS) * g


def _pack_rows(x):
    n = x.shape[1] // 2
    lo = lax.bitcast_convert_type(x[:, :n].astype(BF16).astype(F32), U32)
    hi = lax.bitcast_convert_type(x[:, n:].astype(BF16).astype(F32), U32)
    return (lo >> 16) | hi


def _unpack_rows(p):
    lo = lax.bitcast_convert_type(p << 16, F32)
    hi = lax.bitcast_convert_type(p & jnp.uint32(0xFFFF0000), F32)
    return lo, hi


def _ada_kernel(c_ref, w_ref, b_ref, o_ref):
    c = c_ref[...]
    a = (c / (1.0 + jnp.exp(-c))).astype(BF16)
    o_ref[...] = jnp.dot(a, w_ref[...].astype(BF16), preferred_element_type=F32) + b_ref[...]


def _ada(c, w, b):
    bsz, d = c.shape
    n = w.shape[1]
    tn = 1024
    return pl.pallas_call(
        _ada_kernel,
        grid=(n // tn,),
        in_specs=[pl.BlockSpec((bsz, d), lambda j: (0, 0)),
                  pl.BlockSpec((d, tn), lambda j: (0, j)),
                  pl.BlockSpec((1, tn), lambda j: (0, j))],
        out_specs=pl.BlockSpec((bsz, tn), lambda j: (0, j)),
        out_shape=jax.ShapeDtypeStruct((bsz, n), F32),
        compiler_params=_cparams(("arbitrary",)),
        name="ada",
    )(c, w, b.reshape(1, n))


def _proj_kernel(x_ref, sc_ref, sh_ref, g_ref, win_ref, gq_ref, wq1_ref, wq2_ref, gkv_ref,
                 wk_ref, wv_ref, vone_ref, cq_ref, sq_ref, ck_ref, sk_ref,
                 q_out, k_out, v_out, qc_out, kc_out, vc_out):
    h = _rms(x_ref[...], g_ref[...]) * (1.0 + sc_ref[0]) + sh_ref[0]
    z = jnp.dot(h.astype(BF16), win_ref[...], preferred_element_type=F32)
    o = 0
    cq = z[:, o:o + Q_LORA]; o += Q_LORA
    ckv = z[:, o:o + KV_LORA]; o += KV_LORA
    kr1 = z[:, o:o + LANES]; o += LANES
    kr2 = z[:, o:o + LANES]; o += LANES
    qc = z[:, o:o + CA_W]; o += CA_W
    kc = z[:, o:o + CA_W]; o += CA_W
    vc = z[:, o:o + CA_W]

    cqn = _rms(cq, gq_ref[...]).astype(BF16)
    q1 = jnp.dot(cqn, wq1_ref[...], preferred_element_type=F32)
    q2 = jnp.dot(cqn, wq2_ref[...], preferred_element_type=F32)
    ckvn = _rms(ckv, gkv_ref[...]).astype(BF16)
    kn = jnp.dot(ckvn, wk_ref[...], preferred_element_type=F32)
    v_out[...] = (jnp.dot(ckvn, wv_ref[...], preferred_element_type=F32) + vone_ref[...]).astype(BF16)

    cq_t, sq_t = cq_ref[...], sq_ref[...]
    krope = kr1 * ck_ref[...] + kr2 * sk_ref[...]
    for hd in range(MLA_HEADS):
        sl = slice(hd * HEAD_PAD, (hd + 1) * HEAD_PAD)
        q_out[:, sl] = (q1[:, sl] * cq_t + q2[:, sl] * sq_t).astype(BF16)
        k_out[:, sl] = (kn[:, sl] + krope).astype(BF16)

    qc_out[...] = (qc * (CA_DIM ** -0.5 * LOG2E)).astype(BF16)
    kc_out[...] = kc.astype(BF16)
    vc_out[...] = vc.astype(BF16)


def _proj(x2, sc1, sh1, g_pre, w, tabs, seq):
    t, d = x2.shape
    tpb = seq // TM
    row = lambda i: (i, 0)
    full = lambda i: (0, 0)
    per_b = lambda i: (i // tpb, 0, 0)
    pos = lambda i: (i % tpb, 0)
    hw = MLA_HEADS * HEAD_PAD
    in_specs = [
        pl.BlockSpec((TM, d), row),
        pl.BlockSpec((1, 1, d), per_b), pl.BlockSpec((1, 1, d), per_b),
        pl.BlockSpec((1, d), full),
        pl.BlockSpec((d, W_IN_COLS), full),
        pl.BlockSpec((1, Q_LORA), full),
        pl.BlockSpec((Q_LORA, hw), full), pl.BlockSpec((Q_LORA, hw), full),
        pl.BlockSpec((1, KV_LORA), full),
        pl.BlockSpec((KV_LORA, hw), full), pl.BlockSpec((KV_LORA, hw), full), pl.BlockSpec((1, hw), full),
        pl.BlockSpec((TM, LANES), pos), pl.BlockSpec((TM, LANES), pos),
        pl.BlockSpec((TM, LANES), pos), pl.BlockSpec((TM, LANES), pos),
    ]
    widths = (hw, hw, hw, CA_W, CA_W, CA_W)
    return pl.pallas_call(
        _proj_kernel,
        grid=(t // TM,),
        in_specs=in_specs,
        out_specs=[pl.BlockSpec((TM, n), row) for n in widths],
        out_shape=[jax.ShapeDtypeStruct((t, n), BF16) for n in widths],
        compiler_params=_cparams(("arbitrary",)),
        name="proj",
    )(x2, sc1, sh1, g_pre, w["w_in"], w["g_q"], w["wq1"], w["wq2"], w["g_kv"], w["wk"], w["wv"], w["v_one"],
      tabs["cq"], tabs["sq"], tabs["ck"], tabs["sk"])


def _mla_kernel(q_ref, k_ref, v_ref, o_ref):
    i = pl.program_id(2)
    heads = [slice(hh * HEAD_PAD, (hh + 1) * HEAD_PAD) for hh in range(MLA_HPS)]
    qs = [q_ref[0, :, hs] for hs in heads]

    def step(off, width, carry, masked=False):
        scores = [_nt_dot(qs[hh], k_ref[0, pl.ds(off, width), hs]) for hh, hs in enumerate(heads)]
        if masked:
            r = lax.broadcasted_iota(jnp.int32, (TQ, width), 0) // CHUNK
            c = lax.broadcasted_iota(jnp.int32, (TQ, width), 1) // CHUNK
            scores = [jnp.where(c <= r, s, NEG) for s in scores]
        new = []
        for hh, hs in enumerate(heads):
            m, acc = carry[hh]
            m_new = jnp.maximum(m, jnp.max(scores[hh], axis=-1, keepdims=True))
            p = jnp.exp2(scores[hh] - m_new).astype(BF16)
            pv = jnp.dot(p, v_ref[0, pl.ds(off, width), hs], preferred_element_type=F32)
            new.append((m_new, jnp.exp2(m - m_new) * acc + pv))
        return tuple(new)

    init = tuple((jnp.full((TQ, 1), NEG, F32), jnp.zeros((TQ, HEAD_PAD), F32)) for _ in heads)
    carry = lax.fori_loop(0, i // 2, lambda j, c: step(pl.multiple_of(j * TKW, TKW), TKW, c), init)
    carry = lax.fori_loop(0, i % 2, lambda _, c: step(pl.multiple_of((i - 1) * TQ, TQ), TQ, c), carry)
    carry = step(pl.multiple_of(i * TQ, TQ), TQ, carry, True)
    lane = lax.broadcasted_iota(jnp.int32, (TQ, LANES), 1)
    for pp in range(MLA_HPS // 2):
        acc0, acc1 = carry[2 * pp][1], carry[2 * pp + 1][1]
        o0 = acc0 / acc0[:, MLA_V:MLA_V + 1]
        o1 = acc1 / acc1[:, MLA_V:MLA_V + 1]
        o_ref[0, :, pp * LANES:(pp + 1) * LANES] = jnp.where(
            lane < MLA_V, o0, pltpu.roll(o1, MLA_V, axis=1)).astype(BF16)


def _mla(q, k, v):
    bsz, seq, _ = q.shape
    groups = MLA_HEADS // MLA_HPS
    return pl.pallas_call(
        _mla_kernel,
        grid=(bsz, groups, seq // TQ),
        in_specs=[pl.BlockSpec((1, TQ, MLA_HPS * HEAD_PAD), lambda b, p, i: (b, i, p)),
                  pl.BlockSpec((1, seq, MLA_HPS * HEAD_PAD), lambda b, p, i: (b, 0, p)),
                  pl.BlockSpec((1, seq, MLA_HPS * HEAD_PAD), lambda b, p, i: (b, 0, p))],
        out_specs=pl.BlockSpec((1, TQ, MLA_HPS * MLA_V), lambda b, p, i: (b, i, p)),
        out_shape=jax.ShapeDtypeStruct((bsz, seq, MLA_HEADS * MLA_V), BF16),
        compiler_params=_cparams(("arbitrary", "arbitrary", "arbitrary")),
        name="mla",
    )(q, k, v)


def _chunk_kernel(q_ref, k0_ref, k1_ref, k2_ref, v0_ref, v1_ref, v2_ref, bias_ref, o_ref):
    lane = lax.broadcasted_iota(jnp.int32, (CQ, LANES), 1)
    lo = lane < CA_DIM
    k_refs = (k0_ref, k1_ref, k2_ref)
    v_refs = (v0_ref, v1_ref, v2_ref)
    scores = []
    for p in range(CA_HEADS // 2):
        sl = slice(p * LANES, (p + 1) * LANES)
        q = q_ref[0, :, sl]
        for hh in range(2):
            mine = lo if hh == 0 else jnp.logical_not(lo)
            qm = jnp.where(mine, q, jnp.zeros_like(q))
            scores.append(jnp.concatenate([_nt_dot(qm, kr[0, :, sl]) for kr in k_refs], axis=1))
    for p in range(CA_HEADS // 2):
        sl = slice(p * LANES, (p + 1) * LANES)
        vs = [vr[0, :, sl] for vr in v_refs]
        outs = []
        for hh in range(2):
            mine = lo if hh == 0 else jnp.logical_not(lo)
            den_lane = CA_DIM if hh == 0 else 0
            ones_col = (lane == den_lane).astype(BF16)
            s = scores[2 * p + hh] + bias_ref[0, 2 * p + hh]
            m = jnp.max(s, axis=-1, keepdims=True)
            pb = jnp.exp2(s - m).astype(BF16)
            o = None
            for cb in range(3):
                part = jnp.dot(pb[:, cb * CQ:(cb + 1) * CQ], jnp.where(mine, vs[cb], ones_col),
                               preferred_element_type=F32)
                o = part if o is None else o + part
            outs.append(o / o[:, den_lane:den_lane + 1])
        o_ref[0, :, sl] = jnp.where(lo, outs[0], outs[1]).astype(BF16)


def _chunk_attn(qc, kc, vc, bias):
    bsz, seq, w = qc.shape
    blk = lambda off: pl.BlockSpec((1, CQ, w), lambda b, i: (b, jnp.maximum(i + off, 0), 0))
    return pl.pallas_call(
        _chunk_kernel,
        grid=(bsz, seq // CQ),
        in_specs=[blk(0), blk(-2), blk(-1), blk(0), blk(-2), blk(-1), blk(0),
                  pl.BlockSpec((1, CA_HEADS, CQ, CBAND), lambda b, i: (jnp.minimum(i, 2), 0, 0, 0))],
        out_specs=pl.BlockSpec((1, CQ, w), lambda b, i: (b, i, 0)),
        out_shape=jax.ShapeDtypeStruct((bsz, seq, w), BF16),
        compiler_params=_cparams(("arbitrary", "arbitrary")),
        name="chunk_attn",
    )(qc, kc, kc, kc, vc, vc, vc, bias)


def _post_kernel(oa_ref, ob_ref, x_ref, gt_ref, sc_ref, sh_ref, gpost_ref, gpre_ref,
                 woa_ref, wob_ref, wr_ref, br_ref,
                 x1_out, h2_out, idx_out, gate_out, rank_out, cnt_out, carry_ref):
    t = pl.program_id(0)

    @pl.when(t == 0)
    def _():
        carry_ref[...] = jnp.zeros_like(carry_ref)

    o = jnp.dot(oa_ref[...], woa_ref[...], preferred_element_type=F32)
    o += jnp.dot(ob_ref[...], wob_ref[...], preferred_element_type=F32)
    x1 = x_ref[...] + gt_ref[0] * _rms(o, gpost_ref[...])
    x1_out[...] = x1
    h2f = _rms(x1, gpre_ref[...]) * (1.0 + sc_ref[0]) + sh_ref[0]
    h2_out[...] = _pack_rows(h2f)
    h2 = h2f.astype(BF16)

    logits = _nt_dot(wr_ref[...], h2) + br_ref[...]
    eid = lax.broadcasted_iota(jnp.int32, (N_EXPERTS, TM), 0)
    vals, idxs = [], []
    work = logits
    for _k in range(TOP_K):
        m = jnp.max(work, axis=0, keepdims=True)
        ix = jnp.min(jnp.where(work == m, eid, N_EXPERTS), axis=0, keepdims=True)
        work = jnp.where(eid == ix, -jnp.inf, work)
        vals.append(m)
        idxs.append(ix)
    es = [jnp.exp(v - vals[0]) for v in vals]
    den = es[0] + es[1] + es[2] + es[3]
    gate_out[...] = jnp.concatenate([e / den for e in es], axis=0)
    idx_out[...] = jnp.concatenate(idxs, axis=0)

    sel = (eid == idxs[0]) | (eid == idxs[1]) | (eid == idxs[2]) | (eid == idxs[3])
    self32 = sel.astype(F32)
    rr = lax.broadcasted_iota(jnp.int32, (TM, TM), 0)
    cc = lax.broadcasted_iota(jnp.int32, (TM, TM), 1)
    upper = (rr < cc).astype(BF16)
    before = jnp.dot(self32.astype(BF16), upper, preferred_element_type=F32)
    before = before + carry_ref[:, 0:1]
    ranks = [jnp.sum(jnp.where(eid == ix, before, 0.0), axis=0, keepdims=True) for ix in idxs]
    rank_out[...] = jnp.concatenate(ranks, axis=0).astype(jnp.int32)
    carry_ref[...] = carry_ref[...] + jnp.sum(self32, axis=1, keepdims=True)
    cnt_out[...] = carry_ref[...]


def _post(oa, ob, x2, gt1, sc2, sh2, g_post, g_pre, w, seq, tile_off, t):
    d = x2.shape[1]
    tpb = seq // TM
    row = lambda i: (i, 0)
    src = lambda i: (i + tile_off, 0)
    col = lambda i: (0, i)
    full = lambda i: (0, 0)
    per_b = lambda i: ((i + tile_off) // tpb, 0, 0)
    hw = oa.shape[1]
    in_specs = [
        pl.BlockSpec((TM, hw), src), pl.BlockSpec((TM, hw), src), pl.BlockSpec((TM, d), src),
        pl.BlockSpec((1, 1, d), per_b), pl.BlockSpec((1, 1, d), per_b), pl.BlockSpec((1, 1, d), per_b),
        pl.BlockSpec((1, d), full), pl.BlockSpec((1, d), full),
        pl.BlockSpec((hw, d), full), pl.BlockSpec((hw, d), full),
        pl.BlockSpec((N_EXPERTS, d), full), pl.BlockSpec((N_EXPERTS, 1), full),
    ]
    out_specs = [
        pl.BlockSpec((TM, d), row), pl.BlockSpec((TM, d // 2), row),
        pl.BlockSpec((TOP_K, TM), col), pl.BlockSpec((TOP_K, TM), col), pl.BlockSpec((TOP_K, TM), col),
        pl.BlockSpec((N_EXPERTS, LANES), full),
    ]
    out_shape = [
        jax.ShapeDtypeStruct((t, d), F32), jax.ShapeDtypeStruct((t, d // 2), U32),
        jax.ShapeDtypeStruct((TOP_K, t), jnp.int32), jax.ShapeDtypeStruct((TOP_K, t), F32),
        jax.ShapeDtypeStruct((TOP_K, t), jnp.int32),
        jax.ShapeDtypeStruct((N_EXPERTS, LANES), F32),
    ]
    return pl.pallas_call(
        _post_kernel,
        grid=(t // TM,),
        in_specs=in_specs,
        out_specs=out_specs,
        out_shape=out_shape,
        scratch_shapes=[pltpu.VMEM((N_EXPERTS, LANES), F32)],
        compiler_params=_cparams(("arbitrary",)),
        name="post",
    )(oa, ob, x2, gt1, sc2, sh2, g_post, g_pre, w["wo_a"], w["wo_b"], w["wr_t"], w["b_r"])


def _expert_kernel(be_ref, nu_ref, x_ref, wgu_ref, bgu_ref, wd_ref, bd_ref, y_ref, wgu_bf, wd_bf):
    j = pl.program_id(0)
    used = j < nu_ref[0]

    @pl.when(used & ((j == 0) | (be_ref[j] != be_ref[jnp.maximum(j - 1, 0)])))
    def _():
        wgu_bf[...] = wgu_ref[0].astype(BF16)
        wd_bf[...] = wd_ref[0].astype(BF16)

    @pl.when(used)
    def _():
        for rs in range(0, ROWS, EXPERT_SUB):
            rows = slice(rs, rs + EXPERT_SUB)
            x_lo, x_hi = _unpack_rows(x_ref[rows])
            half = x_lo.shape[1]
            gu = jnp.dot(x_lo.astype(BF16), wgu_bf[:half], preferred_element_type=F32)
            gu += jnp.dot(x_hi.astype(BF16), wgu_bf[half:], preferred_element_type=F32)
            gu += bgu_ref[0]
            gate = jnp.minimum(gu[:, :D_EXPERT], SWIGLU_LIMIT)
            up = jnp.clip(gu[:, D_EXPERT:], -SWIGLU_LIMIT, SWIGLU_LIMIT)
            glu = gate / (1.0 + jnp.exp(-SWIGLU_ALPHA * gate))
            act = ((up + 1.0) * glu).astype(BF16)
            y = jnp.dot(act, wd_bf[...], preferred_element_type=F32) + bd_ref[0]
            y_ref[rows] = _pack_rows(y)

    @pl.when(j >= nu_ref[0])
    def _():
        y_ref[...] = jnp.zeros_like(y_ref)


def _experts(xin, block_exp, n_used, wgu, bgu, wd, bd):
    p_rows, dw = xin.shape
    d = 2 * dw
    nb = p_rows // ROWS
    f2 = wgu.shape[2]
    grid_spec = pltpu.PrefetchScalarGridSpec(
        num_scalar_prefetch=2,
        grid=(nb,),
        in_specs=[
            pl.BlockSpec((ROWS, dw), lambda j, be, nu: (jnp.minimum(j, nu[0] - 1), 0)),
            pl.BlockSpec((1, d, f2), lambda j, be, nu: (be[j], 0, 0)),
            pl.BlockSpec((1, 1, f2), lambda j, be, nu: (be[j], 0, 0)),
            pl.BlockSpec((1, f2 // 2, d), lambda j, be, nu: (be[j], 0, 0)),
            pl.BlockSpec((1, 1, d), lambda j, be, nu: (be[j], 0, 0)),
        ],
        out_specs=pl.BlockSpec((ROWS, dw), lambda j, be, nu: (j, 0)),
        scratch_shapes=[pltpu.VMEM((d, f2), BF16), pltpu.VMEM((f2 // 2, d), BF16)],
    )
    return pl.pallas_call(
        _expert_kernel,
        grid_spec=grid_spec,
        out_shape=jax.ShapeDtypeStruct((p_rows, dw), U32),
        compiler_params=_cparams(("arbitrary",)),
        name="experts",
    )(block_exp, n_used, xin, wgu, bgu, wd, bd)


def _final_kernel(yg_ref, g_ref, x1_ref, gt_ref, gpost_ref, *rest):
    o_ref = rest[-1]
    g = g_ref[...]
    f_lo, f_hi = None, None
    for k in range(TOP_K):
        lo, hi = _unpack_rows(yg_ref[k])
        gk = g[:, k:k + 1]
        f_lo = lo * gk if f_lo is None else f_lo + lo * gk
        f_hi = hi * gk if f_hi is None else f_hi + hi * gk
    f = jnp.concatenate([f_lo, f_hi], axis=1)
    o_ref[...] = x1_ref[...] + gt_ref[0] * _rms(f, gpost_ref[...])


def _final(yg, gates_t, x1, gt2, g_post, seq, tile_off, t_all, prev_out):
    t, d = x1.shape
    tpb = seq // TM
    row = lambda i: (i, 0)
    in_specs = [pl.BlockSpec((TOP_K, TM, d // 2), lambda i: (0, i, 0)),
                pl.BlockSpec((TM, TOP_K), row),
                pl.BlockSpec((TM, d), row),
                pl.BlockSpec((1, 1, d), lambda i: ((i + tile_off) // tpb, 0, 0)),
                pl.BlockSpec((1, d), lambda i: (0, 0))]
    args = [yg, gates_t, x1, gt2, g_post]
    aliases = {}
    if prev_out is not None:
        in_specs.append(pl.BlockSpec(memory_space=pl.ANY))
        args.append(prev_out)
        aliases = {len(args) - 1: 0}
    return pl.pallas_call(
        _final_kernel,
        grid=(t // TM,),
        in_specs=in_specs,
        out_specs=pl.BlockSpec((TM, d), lambda i: (i + tile_off, 0)),
        out_shape=jax.ShapeDtypeStruct((t_all, d), F32),
        input_output_aliases=aliases,
        compiler_params=_cparams(("arbitrary",)),
        name="final",
    )(*args)


def _sc_mesh():
    return plsc.VectorSubcoreMesh(core_axis_name="c", subcore_axis_name="s")


def _sc_worker():
    return lax.axis_index("s") * SC_CORES + lax.axis_index("c")


def _dispatch(h2, dest, p_rows):
    t, dw = h2.shape
    per_w = t // SC_WORKERS
    n_win = per_w // SC_WIN

    @functools.partial(
        pl.kernel, mesh=_sc_mesh(),
        out_type=jax.ShapeDtypeStruct((p_rows, dw), h2.dtype),
        scratch_types=[pltpu.VMEM((TOP_K, SC_WIN), jnp.int32),
                       pltpu.VMEM((SC_WIN, dw), h2.dtype),
                       pltpu.SemaphoreType.DMA],
        name="dispatch",
    )
    def run(h_hbm, d_hbm, o_hbm, idx_v, rows_v, sem):
        wid = _sc_worker()

        @pl.loop(0, n_win)
        def _(wi):
            base = pl.multiple_of(wid * per_w + wi * SC_WIN, SC_WIN)
            for k in range(TOP_K):
                pltpu.sync_copy(d_hbm.at[pl.ds(k * t + base, SC_WIN)], idx_v.at[k])
            pltpu.sync_copy(h_hbm.at[pl.ds(base, SC_WIN)], rows_v)
            for k in range(TOP_K):
                pltpu.async_copy(rows_v, o_hbm.at[idx_v.at[k]], sem).wait()

    return run(h2, dest)


def _gather_rows(y, dest, t):
    _, dw = y.shape
    per_w = t // SC_WORKERS
    n_win = per_w // SC_WIN

    @functools.partial(
        pl.kernel, mesh=_sc_mesh(),
        out_type=jax.ShapeDtypeStruct((TOP_K, t, dw), y.dtype),
        scratch_types=[pltpu.VMEM((SC_WIN,), jnp.int32),
                       pltpu.VMEM((SC_WIN, dw), y.dtype),
                       pltpu.SemaphoreType.DMA],
        name="gather_rows",
    )
    def run(y_hbm, d_hbm, o_hbm, idx_v, rows_v, sem):
        wid = _sc_worker()

        @pl.loop(0, n_win)
        def _(wi):
            base = pl.multiple_of(wid * per_w + wi * SC_WIN, SC_WIN)
            for k in range(TOP_K):
                pltpu.sync_copy(d_hbm.at[pl.ds(k * t + base, SC_WIN)], idx_v)
                pltpu.async_copy(y_hbm.at[idx_v], rows_v, sem).wait()
                pltpu.sync_copy(rows_v, o_hbm.at[k, pl.ds(base, SC_WIN)])

    return run(y, dest)


def _prep_weights(w_in, g_q, w_qb, g_kv, w_kvb, w_o, w_router, b_router):
    d = w_in.shape[0]
    o = 0
    w_cq = w_in[:, o:o + Q_LORA]; o += Q_LORA
    w_ckv = w_in[:, o:o + KV_LORA]; o += KV_LORA
    w_kr = w_in[:, o:o + MLA_ROPE]; o += MLA_ROPE
    w_ca = w_in[:, o:]
    half = MLA_ROPE // 2
    zpad = lambda n: jnp.zeros((d, n), w_in.dtype)
    x1, x2 = w_kr[:, :half], w_kr[:, half:]
    tail = HEAD_PAD - MLA_NOPE - MLA_ROPE
    w_kr1 = jnp.concatenate([zpad(MLA_NOPE), x1, x2, zpad(tail)], axis=1)
    w_kr2 = jnp.concatenate([zpad(MLA_NOPE), -x2, x1, zpad(tail)], axis=1)
    w_in_all = jnp.concatenate([w_cq, w_ckv, w_kr1, w_kr2, w_ca], axis=1).astype(BF16)

    wq = w_qb.reshape(Q_LORA, MLA_HEADS, MLA_NOPE + MLA_ROPE)
    qn, q1, q2 = wq[..., :MLA_NOPE], wq[..., MLA_NOPE:MLA_NOPE + half], wq[..., MLA_NOPE + half:]
    zq = jnp.zeros((Q_LORA, MLA_HEADS, tail), w_qb.dtype)
    wq1 = jnp.concatenate([qn, q1, q2, zq], axis=-1).reshape(Q_LORA, -1).astype(BF16)
    wq2 = jnp.concatenate([jnp.zeros_like(qn), -q2, q1, zq], axis=-1).reshape(Q_LORA, -1).astype(BF16)

    wkv = w_kvb.reshape(KV_LORA, MLA_HEADS, MLA_NOPE + MLA_V)
    kn = wkv[..., :MLA_NOPE]
    wk = jnp.concatenate([kn, jnp.zeros((KV_LORA, MLA_HEADS, HEAD_PAD - MLA_NOPE), w_kvb.dtype)],
                         axis=-1).reshape(KV_LORA, -1).astype(BF16)
    wv = jnp.concatenate([wkv[..., MLA_NOPE:], jnp.zeros((KV_LORA, MLA_HEADS, HEAD_PAD - MLA_V), w_kvb.dtype)],
                         axis=-1).reshape(KV_LORA, -1).astype(BF16)
    v_one = jnp.tile((jnp.arange(HEAD_PAD) == MLA_V).astype(F32), MLA_HEADS).reshape(1, -1)
    mla_w = MLA_HEADS * MLA_V
    return {
        "w_in": w_in_all, "g_q": g_q.reshape(1, -1), "wq1": wq1, "wq2": wq2,
        "g_kv": g_kv.reshape(1, -1), "wk": wk, "wv": wv, "v_one": v_one,
        "wo_a": w_o[:mla_w].astype(BF16), "wo_b": w_o[mla_w:].astype(BF16),
        "wr_t": w_router.T.astype(BF16), "b_r": b_router.reshape(-1, 1),
    }


def _rope_tables(seq):
    half = MLA_ROPE // 2
    inv_freq = ROPE_THETA ** (-jnp.arange(half, dtype=F32) / half)
    ang = jnp.arange(seq, dtype=F32)[:, None] * inv_freq[None, :]
    cos, sin = jnp.cos(ang), jnp.sin(ang)
    tail = HEAD_PAD - MLA_NOPE - MLA_ROPE
    ones = jnp.ones((seq, MLA_NOPE), F32)
    zn = jnp.zeros((seq, MLA_NOPE), F32)
    zt = jnp.zeros((seq, tail), F32)
    qs = (MLA_NOPE + MLA_ROPE) ** -0.5 * LOG2E
    return {
        "cq": jnp.concatenate([ones, cos, cos, zt], axis=1) * qs,
        "sq": jnp.concatenate([zn, sin, sin, zt], axis=1) * qs,
        "ck": jnp.concatenate([zn, cos, cos, zt], axis=1),
        "sk": jnp.concatenate([zn, sin, sin, zt], axis=1),
    }


def _bias_table(rel_bias):
    n = CQ + CBAND - 1
    rel = (CBAND - 1) - jnp.arange(n)
    diag = rel_bias.astype(F32)[:, jnp.clip(rel, -(CHUNK - 1), REL_MAX) + (CHUNK - 1)] * LOG2E
    diag = jnp.concatenate([diag, jnp.zeros((diag.shape[0], 1), F32)], axis=1)
    b = jnp.tile(diag, (1, CQ))[:, :CQ * n].reshape(-1, CQ, n)[:, :, CQ - 1:]
    r = jnp.arange(CQ)[:, None]
    c = jnp.arange(CBAND)[None, :]
    dchunk = r // CHUNK - (c // CHUNK - CA_LEFT)
    visible = (dchunk >= 0) & (dchunk <= CA_LEFT)
    exists = (c // CQ)[None] >= (2 - jnp.arange(3))[:, None, None]
    return jnp.where((visible[None] & exists)[:, None], b[None], NEG)


def _layer(x, c, w_ada, b_ada, g_pre_mix, g_post_mix, g_pre_ffn, g_post_ffn, w_in, g_q, w_qb,
           g_kv, w_kvb, rel_bias, w_o, w_router, b_router, w_gu, b_gu, w_down, b_down):
    bsz, seq, d = x.shape
    t = bsz * seq
    mod = _ada(c, w_ada, b_ada).reshape(bsz, 6, 1, d)
    sh1, sc1, gt1, sh2, sc2, gt2 = [mod[:, k] for k in range(6)]
    w = _prep_weights(w_in, g_q, w_qb, g_kv, w_kvb, w_o, w_router, b_router)
    tabs = _rope_tables(seq)
    x2 = x.reshape(t, d)

    q, k, v, qc, kc, vc = _proj(x2, sc1, sh1, g_pre_mix.reshape(1, d), w, tabs, seq)
    shp = lambda a: a.reshape(bsz, seq, a.shape[-1])
    oa = _mla(shp(q), shp(k), shp(v)).reshape(t, -1)
    ob = _chunk_attn(shp(qc), shp(kc), shp(vc), _bias_table(rel_bias)).reshape(t, -1)

    tp = t // MOE_PARTS
    eids = jnp.arange(N_EXPERTS, dtype=jnp.int32)[:, None, None]
    p_rows = tp * TOP_K + N_EXPERTS * ROWS
    block_start = jnp.arange(p_rows // ROWS, dtype=jnp.int32) * ROWS
    routed = []
    for part in range(MOE_PARTS):
        x1, h2, top_idx, gates, rank, cnt = _post(oa, ob, x2, gt1, sc2, sh2, g_post_mix.reshape(1, d),
                                                  g_pre_ffn.reshape(1, d), w, seq, part * (tp // TM), tp)
        counts = cnt[:, 0].astype(jnp.int32)
        padded = ((counts + ROWS - 1) // ROWS) * ROWS
        pend = jnp.cumsum(padded)
        pstart = pend - padded
        dest = (jnp.sum(jnp.where(top_idx[None] == eids, pstart[:, None, None], 0), axis=0) + rank).reshape(-1)
        block_exp = jnp.minimum(jnp.sum(pend[None, :] <= block_start[:, None], axis=1),
                                N_EXPERTS - 1).astype(jnp.int32)
        n_used = (pend[-1:] // ROWS).astype(jnp.int32)
        routed.append((x1, gates.T, dest, block_exp, n_used, _dispatch(h2, dest, p_rows)))

    out = None
    for part, (x1, gates_t, dest, block_exp, n_used, xin) in enumerate(routed):
        y = _experts(xin, block_exp, n_used, w_gu, b_gu.reshape(N_EXPERTS, 1, -1),
                     w_down, b_down.reshape(N_EXPERTS, 1, -1))
        yg = _gather_rows(y, dest, tp)
        out = _final(yg, gates_t, x1, gt2, g_post_ffn.reshape(1, d), seq, part * (tp // TM), t, out)
    return out.reshape(bsz, seq, d)


def kernel(x, c, w_ada, b_ada, g_pre_mix, g_post_mix, g_pre_ffn, g_post_ffn, w_in, g_q, w_qb,
           g_kv, w_kvb, rel_bias, w_o, w_router, b_router, w_gu, b_gu, w_down, b_down):
    for l in range(w_ada.shape[0]):
        x = _layer(x, c, w_ada[l], b_ada[l], g_pre_mix[l], g_post_mix[l], g_pre_ffn[l], g_post_ffn[l],
                   w_in[l], g_q[l], w_qb[l], g_kv[l], w_kvb[l], rel_bias[l], w_o[l], w_router[l],
                   b_router[l], w_gu[l], b_gu[l], w_down[l], b_down[l])
    return x
```

```python
import functools
import math

import jax
import jax.numpy as jnp
from jax import lax
from jax.experimental import pallas as pl
from jax.experimental.pallas import tpu as pltpu
from jax.experimental.pallas import tpu_sc as plsc

F32 = jnp.float32
BF16 = jnp.bfloat16
U32 = jnp.uint32

D_MODEL = 1024
CHUNK = 64
EPS = 1e-6
MLA_HEADS = 8
MLA_NOPE = 64
MLA_ROPE = 32
MLA_V = 64
Q_LORA = 256
KV_LORA = 128
ROPE_THETA = 10000.0
CA_HEADS = 8
CA_DIM = 64
CA_LEFT = 8
REL_MAX = 256
N_EXPERTS = 32
TOP_K = 4
D_EXPERT = 1024
SWIGLU_LIMIT = 7.0
SWIGLU_ALPHA = 1.702

LANES = 128
HEAD_PAD = 128
LOG2E = math.log2(math.e)
NEG = -1e30
VMEM_LIMIT = 56 * 1024 * 1024

TM = 512
TQ = 512
TKW = 2 * TQ
MLA_HPS = 4
MLA_LEAD = 2
CQ = 256
CBAND = 3 * CQ
CA_LEAD = 8
ROWS = 512
EXPERT_SUB = 256
MOE_PARTS = 2
SC_CORES = 2
SC_WORKERS = SC_CORES * 16
SC_WIN = 128
CA_W = CA_HEADS * CA_DIM
W_IN_COLS = Q_LORA + KV_LORA + 2 * LANES + 3 * CA_W


def _cparams(sem, flags=None):
    return pltpu.CompilerParams(dimension_semantics=sem, vmem_limit_bytes=VMEM_LIMIT, flags=flags)


def _nt_dot(a, b):
    return lax.dot_general(a, b, (((1,), (1,)), ((), ())), preferred_element_type=F32)


def _rms(x, g):
    return x * lax.rsqrt(jnp.mean(x * x, axis=-1, keepdims=True) + EPS) * g


def _pack_rows(x):
    n = x.shape[1] // 2
    lo = lax.bitcast_convert_type(x[:, :n].astype(BF16).astype(F32), U32)
    hi = lax.bitcast_convert_type(x[:, n:].astype(BF16).astype(F32), U32)
    return (lo >> 16) | hi


def _unpack_rows(p):
    lo = lax.bitcast_convert_type(p << 16, F32)
    hi = lax.bitcast_convert_type(p & jnp.uint32(0xFFFF0000), F32)
    return lo, hi


def _ada_kernel(c_ref, w_ref, b_ref, o_ref):
    c = c_ref[...]
    a = (c / (1.0 + jnp.exp(-c))).astype(BF16)
    o_ref[...] = jnp.dot(a, w_ref[...].astype(BF16), preferred_element_type=F32) + b_ref[...]


def _ada(c, w, b):
    bsz, d = c.shape
    n = w.shape[1]
    tn = 1024
    return pl.pallas_call(
        _ada_kernel,
        grid=(n // tn,),
        in_specs=[pl.BlockSpec((bsz, d), lambda j: (0, 0)),
                  pl.BlockSpec((d, tn), lambda j: (0, j)),
                  pl.BlockSpec((1, tn), lambda j: (0, j))],
        out_specs=pl.BlockSpec((bsz, tn), lambda j: (0, j)),
        out_shape=jax.ShapeDtypeStruct((bsz, n), F32),
        compiler_params=_cparams(("arbitrary",)),
        name="ada",
    )(c, w, b.reshape(1, n))


def _proj_kernel(x_ref, sc_ref, sh_ref, g_ref, win_ref, gq_ref, wq1_ref, wq2_ref, gkv_ref,
                 wk_ref, wv_ref, vone_ref, cq_ref, sq_ref, ck_ref, sk_ref,
                 q_out, k_out, v_out, qc_out, kc_out, vc_out):
    h = _rms(x_ref[...], g_ref[...]) * (1.0 + sc_ref[0]) + sh_ref[0]
    z = jnp.dot(h.astype(BF16), win_ref[...], preferred_element_type=F32)
    o = 0
    cq = z[:, o:o + Q_LORA]; o += Q_LORA
    ckv = z[:, o:o + KV_LORA]; o += KV_LORA
    kr1 = z[:, o:o + LANES]; o += LANES
    kr2 = z[:, o:o + LANES]; o += LANES
    qc = z[:, o:o + CA_W]; o += CA_W
    kc = z[:, o:o + CA_W]; o += CA_W
    vc = z[:, o:o + CA_W]

    cqn = _rms(cq, gq_ref[...]).astype(BF16)
    q1 = jnp.dot(cqn, wq1_ref[...], preferred_element_type=F32)
    q2 = jnp.dot(cqn, wq2_ref[...], preferred_element_type=F32)
    ckvn = _rms(ckv, gkv_ref[...]).astype(BF16)
    kn = jnp.dot(ckvn, wk_ref[...], preferred_element_type=F32)
    v_out[0] = (_nt_dot(wv_ref[...], ckvn) + vone_ref[...]).astype(BF16)

    cq_t, sq_t = cq_ref[...], sq_ref[...]
    krope = kr1 * ck_ref[...] + kr2 * sk_ref[...]
    for hd in range(MLA_HEADS):
        sl = slice(hd * HEAD_PAD, (hd + 1) * HEAD_PAD)
        q_out[:, sl] = (q1[:, sl] * cq_t + q2[:, sl] * sq_t).astype(BF16)
        k_out[:, sl] = (kn[:, sl] + krope).astype(BF16)

    qc_out[...] = (qc * (CA_DIM ** -0.5 * LOG2E)).astype(BF16)
    kc_out[...] = kc.astype(BF16)
    vc_out[...] = vc.astype(BF16)


def _proj(x2, sc1, sh1, g_pre, w, tabs, seq):
    t, d = x2.shape
    tpb = seq // TM
    row = lambda i: (i, 0)
    full = lambda i: (0, 0)
    per_b = lambda i: (i // tpb, 0, 0)
    pos = lambda i: (i % tpb, 0)
    hw = MLA_HEADS * HEAD_PAD
    in_specs = [
        pl.BlockSpec((TM, d), row),
        pl.BlockSpec((1, 1, d), per_b), pl.BlockSpec((1, 1, d), per_b),
        pl.BlockSpec((1, d), full),
        pl.BlockSpec((d, W_IN_COLS), full),
        pl.BlockSpec((1, Q_LORA), full),
        pl.BlockSpec((Q_LORA, hw), full), pl.BlockSpec((Q_LORA, hw), full),
        pl.BlockSpec((1, KV_LORA), full),
        pl.BlockSpec((KV_LORA, hw), full), pl.BlockSpec((hw, KV_LORA), full), pl.BlockSpec((hw, 1), full),
        pl.BlockSpec((TM, LANES), pos), pl.BlockSpec((TM, LANES), pos),
        pl.BlockSpec((TM, LANES), pos), pl.BlockSpec((TM, LANES), pos),
    ]
    widths = (hw, hw, None, CA_W, CA_W, CA_W)
    vt_spec = pl.BlockSpec((1, hw, TM), lambda i: (i // tpb, 0, i % tpb))
    vt_shape = jax.ShapeDtypeStruct((t // seq, hw, seq), BF16)
    return pl.pallas_call(
        _proj_kernel,
        grid=(t // TM,),
        in_specs=in_specs,
        out_specs=[vt_spec if n is None else pl.BlockSpec((TM, n), row) for n in widths],
        out_shape=[vt_shape if n is None else jax.ShapeDtypeStruct((t, n), BF16) for n in widths],
        compiler_params=_cparams(("arbitrary",)),
        name="proj",
    )(x2, sc1, sh1, g_pre, w["w_in"], w["g_q"], w["wq1"], w["wq2"], w["g_kv"], w["wk"], w["wv"], w["v_one"],
      tabs["cq"], tabs["sq"], tabs["ck"], tabs["sk"])


def _mla_kernel(q_ref, k_ref, v_ref, o_ref):
    i = pl.program_id(2)
    heads = [slice(hh * HEAD_PAD, (hh + 1) * HEAD_PAD) for hh in range(MLA_HPS)]

    def step(off, width, carry, masked=False):
        def score(hs):
            s = _nt_dot(k_ref[0, pl.ds(off, width), hs], q_ref[0, :, hs])
            if masked:
                kc = lax.broadcasted_iota(jnp.int32, (width, TQ), 0) // CHUNK
                qc = lax.broadcasted_iota(jnp.int32, (width, TQ), 1) // CHUNK
                s = jnp.where(kc <= qc, s, NEG)
            return s

        scores = [score(hs) for hs in heads[:MLA_LEAD]]
        new = []
        for hh, hs in enumerate(heads):
            m, acc = carry[hh]
            m_new = jnp.maximum(m, jnp.max(scores[hh], axis=0, keepdims=True))
            p = jnp.exp2(scores[hh] - m_new).astype(BF16)
            if hh + MLA_LEAD < MLA_HPS:
                scores.append(score(heads[hh + MLA_LEAD]))
            pv = jnp.dot(v_ref[0, hs, pl.ds(off, width)], p, preferred_element_type=F32)
            new.append((m_new, jnp.exp2(m - m_new) * acc + pv))
        return tuple(new)

    init = tuple((jnp.full((1, TQ), NEG, F32), jnp.zeros((HEAD_PAD, TQ), F32)) for _ in heads)
    carry = lax.fori_loop(0, i // 2, lambda j, c: step(pl.multiple_of(j * TKW, TKW), TKW, c), init)
    carry = lax.fori_loop(0, i % 2, lambda _, c: step(pl.multiple_of((i - 1) * TQ, TQ), TQ, c), carry)
    carry = step(pl.multiple_of(i * TQ, TQ), TQ, carry, True)
    for pp in range(MLA_HPS // 2):
        pair = [acc[:MLA_V] / acc[MLA_V:MLA_V + 1] for _, acc in (carry[2 * pp], carry[2 * pp + 1])]
        o_ref[0, :, pp * LANES:(pp + 1) * LANES] = jnp.concatenate(pair, axis=0).T.astype(BF16)


def _mla(q, k, v):
    bsz, seq, _ = q.shape
    groups = MLA_HEADS // MLA_HPS
    return pl.pallas_call(
        _mla_kernel,
        grid=(bsz, groups, seq // TQ),
        in_specs=[pl.BlockSpec((1, TQ, MLA_HPS * HEAD_PAD), lambda b, p, i: (b, i, p)),
                  pl.BlockSpec((1, seq, MLA_HPS * HEAD_PAD), lambda b, p, i: (b, 0, p)),
                  pl.BlockSpec((1, MLA_HPS * HEAD_PAD, seq), lambda b, p, i: (b, p, 0))],
        out_specs=pl.BlockSpec((1, TQ, MLA_HPS * MLA_V), lambda b, p, i: (b, i, p)),
        out_shape=jax.ShapeDtypeStruct((bsz, seq, MLA_HEADS * MLA_V), BF16),
        compiler_params=_cparams(("arbitrary", "arbitrary", "arbitrary")),
        name="mla",
    )(q, k, v)


def _chunk_kernel(q_ref, k0_ref, k1_ref, k2_ref, v0_ref, v1_ref, v2_ref, bias_ref, o_ref):
    lane = lax.broadcasted_iota(jnp.int32, (CQ, LANES), 1)
    lo = lane < CA_DIM
    k_refs = (k0_ref, k1_ref, k2_ref)
    v_refs = (v0_ref, v1_ref, v2_ref)
    def score(head):
        sl = slice((head // 2) * LANES, (head // 2 + 1) * LANES)
        q = q_ref[0, :, sl]
        qm = jnp.where(lo if head % 2 == 0 else jnp.logical_not(lo), q, jnp.zeros_like(q))
        return jnp.concatenate([_nt_dot(qm, kr[0, :, sl]) for kr in k_refs], axis=1)

    scores = [score(h) for h in range(CA_LEAD)]
    for p in range(CA_HEADS // 2):
        sl = slice(p * LANES, (p + 1) * LANES)
        vs = [vr[0, :, sl] for vr in v_refs]
        outs = []
        for hh in range(2):
            mine = lo if hh == 0 else jnp.logical_not(lo)
            den_lane = CA_DIM if hh == 0 else 0
            ones_col = (lane == den_lane).astype(BF16)
            s = scores[2 * p + hh] + bias_ref[0, 2 * p + hh]
            m = jnp.max(s, axis=-1, keepdims=True)
            pb = jnp.exp2(s - m).astype(BF16)
            if 2 * p + hh + CA_LEAD < CA_HEADS:
                scores.append(score(2 * p + hh + CA_LEAD))
            o = None
            for cb in range(3):
                part = jnp.dot(pb[:, cb * CQ:(cb + 1) * CQ], jnp.where(mine, vs[cb], ones_col),
                               preferred_element_type=F32)
                o = part if o is None else o + part
            outs.append(o / o[:, den_lane:den_lane + 1])
        o_ref[0, :, sl] = jnp.where(lo, outs[0], outs[1]).astype(BF16)


def _chunk_attn(qc, kc, vc, bias):
    bsz, seq, w = qc.shape
    blk = lambda off: pl.BlockSpec((1, CQ, w), lambda b, i: (b, jnp.maximum(i + off, 0), 0))
    return pl.pallas_call(
        _chunk_kernel,
        grid=(bsz, seq // CQ),
        in_specs=[blk(0), blk(-2), blk(-1), blk(0), blk(-2), blk(-1), blk(0),
                  pl.BlockSpec((1, CA_HEADS, CQ, CBAND), lambda b, i: (jnp.minimum(i, 2), 0, 0, 0))],
        out_specs=pl.BlockSpec((1, CQ, w), lambda b, i: (b, i, 0)),
        out_shape=jax.ShapeDtypeStruct((bsz, seq, w), BF16),
        compiler_params=_cparams(("arbitrary", "arbitrary")),
        name="chunk_attn",
    )(qc, kc, kc, kc, vc, vc, vc, bias)


def _post_kernel(oa_ref, ob_ref, x_ref, gt_ref, sc_ref, sh_ref, gpost_ref, gpre_ref,
                 woa_ref, wob_ref, wr_ref, br_ref,
                 x1_out, h2_out, idx_out, gate_out, rank_out, cnt_out, carry_ref):
    t = pl.program_id(0)

    @pl.when(t == 0)
    def _():
        carry_ref[...] = jnp.zeros_like(carry_ref)

    o = jnp.dot(oa_ref[...], woa_ref[...], preferred_element_type=F32)
    o += jnp.dot(ob_ref[...], wob_ref[...], preferred_element_type=F32)
    x1 = x_ref[...] + gt_ref[0] * _rms(o, gpost_ref[...])
    x1_out[...] = x1
    h2f = _rms(x1, gpre_ref[...]) * (1.0 + sc_ref[0]) + sh_ref[0]
    h2_out[...] = _pack_rows(h2f)
    h2 = h2f.astype(BF16)

    logits = _nt_dot(wr_ref[...], h2) + br_ref[...]
    eid = lax.broadcasted_iota(jnp.int32, (N_EXPERTS, TM), 0)
    vals, idxs = [], []
    work = logits
    for _k in range(TOP_K):
        m = jnp.max(work, axis=0, keepdims=True)
        ix = jnp.min(jnp.where(work == m, eid, N_EXPERTS), axis=0, keepdims=True)
        work = jnp.where(eid == ix, -jnp.inf, work)
        vals.append(m)
        idxs.append(ix)
    es = [jnp.exp(v - vals[0]) for v in vals]
    den = es[0] + es[1] + es[2] + es[3]
    gate_out[...] = jnp.concatenate([e / den for e in es], axis=0)
    idx_out[...] = jnp.concatenate(idxs, axis=0)

    sel = (eid == idxs[0]) | (eid == idxs[1]) | (eid == idxs[2]) | (eid == idxs[3])
    self32 = sel.astype(F32)
    rr = lax.broadcasted_iota(jnp.int32, (TM, TM), 0)
    cc = lax.broadcasted_iota(jnp.int32, (TM, TM), 1)
    upper = (rr < cc).astype(BF16)
    before = jnp.dot(self32.astype(BF16), upper, preferred_element_type=F32)
    before = before + carry_ref[:, 0:1]
    ranks = [jnp.sum(jnp.where(eid == ix, before, 0.0), axis=0, keepdims=True) for ix in idxs]
    rank_out[...] = jnp.concatenate(ranks, axis=0).astype(jnp.int32)
    carry_ref[...] = carry_ref[...] + jnp.sum(self32, axis=1, keepdims=True)
    cnt_out[...] = carry_ref[...]


def _post(oa, ob, x2, gt1, sc2, sh2, g_post, g_pre, w, seq, tile_off, t):
    d = x2.shape[1]
    tpb = seq // TM
    row = lambda i: (i, 0)
    src = lambda i: (i + tile_off, 0)
    col = lambda i: (0, i)
    full = lambda i: (0, 0)
    per_b = lambda i: ((i + tile_off) // tpb, 0, 0)
    hw = oa.shape[1]
    in_specs = [
        pl.BlockSpec((TM, hw), src), pl.BlockSpec((TM, hw), src), pl.BlockSpec((TM, d), src),
        pl.BlockSpec((1, 1, d), per_b), pl.BlockSpec((1, 1, d), per_b), pl.BlockSpec((1, 1, d), per_b),
        pl.BlockSpec((1, d), full), pl.BlockSpec((1, d), full),
        pl.BlockSpec((hw, d), full), pl.BlockSpec((hw, d), full),
        pl.BlockSpec((N_EXPERTS, d), full), pl.BlockSpec((N_EXPERTS, 1), full),
    ]
    out_specs = [
        pl.BlockSpec((TM, d), row), pl.BlockSpec((TM, d // 2), row),
        pl.BlockSpec((TOP_K, TM), col), pl.BlockSpec((TOP_K, TM), col), pl.BlockSpec((TOP_K, TM), col),
        pl.BlockSpec((N_EXPERTS, LANES), full),
    ]
    out_shape = [
        jax.ShapeDtypeStruct((t, d), F32), jax.ShapeDtypeStruct((t, d // 2), U32),
        jax.ShapeDtypeStruct((TOP_K, t), jnp.int32), jax.ShapeDtypeStruct((TOP_K, t), F32),
        jax.ShapeDtypeStruct((TOP_K, t), jnp.int32),
        jax.ShapeDtypeStruct((N_EXPERTS, LANES), F32),
    ]
    return pl.pallas_call(
        _post_kernel,
        grid=(t // TM,),
        in_specs=in_specs,
        out_specs=out_specs,
        out_shape=out_shape,
        scratch_shapes=[pltpu.VMEM((N_EXPERTS, LANES), F32)],
        compiler_params=_cparams(("arbitrary",)),
        name="post",
    )(oa, ob, x2, gt1, sc2, sh2, g_post, g_pre, w["wo_a"], w["wo_b"], w["wr_t"], w["b_r"])


def _expert_kernel(be_ref, nu_ref, x_ref, wgu_ref, bgu_ref, wd_ref, bd_ref, y_ref, wgu_bf, wd_bf):
    j = pl.program_id(0)
    used = j < nu_ref[0]

    @pl.when(used & ((j == 0) | (be_ref[j] != be_ref[jnp.maximum(j - 1, 0)])))
    def _():
        wgu_bf[...] = wgu_ref[0].astype(BF16)
        wd_bf[...] = wd_ref[0].astype(BF16)

    @pl.when(used)
    def _():
        gus = []
        for rs in range(0, ROWS, EXPERT_SUB):
            x_lo, x_hi = _unpack_rows(x_ref[rs:rs + EXPERT_SUB])
            half = x_lo.shape[1]
            gu = jnp.dot(x_lo.astype(BF16), wgu_bf[:half], preferred_element_type=F32)
            gu += jnp.dot(x_hi.astype(BF16), wgu_bf[half:], preferred_element_type=F32)
            gus.append(gu + bgu_ref[0])
        for sub, gu in enumerate(gus):
            rows = slice(sub * EXPERT_SUB, (sub + 1) * EXPERT_SUB)
            gate = jnp.minimum(gu[:, :D_EXPERT], SWIGLU_LIMIT)
            up = jnp.clip(gu[:, D_EXPERT:], -SWIGLU_LIMIT, SWIGLU_LIMIT)
            glu = gate / (1.0 + jnp.exp(-SWIGLU_ALPHA * gate))
            act = ((up + 1.0) * glu).astype(BF16)
            y = jnp.dot(act, wd_bf[...], preferred_element_type=F32) + bd_ref[0]
            y_ref[rows] = _pack_rows(y)

    @pl.when(j >= nu_ref[0])
    def _():
        y_ref[...] = jnp.zeros_like(y_ref)


def _experts(xin, block_exp, n_used, wgu, bgu, wd, bd):
    p_rows, dw = xin.shape
    d = 2 * dw
    nb = p_rows // ROWS
    f2 = wgu.shape[2]
    grid_spec = pltpu.PrefetchScalarGridSpec(
        num_scalar_prefetch=2,
        grid=(nb,),
        in_specs=[
            pl.BlockSpec((ROWS, dw), lambda j, be, nu: (jnp.minimum(j, nu[0] - 1), 0)),
            pl.BlockSpec((1, d, f2), lambda j, be, nu: (be[j], 0, 0)),
            pl.BlockSpec((1, 1, f2), lambda j, be, nu: (be[j], 0, 0)),
            pl.BlockSpec((1, f2 // 2, d), lambda j, be, nu: (be[j], 0, 0)),
            pl.BlockSpec((1, 1, d), lambda j, be, nu: (be[j], 0, 0)),
        ],
        out_specs=pl.BlockSpec((ROWS, dw), lambda j, be, nu: (j, 0)),
        scratch_shapes=[pltpu.VMEM((d, f2), BF16), pltpu.VMEM((f2 // 2, d), BF16)],
    )
    return pl.pallas_call(
        _expert_kernel,
        grid_spec=grid_spec,
        out_shape=jax.ShapeDtypeStruct((p_rows, dw), U32),
        compiler_params=_cparams(("arbitrary",)),
        name="experts",
    )(block_exp, n_used, xin, wgu, bgu, wd, bd)


def _final_kernel(yg_ref, g_ref, x1_ref, gt_ref, gpost_ref, *rest):
    o_ref = rest[-1]
    g = g_ref[...]
    f_lo, f_hi = None, None
    for k in range(TOP_K):
        lo, hi = _unpack_rows(yg_ref[k])
        gk = g[:, k:k + 1]
        f_lo = lo * gk if f_lo is None else f_lo + lo * gk
        f_hi = hi * gk if f_hi is None else f_hi + hi * gk
    f = jnp.concatenate([f_lo, f_hi], axis=1)
    o_ref[...] = x1_ref[...] + gt_ref[0] * _rms(f, gpost_ref[...])


def _final(yg, gates_t, x1, gt2, g_post, seq, tile_off, t_all, prev_out):
    t, d = x1.shape
    tpb = seq // TM
    row = lambda i: (i, 0)
    in_specs = [pl.BlockSpec((TOP_K, TM, d // 2), lambda i: (0, i, 0)),
                pl.BlockSpec((TM, TOP_K), row),
                pl.BlockSpec((TM, d), row),
                pl.BlockSpec((1, 1, d), lambda i: ((i + tile_off) // tpb, 0, 0)),
                pl.BlockSpec((1, d), lambda i: (0, 0))]
    args = [yg, gates_t, x1, gt2, g_post]
    aliases = {}
    if prev_out is not None:
        in_specs.append(pl.BlockSpec(memory_space=pl.ANY))
        args.append(prev_out)
        aliases = {len(args) - 1: 0}
    return pl.pallas_call(
        _final_kernel,
        grid=(t // TM,),
        in_specs=in_specs,
        out_specs=pl.BlockSpec((TM, d), lambda i: (i + tile_off, 0)),
        out_shape=jax.ShapeDtypeStruct((t_all, d), F32),
        input_output_aliases=aliases,
        compiler_params=_cparams(("arbitrary",)),
        name="final",
    )(*args)


def _sc_mesh():
    return plsc.VectorSubcoreMesh(core_axis_name="c", subcore_axis_name="s")


def _sc_worker():
    return lax.axis_index("s") * SC_CORES + lax.axis_index("c")


def _dispatch(h2, dest, p_rows):
    t, dw = h2.shape
    per_w = t // SC_WORKERS
    n_win = per_w // SC_WIN

    @functools.partial(
        pl.kernel, mesh=_sc_mesh(),
        out_type=jax.ShapeDtypeStruct((p_rows, dw), h2.dtype),
        scratch_types=[pltpu.VMEM((TOP_K, SC_WIN), jnp.int32),
                       pltpu.VMEM((SC_WIN, dw), h2.dtype),
                       pltpu.SemaphoreType.DMA],
        name="dispatch",
    )
    def run(h_hbm, d_hbm, o_hbm, idx_v, rows_v, sem):
        wid = _sc_worker()

        @pl.loop(0, n_win)
        def _(wi):
            base = pl.multiple_of(wid * per_w + wi * SC_WIN, SC_WIN)
            for k in range(TOP_K):
                pltpu.sync_copy(d_hbm.at[pl.ds(k * t + base, SC_WIN)], idx_v.at[k])
            pltpu.sync_copy(h_hbm.at[pl.ds(base, SC_WIN)], rows_v)
            for k in range(TOP_K):
                pltpu.async_copy(rows_v, o_hbm.at[idx_v.at[k]], sem).wait()

    return run(h2, dest)


def _gather_rows(y, dest, t):
    _, dw = y.shape
    per_w = t // SC_WORKERS
    n_win = per_w // SC_WIN

    @functools.partial(
        pl.kernel, mesh=_sc_mesh(),
        out_type=jax.ShapeDtypeStruct((TOP_K, t, dw), y.dtype),
        scratch_types=[pltpu.VMEM((SC_WIN,), jnp.int32),
                       pltpu.VMEM((SC_WIN, dw), y.dtype),
                       pltpu.SemaphoreType.DMA],
        name="gather_rows",
    )
    def run(y_hbm, d_hbm, o_hbm, idx_v, rows_v, sem):
        wid = _sc_worker()

        @pl.loop(0, n_win)
        def _(wi):
            base = pl.multiple_of(wid * per_w + wi * SC_WIN, SC_WIN)
            for k in range(TOP_K):
                pltpu.sync_copy(d_hbm.at[pl.ds(k * t + base, SC_WIN)], idx_v)
                pltpu.async_copy(y_hbm.at[idx_v], rows_v, sem).wait()
                pltpu.sync_copy(rows_v, o_hbm.at[k, pl.ds(base, SC_WIN)])

    return run(y, dest)


def _prep_weights(w_in, g_q, w_qb, g_kv, w_kvb, w_o, w_router, b_router):
    d = w_in.shape[0]
    o = 0
    w_cq = w_in[:, o:o + Q_LORA]; o += Q_LORA
    w_ckv = w_in[:, o:o + KV_LORA]; o += KV_LORA
    w_kr = w_in[:, o:o + MLA_ROPE]; o += MLA_ROPE
    w_ca = w_in[:, o:]
    half = MLA_ROPE // 2
    zpad = lambda n: jnp.zeros((d, n), w_in.dtype)
    x1, x2 = w_kr[:, :half], w_kr[:, half:]
    tail = HEAD_PAD - MLA_NOPE - MLA_ROPE
    w_kr1 = jnp.concatenate([zpad(MLA_NOPE), x1, x2, zpad(tail)], axis=1)
    w_kr2 = jnp.concatenate([zpad(MLA_NOPE), -x2, x1, zpad(tail)], axis=1)
    w_in_all = jnp.concatenate([w_cq, w_ckv, w_kr1, w_kr2, w_ca], axis=1).astype(BF16)

    wq = w_qb.reshape(Q_LORA, MLA_HEADS, MLA_NOPE + MLA_ROPE)
    qn, q1, q2 = wq[..., :MLA_NOPE], wq[..., MLA_NOPE:MLA_NOPE + half], wq[..., MLA_NOPE + half:]
    zq = jnp.zeros((Q_LORA, MLA_HEADS, tail), w_qb.dtype)
    wq1 = jnp.concatenate([qn, q1, q2, zq], axis=-1).reshape(Q_LORA, -1).astype(BF16)
    wq2 = jnp.concatenate([jnp.zeros_like(qn), -q2, q1, zq], axis=-1).reshape(Q_LORA, -1).astype(BF16)

    wkv = w_kvb.reshape(KV_LORA, MLA_HEADS, MLA_NOPE + MLA_V)
    kn = wkv[..., :MLA_NOPE]
    wk = jnp.concatenate([kn, jnp.zeros((KV_LORA, MLA_HEADS, HEAD_PAD - MLA_NOPE), w_kvb.dtype)],
                         axis=-1).reshape(KV_LORA, -1).astype(BF16)
    wv = jnp.concatenate([wkv[..., MLA_NOPE:], jnp.zeros((KV_LORA, MLA_HEADS, HEAD_PAD - MLA_V), w_kvb.dtype)],
                         axis=-1).reshape(KV_LORA, -1).T.astype(BF16)
    v_one = jnp.tile((jnp.arange(HEAD_PAD) == MLA_V).astype(F32), MLA_HEADS).reshape(-1, 1)
    mla_w = MLA_HEADS * MLA_V
    return {
        "w_in": w_in_all, "g_q": g_q.reshape(1, -1), "wq1": wq1, "wq2": wq2,
        "g_kv": g_kv.reshape(1, -1), "wk": wk, "wv": wv, "v_one": v_one,
        "wo_a": w_o[:mla_w].astype(BF16), "wo_b": w_o[mla_w:].astype(BF16),
        "wr_t": w_router.T.astype(BF16), "b_r": b_router.reshape(-1, 1),
    }


def _rope_tables(seq):
    half = MLA_ROPE // 2
    inv_freq = ROPE_THETA ** (-jnp.arange(half, dtype=F32) / half)
    ang = jnp.arange(seq, dtype=F32)[:, None] * inv_freq[None, :]
    cos, sin = jnp.cos(ang), jnp.sin(ang)
    tail = HEAD_PAD - MLA_NOPE - MLA_ROPE
    ones = jnp.ones((seq, MLA_NOPE), F32)
    zn = jnp.zeros((seq, MLA_NOPE), F32)
    zt = jnp.zeros((seq, tail), F32)
    qs = (MLA_NOPE + MLA_ROPE) ** -0.5 * LOG2E
    return {
        "cq": jnp.concatenate([ones, cos, cos, zt], axis=1) * qs,
        "sq": jnp.concatenate([zn, sin, sin, zt], axis=1) * qs,
        "ck": jnp.concatenate([zn, cos, cos, zt], axis=1),
        "sk": jnp.concatenate([zn, sin, sin, zt], axis=1),
    }


def _bias_table(rel_bias):
    n = CQ + CBAND - 1
    rel = (CBAND - 1) - jnp.arange(n)
    diag = rel_bias.astype(F32)[:, jnp.clip(rel, -(CHUNK - 1), REL_MAX) + (CHUNK - 1)] * LOG2E
    diag = jnp.concatenate([diag, jnp.zeros((diag.shape[0], 1), F32)], axis=1)
    b = jnp.tile(diag, (1, CQ))[:, :CQ * n].reshape(-1, CQ, n)[:, :, CQ - 1:]
    r = jnp.arange(CQ)[:, None]
    c = jnp.arange(CBAND)[None, :]
    dchunk = r // CHUNK - (c // CHUNK - CA_LEFT)
    visible = (dchunk >= 0) & (dchunk <= CA_LEFT)
    exists = (c // CQ)[None] >= (2 - jnp.arange(3))[:, None, None]
    return jnp.where((visible[None] & exists)[:, None], b[None], NEG)


def _layer(x, c, w_ada, b_ada, g_pre_mix, g_post_mix, g_pre_ffn, g_post_ffn, w_in, g_q, w_qb,
           g_kv, w_kvb, rel_bias, w_o, w_router, b_router, w_gu, b_gu, w_down, b_down):
    bsz, seq, d = x.shape
    t = bsz * seq
    mod = _ada(c, w_ada, b_ada).reshape(bsz, 6, 1, d)
    sh1, sc1, gt1, sh2, sc2, gt2 = [mod[:, k] for k in range(6)]
    w = _prep_weights(w_in, g_q, w_qb, g_kv, w_kvb, w_o, w_router, b_router)
    tabs = _rope_tables(seq)
    x2 = x.reshape(t, d)

    q, k, v, qc, kc, vc = _proj(x2, sc1, sh1, g_pre_mix.reshape(1, d), w, tabs, seq)
    shp = lambda a: a.reshape(bsz, seq, a.shape[-1])
    oa = _mla(shp(q), shp(k), v).reshape(t, -1)
    ob = _chunk_attn(shp(qc), shp(kc), shp(vc), _bias_table(rel_bias)).reshape(t, -1)

    tp = t // MOE_PARTS
    eids = jnp.arange(N_EXPERTS, dtype=jnp.int32)[:, None, None]
    p_rows = tp * TOP_K + N_EXPERTS * ROWS
    block_start = jnp.arange(p_rows // ROWS, dtype=jnp.int32) * ROWS
    routed = []
    for part in range(MOE_PARTS):
        x1, h2, top_idx, gates, rank, cnt = _post(oa, ob, x2, gt1, sc2, sh2, g_post_mix.reshape(1, d),
                                                  g_pre_ffn.reshape(1, d), w, seq, part * (tp // TM), tp)
        counts = cnt[:, 0].astype(jnp.int32)
        padded = ((counts + ROWS - 1) // ROWS) * ROWS
        pend = jnp.cumsum(padded)
        pstart = pend - padded
        dest = (jnp.sum(jnp.where(top_idx[None] == eids, pstart[:, None, None], 0), axis=0) + rank).reshape(-1)
        block_exp = jnp.minimum(jnp.sum(pend[None, :] <= block_start[:, None], axis=1),
                                N_EXPERTS - 1).astype(jnp.int32)
        n_used = (pend[-1:] // ROWS).astype(jnp.int32)
        routed.append((x1, gates.T, dest, block_exp, n_used, _dispatch(h2, dest, p_rows)))

    out = None
    for part, (x1, gates_t, dest, block_exp, n_used, xin) in enumerate(routed):
        y = _experts(xin, block_exp, n_used, w_gu, b_gu.reshape(N_EXPERTS, 1, -1),
                     w_down, b_down.reshape(N_EXPERTS, 1, -1))
        yg = _gather_rows(y, dest, tp)
        out = _final(yg, gates_t, x1, gt2, g_post_ffn.reshape(1, d), seq, part * (tp // TM), t, out)
    return out.reshape(bsz, seq, d)


def kernel(x, c, w_ada, b_ada, g_pre_mix, g_post_mix, g_pre_ffn, g_post_ffn, w_in, g_q, w_qb,
           g_kv, w_kvb, rel_bias, w_o, w_router, b_router, w_gu, b_gu, w_down, b_down):
    for l in range(w_ada.shape[0]):
        x = _layer(x, c, w_ada[l], b_ada[l], g_pre_mix[l], g_post_mix[l], g_pre_ffn[l], g_post_ffn[l],
                   w_in[l], g_q[l], w_qb[l], g_kv[l], w_kvb[l], rel_bias[l], w_o[l], w_router[l],
                   b_router[l], w_gu[l], b_gu[l], w_down[l], b_down[l])
    return x
```

```python
import functools
import math

import jax
import jax.numpy as jnp
from jax import lax
from jax.experimental import pallas as pl
from jax.experimental.pallas import tpu as pltpu
from jax.experimental.pallas import tpu_sc as plsc

F32 = jnp.float32
BF16 = jnp.bfloat16
U32 = jnp.uint32

D_MODEL = 1024
CHUNK = 64
EPS = 1e-6
MLA_HEADS = 8
MLA_NOPE = 64
MLA_ROPE = 32
MLA_V = 64
Q_LORA = 256
KV_LORA = 128
ROPE_THETA = 10000.0
CA_HEADS = 8
CA_DIM = 64
CA_LEFT = 8
REL_MAX = 256
N_EXPERTS = 32
TOP_K = 4
D_EXPERT = 1024
SWIGLU_LIMIT = 7.0
SWIGLU_ALPHA = 1.702

LANES = 128
HEAD_PAD = 128
LOG2E = math.log2(math.e)
NEG = -1e30
VMEM_LIMIT = 56 * 1024 * 1024

TM = 512
POST_SUB = 128
TQ = 512
TKW = 2 * TQ
MLA_HPS = 4
MLA_LEAD = 2
CQ = 256
CBAND = 3 * CQ
CA_LEAD = 8
ROWS = 512
EXPERT_SUB = 256
MOE_PARTS = 2
SC_CORES = 2
SC_WORKERS = SC_CORES * 16
SC_WIN = 128
CA_W = CA_HEADS * CA_DIM
W_IN_COLS = Q_LORA + KV_LORA + 2 * LANES + 3 * CA_W


def _cparams(sem, flags=None):
    return pltpu.CompilerParams(dimension_semantics=sem, vmem_limit_bytes=VMEM_LIMIT, flags=flags)


def _nt_dot(a, b):
    return lax.dot_general(a, b, (((1,), (1,)), ((), ())), preferred_element_type=F32)


def _rms(x, g):
    return x * lax.rsqrt(jnp.mean(x * x, axis=-1, keepdims=True) + EPS) * g


def _pack_rows(x):
    n = x.shape[1] // 2
    lo = lax.bitcast_convert_type(x[:, :n].astype(BF16).astype(F32), U32)
    hi = lax.bitcast_convert_type(x[:, n:].astype(BF16).astype(F32), U32)
    return (lo >> 16) | hi


def _unpack_rows(p):
    lo = lax.bitcast_convert_type(p << 16, F32)
    hi = lax.bitcast_convert_type(p & jnp.uint32(0xFFFF0000), F32)
    return lo, hi


def _ada_kernel(c_ref, w_ref, b_ref, o_ref):
    c = c_ref[...]
    a = (c / (1.0 + jnp.exp(-c))).astype(BF16)
    o_ref[...] = jnp.dot(a, w_ref[...].astype(BF16), preferred_element_type=F32) + b_ref[...]


def _ada(c, w, b):
    bsz, d = c.shape
    n = w.shape[1]
    tn = 1024
    return pl.pallas_call(
        _ada_kernel,
        grid=(n // tn,),
        in_specs=[pl.BlockSpec((bsz, d), lambda j: (0, 0)),
                  pl.BlockSpec((d, tn), lambda j: (0, j)),
                  pl.BlockSpec((1, tn), lambda j: (0, j))],
        out_specs=pl.BlockSpec((bsz, tn), lambda j: (0, j)),
        out_shape=jax.ShapeDtypeStruct((bsz, n), F32),
        compiler_params=_cparams(("arbitrary",)),
        name="ada",
    )(c, w, b.reshape(1, n))


def _proj_kernel(x_ref, sc_ref, sh_ref, g_ref, win_ref, gq_ref, wq1_ref, wq2_ref, gkv_ref,
                 wk_ref, wv_ref, vone_ref, cq_ref, sq_ref, ck_ref, sk_ref,
                 q_out, k_out, v_out, qc_out, kc_out, vc_out):
    h = _rms(x_ref[...], g_ref[...]) * (1.0 + sc_ref[0]) + sh_ref[0]
    z = jnp.dot(h.astype(BF16), win_ref[...], preferred_element_type=F32)
    o = 0
    cq = z[:, o:o + Q_LORA]; o += Q_LORA
    ckv = z[:, o:o + KV_LORA]; o += KV_LORA
    kr1 = z[:, o:o + LANES]; o += LANES
    kr2 = z[:, o:o + LANES]; o += LANES
    qc = z[:, o:o + CA_W]; o += CA_W
    kc = z[:, o:o + CA_W]; o += CA_W
    vc = z[:, o:o + CA_W]

    cqn = _rms(cq, gq_ref[...]).astype(BF16)
    q1 = jnp.dot(cqn, wq1_ref[...], preferred_element_type=F32)
    q2 = jnp.dot(cqn, wq2_ref[...], preferred_element_type=F32)
    ckvn = _rms(ckv, gkv_ref[...]).astype(BF16)
    kn = jnp.dot(ckvn, wk_ref[...], preferred_element_type=F32)
    v_out[0] = (_nt_dot(wv_ref[...], ckvn) + vone_ref[...]).astype(BF16)

    cq_t, sq_t = cq_ref[...], sq_ref[...]
    krope = kr1 * ck_ref[...] + kr2 * sk_ref[...]
    for hd in range(MLA_HEADS):
        sl = slice(hd * HEAD_PAD, (hd + 1) * HEAD_PAD)
        q_out[:, sl] = (q1[:, sl] * cq_t + q2[:, sl] * sq_t).astype(BF16)
        k_out[:, sl] = (kn[:, sl] + krope).astype(BF16)

    qc_out[...] = (qc * (CA_DIM ** -0.5 * LOG2E)).astype(BF16)
    kc_out[...] = kc.astype(BF16)
    vc_out[0] = vc.T.astype(BF16)


def _proj(x2, sc1, sh1, g_pre, w, tabs, seq):
    t, d = x2.shape
    tpb = seq // TM
    row = lambda i: (i, 0)
    full = lambda i: (0, 0)
    per_b = lambda i: (i // tpb, 0, 0)
    pos = lambda i: (i % tpb, 0)
    hw = MLA_HEADS * HEAD_PAD
    in_specs = [
        pl.BlockSpec((TM, d), row),
        pl.BlockSpec((1, 1, d), per_b), pl.BlockSpec((1, 1, d), per_b),
        pl.BlockSpec((1, d), full),
        pl.BlockSpec((d, W_IN_COLS), full),
        pl.BlockSpec((1, Q_LORA), full),
        pl.BlockSpec((Q_LORA, hw), full), pl.BlockSpec((Q_LORA, hw), full),
        pl.BlockSpec((1, KV_LORA), full),
        pl.BlockSpec((KV_LORA, hw), full), pl.BlockSpec((hw, KV_LORA), full), pl.BlockSpec((hw, 1), full),
        pl.BlockSpec((TM, LANES), pos), pl.BlockSpec((TM, LANES), pos),
        pl.BlockSpec((TM, LANES), pos), pl.BlockSpec((TM, LANES), pos),
    ]
    outs = ((hw, False), (hw, False), (hw, True), (CA_W, False), (CA_W, False), (CA_W, True))
    t_spec = lambda n: pl.BlockSpec((1, n, TM), lambda i: (i // tpb, 0, i % tpb))
    return pl.pallas_call(
        _proj_kernel,
        grid=(t // TM,),
        in_specs=in_specs,
        out_specs=[t_spec(n) if tr else pl.BlockSpec((TM, n), row) for n, tr in outs],
        out_shape=[jax.ShapeDtypeStruct((t // seq, n, seq) if tr else (t, n), BF16) for n, tr in outs],
        compiler_params=_cparams(("arbitrary",)),
        name="proj",
    )(x2, sc1, sh1, g_pre, w["w_in"], w["g_q"], w["wq1"], w["wq2"], w["g_kv"], w["wk"], w["wv"], w["v_one"],
      tabs["cq"], tabs["sq"], tabs["ck"], tabs["sk"])


def _mla_kernel(q_ref, k_ref, v_ref, o_ref):
    i = pl.program_id(2)
    heads = [slice(hh * HEAD_PAD, (hh + 1) * HEAD_PAD) for hh in range(MLA_HPS)]

    def step(off, width, carry, masked=False):
        def score(hs):
            s = _nt_dot(k_ref[0, pl.ds(off, width), hs], q_ref[0, :, hs])
            if masked:
                kc = lax.broadcasted_iota(jnp.int32, (width, TQ), 0) // CHUNK
                qc = lax.broadcasted_iota(jnp.int32, (width, TQ), 1) // CHUNK
                s = jnp.where(kc <= qc, s, NEG)
            return s

        scores = [score(hs) for hs in heads[:MLA_LEAD]]
        new = []
        for hh, hs in enumerate(heads):
            m, acc = carry[hh]
            m_new = jnp.maximum(m, jnp.max(scores[hh], axis=0, keepdims=True))
            p = jnp.exp2(scores[hh] - m_new).astype(BF16)
            if hh + MLA_LEAD < MLA_HPS:
                scores.append(score(heads[hh + MLA_LEAD]))
            pv = jnp.dot(v_ref[0, hs, pl.ds(off, width)], p, preferred_element_type=F32)
            new.append((m_new, jnp.exp2(m - m_new) * acc + pv))
        return tuple(new)

    init = tuple((jnp.full((1, TQ), NEG, F32), jnp.zeros((HEAD_PAD, TQ), F32)) for _ in heads)
    carry = lax.fori_loop(0, i // 2, lambda j, c: step(pl.multiple_of(j * TKW, TKW), TKW, c), init)
    carry = lax.fori_loop(0, i % 2, lambda _, c: step(pl.multiple_of((i - 1) * TQ, TQ), TQ, c), carry)
    carry = step(pl.multiple_of(i * TQ, TQ), TQ, carry, True)
    for pp in range(MLA_HPS // 2):
        pair = [acc[:MLA_V] / acc[MLA_V:MLA_V + 1] for _, acc in (carry[2 * pp], carry[2 * pp + 1])]
        o_ref[0, :, pp * LANES:(pp + 1) * LANES] = jnp.concatenate(pair, axis=0).T.astype(BF16)


def _mla(q, k, v):
    bsz, seq, _ = q.shape
    groups = MLA_HEADS // MLA_HPS
    return pl.pallas_call(
        _mla_kernel,
        grid=(bsz, groups, seq // TQ),
        in_specs=[pl.BlockSpec((1, TQ, MLA_HPS * HEAD_PAD), lambda b, p, i: (b, i, p)),
                  pl.BlockSpec((1, seq, MLA_HPS * HEAD_PAD), lambda b, p, i: (b, 0, p)),
                  pl.BlockSpec((1, MLA_HPS * HEAD_PAD, seq), lambda b, p, i: (b, p, 0))],
        out_specs=pl.BlockSpec((1, TQ, MLA_HPS * MLA_V), lambda b, p, i: (b, i, p)),
        out_shape=jax.ShapeDtypeStruct((bsz, seq, MLA_HEADS * MLA_V), BF16),
        compiler_params=_cparams(("arbitrary", "arbitrary", "arbitrary")),
        name="mla",
    )(q, k, v)


def _chunk_kernel(q_ref, k0_ref, k1_ref, k2_ref, v0_ref, v1_ref, v2_ref, bias_ref, o_ref):
    lane = lax.broadcasted_iota(jnp.int32, (CQ, LANES), 1)
    lo = lane < CA_DIM
    row = lax.broadcasted_iota(jnp.int32, (LANES, CQ), 0)
    top = row < CA_DIM
    k_refs = (k0_ref, k1_ref, k2_ref)
    v_refs = (v0_ref, v1_ref, v2_ref)

    def score(head):
        sl = slice((head // 2) * LANES, (head // 2 + 1) * LANES)
        q = q_ref[0, :, sl]
        qm = jnp.where(lo if head % 2 == 0 else jnp.logical_not(lo), q, jnp.zeros_like(q))
        return jnp.concatenate([_nt_dot(kr[0, :, sl], qm) for kr in k_refs], axis=0)

    scores = [score(h) for h in range(CA_LEAD)]
    for p in range(CA_HEADS // 2):
        sl = slice(p * LANES, (p + 1) * LANES)
        vts = [vr[0, sl, :] for vr in v_refs]
        outs = []
        for hh in range(2):
            mine = top if hh == 0 else jnp.logical_not(top)
            den_row = CA_DIM if hh == 0 else 0
            ones_row = (row == den_row).astype(BF16)
            s = scores[2 * p + hh] + bias_ref[0, 2 * p + hh]
            m = jnp.max(s, axis=0, keepdims=True)
            pb = jnp.exp2(s - m).astype(BF16)
            if 2 * p + hh + CA_LEAD < CA_HEADS:
                scores.append(score(2 * p + hh + CA_LEAD))
            o = None
            for cb in range(3):
                part = jnp.dot(jnp.where(mine, vts[cb], ones_row), pb[cb * CQ:(cb + 1) * CQ],
                               preferred_element_type=F32)
                o = part if o is None else o + part
            o = o / o[den_row:den_row + 1]
            outs.append(o[:CA_DIM] if hh == 0 else o[CA_DIM:])
        o_ref[0, :, sl] = jnp.concatenate(outs, axis=0).T.astype(BF16)


def _chunk_attn(qc, kc, vct, bias):
    bsz, seq, w = qc.shape
    blk = lambda off: pl.BlockSpec((1, CQ, w), lambda b, i: (b, jnp.maximum(i + off, 0), 0))
    blk_t = lambda off: pl.BlockSpec((1, w, CQ), lambda b, i: (b, 0, jnp.maximum(i + off, 0)))
    return pl.pallas_call(
        _chunk_kernel,
        grid=(bsz, seq // CQ),
        in_specs=[blk(0), blk(-2), blk(-1), blk(0), blk_t(-2), blk_t(-1), blk_t(0),
                  pl.BlockSpec((1, CA_HEADS, CBAND, CQ), lambda b, i: (jnp.minimum(i, 2), 0, 0, 0))],
        out_specs=pl.BlockSpec((1, CQ, w), lambda b, i: (b, i, 0)),
        out_shape=jax.ShapeDtypeStruct((bsz, seq, w), BF16),
        compiler_params=_cparams(("arbitrary", "arbitrary")),
        name="chunk_attn",
    )(qc, kc, kc, kc, vct, vct, vct, bias)


def _post_kernel(oa_ref, ob_ref, x_ref, gt_ref, sc_ref, sh_ref, gpost_ref, gpre_ref,
                 woa_ref, wob_ref, wr_ref, br_ref,
                 x1_out, h2_out, idx_out, gate_out, rank_out, cnt_out, carry_ref):
    t = pl.program_id(0)

    @pl.when(t == 0)
    def _():
        carry_ref[...] = jnp.zeros_like(carry_ref)

    os = []
    for rs in range(0, TM, POST_SUB):
        rows = slice(rs, rs + POST_SUB)
        o = jnp.dot(oa_ref[rows], woa_ref[...], preferred_element_type=F32)
        os.append(o + jnp.dot(ob_ref[rows], wob_ref[...], preferred_element_type=F32))
    h2s = []
    gain1 = gt_ref[0] * gpost_ref[...]
    gain2 = gpre_ref[...] * (1.0 + sc_ref[0])
    for sub, o in enumerate(os):
        rows = slice(sub * POST_SUB, (sub + 1) * POST_SUB)
        x1 = x_ref[rows] + _rms(o, gain1)
        x1_out[rows] = x1
        h2f = _rms(x1, gain2) + sh_ref[0]
        h2_out[rows] = _pack_rows(h2f)
        h2s.append(h2f.astype(BF16))
    h2 = jnp.concatenate(h2s, axis=0)

    logits = _nt_dot(wr_ref[...], h2) + br_ref[...]
    eid = lax.broadcasted_iota(jnp.int32, (N_EXPERTS, TM), 0)
    vals, idxs = [], []
    work = logits
    for _k in range(TOP_K):
        m = jnp.max(work, axis=0, keepdims=True)
        ix = jnp.min(jnp.where(work == m, eid, N_EXPERTS), axis=0, keepdims=True)
        work = jnp.where(eid == ix, -jnp.inf, work)
        vals.append(m)
        idxs.append(ix)
    es = [jnp.exp(v - vals[0]) for v in vals]
    den = es[0] + es[1] + es[2] + es[3]
    gate_out[...] = jnp.concatenate([e / den for e in es], axis=0)
    idx_out[...] = jnp.concatenate(idxs, axis=0)

    sel = (eid == idxs[0]) | (eid == idxs[1]) | (eid == idxs[2]) | (eid == idxs[3])
    self32 = sel.astype(F32)
    rr = lax.broadcasted_iota(jnp.int32, (TM, TM), 0)
    cc = lax.broadcasted_iota(jnp.int32, (TM, TM), 1)
    upper = (rr < cc).astype(BF16)
    before = jnp.dot(self32.astype(BF16), upper, preferred_element_type=F32)
    before = before + carry_ref[:, 0:1]
    ranks = [jnp.sum(jnp.where(eid == ix, before, 0.0), axis=0, keepdims=True) for ix in idxs]
    rank_out[...] = jnp.concatenate(ranks, axis=0).astype(jnp.int32)
    carry_ref[...] = carry_ref[...] + jnp.sum(self32, axis=1, keepdims=True)
    cnt_out[...] = carry_ref[...]


def _post(oa, ob, x2, gt1, sc2, sh2, g_post, g_pre, w, seq, tile_off, t):
    d = x2.shape[1]
    tpb = seq // TM
    row = lambda i: (i, 0)
    src = lambda i: (i + tile_off, 0)
    col = lambda i: (0, i)
    full = lambda i: (0, 0)
    per_b = lambda i: ((i + tile_off) // tpb, 0, 0)
    hw = oa.shape[1]
    in_specs = [
        pl.BlockSpec((TM, hw), src), pl.BlockSpec((TM, hw), src), pl.BlockSpec((TM, d), src),
        pl.BlockSpec((1, 1, d), per_b), pl.BlockSpec((1, 1, d), per_b), pl.BlockSpec((1, 1, d), per_b),
        pl.BlockSpec((1, d), full), pl.BlockSpec((1, d), full),
        pl.BlockSpec((hw, d), full), pl.BlockSpec((hw, d), full),
        pl.BlockSpec((N_EXPERTS, d), full), pl.BlockSpec((N_EXPERTS, 1), full),
    ]
    out_specs = [
        pl.BlockSpec((TM, d), row), pl.BlockSpec((TM, d // 2), row),
        pl.BlockSpec((TOP_K, TM), col), pl.BlockSpec((TOP_K, TM), col), pl.BlockSpec((TOP_K, TM), col),
        pl.BlockSpec((N_EXPERTS, LANES), full),
    ]
    out_shape = [
        jax.ShapeDtypeStruct((t, d), F32), jax.ShapeDtypeStruct((t, d // 2), U32),
        jax.ShapeDtypeStruct((TOP_K, t), jnp.int32), jax.ShapeDtypeStruct((TOP_K, t), F32),
        jax.ShapeDtypeStruct((TOP_K, t), jnp.int32),
        jax.ShapeDtypeStruct((N_EXPERTS, LANES), F32),
    ]
    return pl.pallas_call(
        _post_kernel,
        grid=(t // TM,),
        in_specs=in_specs,
        out_specs=out_specs,
        out_shape=out_shape,
        scratch_shapes=[pltpu.VMEM((N_EXPERTS, LANES), F32)],
        compiler_params=_cparams(("arbitrary",)),
        name="post",
    )(oa, ob, x2, gt1, sc2, sh2, g_post, g_pre, w["wo_a"], w["wo_b"], w["wr_t"], w["b_r"])


def _expert_kernel(be_ref, nu_ref, x_ref, wgu_ref, bgu_ref, wd_ref, bd_ref, y_ref, wgu_bf, wd_bf):
    j = pl.program_id(0)
    used = j < nu_ref[0]

    @pl.when(used & ((j == 0) | (be_ref[j] != be_ref[jnp.maximum(j - 1, 0)])))
    def _():
        wgu_bf[...] = wgu_ref[0].astype(BF16)
        wd_bf[...] = wd_ref[0].astype(BF16)

    @pl.when(used)
    def _():
        gus = []
        for rs in range(0, ROWS, EXPERT_SUB):
            x_lo, x_hi = _unpack_rows(x_ref[rs:rs + EXPERT_SUB])
            half = x_lo.shape[1]
            gu = jnp.dot(x_lo.astype(BF16), wgu_bf[:half], preferred_element_type=F32)
            gu += jnp.dot(x_hi.astype(BF16), wgu_bf[half:], preferred_element_type=F32)
            gus.append(gu + bgu_ref[0])
        for sub, gu in enumerate(gus):
            rows = slice(sub * EXPERT_SUB, (sub + 1) * EXPERT_SUB)
            gate = jnp.minimum(gu[:, :D_EXPERT], SWIGLU_LIMIT)
            up = jnp.clip(gu[:, D_EXPERT:], -SWIGLU_LIMIT, SWIGLU_LIMIT)
            glu = gate / (1.0 + jnp.exp(-SWIGLU_ALPHA * gate))
            act = ((up + 1.0) * glu).astype(BF16)
            y = jnp.dot(act, wd_bf[...], preferred_element_type=F32) + bd_ref[0]
            y_ref[rows] = _pack_rows(y)

    @pl.when(j >= nu_ref[0])
    def _():
        y_ref[...] = jnp.zeros_like(y_ref)


def _experts(xin, block_exp, n_used, wgu, bgu, wd, bd):
    p_rows, dw = xin.shape
    d = 2 * dw
    nb = p_rows // ROWS
    f2 = wgu.shape[2]
    grid_spec = pltpu.PrefetchScalarGridSpec(
        num_scalar_prefetch=2,
        grid=(nb,),
        in_specs=[
            pl.BlockSpec((ROWS, dw), lambda j, be, nu: (jnp.minimum(j, nu[0] - 1), 0)),
            pl.BlockSpec((1, d, f2), lambda j, be, nu: (be[j], 0, 0)),
            pl.BlockSpec((1, 1, f2), lambda j, be, nu: (be[j], 0, 0)),
            pl.BlockSpec((1, f2 // 2, d), lambda j, be, nu: (be[j], 0, 0)),
            pl.BlockSpec((1, 1, d), lambda j, be, nu: (be[j], 0, 0)),
        ],
        out_specs=pl.BlockSpec((ROWS, dw), lambda j, be, nu: (j, 0)),
        scratch_shapes=[pltpu.VMEM((d, f2), BF16), pltpu.VMEM((f2 // 2, d), BF16)],
    )
    return pl.pallas_call(
        _expert_kernel,
        grid_spec=grid_spec,
        out_shape=jax.ShapeDtypeStruct((p_rows, dw), U32),
        compiler_params=_cparams(("arbitrary",)),
        name="experts",
    )(block_exp, n_used, xin, wgu, bgu, wd, bd)


def _final_kernel(yg_ref, g_ref, x1_ref, gt_ref, gpost_ref, *rest):
    o_ref = rest[-1]
    g = g_ref[...]
    f_lo, f_hi = None, None
    for k in range(TOP_K):
        lo, hi = _unpack_rows(yg_ref[k])
        gk = g[:, k:k + 1]
        f_lo = lo * gk if f_lo is None else f_lo + lo * gk
        f_hi = hi * gk if f_hi is None else f_hi + hi * gk
    f = jnp.concatenate([f_lo, f_hi], axis=1)
    o_ref[...] = x1_ref[...] + gt_ref[0] * _rms(f, gpost_ref[...])


def _final(yg, gates_t, x1, gt2, g_post, seq, tile_off, t_all, prev_out):
    t, d = x1.shape
    tpb = seq // TM
    row = lambda i: (i, 0)
    in_specs = [pl.BlockSpec((TOP_K, TM, d // 2), lambda i: (0, i, 0)),
                pl.BlockSpec((TM, TOP_K), row),
                pl.BlockSpec((TM, d), row),
                pl.BlockSpec((1, 1, d), lambda i: ((i + tile_off) // tpb, 0, 0)),
                pl.BlockSpec((1, d), lambda i: (0, 0))]
    args = [yg, gates_t, x1, gt2, g_post]
    aliases = {}
    if prev_out is not None:
        in_specs.append(pl.BlockSpec(memory_space=pl.ANY))
        args.append(prev_out)
        aliases = {len(args) - 1: 0}
    return pl.pallas_call(
        _final_kernel,
        grid=(t // TM,),
        in_specs=in_specs,
        out_specs=pl.BlockSpec((TM, d), lambda i: (i + tile_off, 0)),
        out_shape=jax.ShapeDtypeStruct((t_all, d), F32),
        input_output_aliases=aliases,
        compiler_params=_cparams(("arbitrary",)),
        name="final",
    )(*args)


def _sc_mesh():
    return plsc.VectorSubcoreMesh(core_axis_name="c", subcore_axis_name="s")


def _sc_worker():
    return lax.axis_index("s") * SC_CORES + lax.axis_index("c")


def _dispatch(h2, dest, p_rows):
    t, dw = h2.shape
    per_w = t // SC_WORKERS
    n_win = per_w // SC_WIN

    @functools.partial(
        pl.kernel, mesh=_sc_mesh(),
        out_type=jax.ShapeDtypeStruct((p_rows, dw), h2.dtype),
        scratch_types=[pltpu.VMEM((TOP_K, SC_WIN), jnp.int32),
                       pltpu.VMEM((SC_WIN, dw), h2.dtype),
                       pltpu.SemaphoreType.DMA],
        name="dispatch",
    )
    def run(h_hbm, d_hbm, o_hbm, idx_v, rows_v, sem):
        wid = _sc_worker()

        @pl.loop(0, n_win)
        def _(wi):
            base = pl.multiple_of(wid * per_w + wi * SC_WIN, SC_WIN)
            for k in range(TOP_K):
                pltpu.sync_copy(d_hbm.at[pl.ds(k * t + base, SC_WIN)], idx_v.at[k])
            pltpu.sync_copy(h_hbm.at[pl.ds(base, SC_WIN)], rows_v)
            for k in range(TOP_K):
                pltpu.async_copy(rows_v, o_hbm.at[idx_v.at[k]], sem).wait()

    return run(h2, dest)


def _gather_rows(y, dest, t):
    _, dw = y.shape
    per_w = t // SC_WORKERS
    n_win = per_w // SC_WIN

    @functools.partial(
        pl.kernel, mesh=_sc_mesh(),
        out_type=jax.ShapeDtypeStruct((TOP_K, t, dw), y.dtype),
        scratch_types=[pltpu.VMEM((SC_WIN,), jnp.int32),
                       pltpu.VMEM((SC_WIN, dw), y.dtype),
                       pltpu.SemaphoreType.DMA],
        name="gather_rows",
    )
    def run(y_hbm, d_hbm, o_hbm, idx_v, rows_v, sem):
        wid = _sc_worker()

        @pl.loop(0, n_win)
        def _(wi):
            base = pl.multiple_of(wid * per_w + wi * SC_WIN, SC_WIN)
            for k in range(TOP_K):
                pltpu.sync_copy(d_hbm.at[pl.ds(k * t + base, SC_WIN)], idx_v)
                pltpu.async_copy(y_hbm.at[idx_v], rows_v, sem).wait()
                pltpu.sync_copy(rows_v, o_hbm.at[k, pl.ds(base, SC_WIN)])

    return run(y, dest)


def _prep_weights(w_in, g_q, w_qb, g_kv, w_kvb, w_o, w_router, b_router):
    d = w_in.shape[0]
    o = 0
    w_cq = w_in[:, o:o + Q_LORA]; o += Q_LORA
    w_ckv = w_in[:, o:o + KV_LORA]; o += KV_LORA
    w_kr = w_in[:, o:o + MLA_ROPE]; o += MLA_ROPE
    w_ca = w_in[:, o:]
    half = MLA_ROPE // 2
    zpad = lambda n: jnp.zeros((d, n), w_in.dtype)
    x1, x2 = w_kr[:, :half], w_kr[:, half:]
    tail = HEAD_PAD - MLA_NOPE - MLA_ROPE
    w_kr1 = jnp.concatenate([zpad(MLA_NOPE), x1, x2, zpad(tail)], axis=1)
    w_kr2 = jnp.concatenate([zpad(MLA_NOPE), -x2, x1, zpad(tail)], axis=1)
    w_in_all = jnp.concatenate([w_cq, w_ckv, w_kr1, w_kr2, w_ca], axis=1).astype(BF16)

    wq = w_qb.reshape(Q_LORA, MLA_HEADS, MLA_NOPE + MLA_ROPE)
    qn, q1, q2 = wq[..., :MLA_NOPE], wq[..., MLA_NOPE:MLA_NOPE + half], wq[..., MLA_NOPE + half:]
    zq = jnp.zeros((Q_LORA, MLA_HEADS, tail), w_qb.dtype)
    wq1 = jnp.concatenate([qn, q1, q2, zq], axis=-1).reshape(Q_LORA, -1).astype(BF16)
    wq2 = jnp.concatenate([jnp.zeros_like(qn), -q2, q1, zq], axis=-1).reshape(Q_LORA, -1).astype(BF16)

    wkv = w_kvb.reshape(KV_LORA, MLA_HEADS, MLA_NOPE + MLA_V)
    kn = wkv[..., :MLA_NOPE]
    wk = jnp.concatenate([kn, jnp.zeros((KV_LORA, MLA_HEADS, HEAD_PAD - MLA_NOPE), w_kvb.dtype)],
                         axis=-1).reshape(KV_LORA, -1).astype(BF16)
    wv = jnp.concatenate([wkv[..., MLA_NOPE:], jnp.zeros((KV_LORA, MLA_HEADS, HEAD_PAD - MLA_V), w_kvb.dtype)],
                         axis=-1).reshape(KV_LORA, -1).T.astype(BF16)
    v_one = jnp.tile((jnp.arange(HEAD_PAD) == MLA_V).astype(F32), MLA_HEADS).reshape(-1, 1)
    mla_w = MLA_HEADS * MLA_V
    return {
        "w_in": w_in_all, "g_q": g_q.reshape(1, -1), "wq1": wq1, "wq2": wq2,
        "g_kv": g_kv.reshape(1, -1), "wk": wk, "wv": wv, "v_one": v_one,
        "wo_a": w_o[:mla_w].astype(BF16), "wo_b": w_o[mla_w:].astype(BF16),
        "wr_t": w_router.T.astype(BF16), "b_r": b_router.reshape(-1, 1),
    }


def _rope_tables(seq):
    half = MLA_ROPE // 2
    inv_freq = ROPE_THETA ** (-jnp.arange(half, dtype=F32) / half)
    ang = jnp.arange(seq, dtype=F32)[:, None] * inv_freq[None, :]
    cos, sin = jnp.cos(ang), jnp.sin(ang)
    tail = HEAD_PAD - MLA_NOPE - MLA_ROPE
    ones = jnp.ones((seq, MLA_NOPE), F32)
    zn = jnp.zeros((seq, MLA_NOPE), F32)
    zt = jnp.zeros((seq, tail), F32)
    qs = (MLA_NOPE + MLA_ROPE) ** -0.5 * LOG2E
    return {
        "cq": jnp.concatenate([ones, cos, cos, zt], axis=1) * qs,
        "sq": jnp.concatenate([zn, sin, sin, zt], axis=1) * qs,
        "ck": jnp.concatenate([zn, cos, cos, zt], axis=1),
        "sk": jnp.concatenate([zn, sin, sin, zt], axis=1),
    }


def _bias_table(rel_bias):
    n = CQ + CBAND - 1
    rel = (CBAND - 1) - jnp.arange(n)
    diag = rel_bias.astype(F32)[:, jnp.clip(rel, -(CHUNK - 1), REL_MAX) + (CHUNK - 1)] * LOG2E
    diag = jnp.concatenate([diag, jnp.zeros((diag.shape[0], 1), F32)], axis=1)
    b = jnp.tile(diag, (1, CQ))[:, :CQ * n].reshape(-1, CQ, n)[:, :, CQ - 1:]
    r = jnp.arange(CQ)[:, None]
    c = jnp.arange(CBAND)[None, :]
    dchunk = r // CHUNK - (c // CHUNK - CA_LEFT)
    visible = (dchunk >= 0) & (dchunk <= CA_LEFT)
    exists = (c // CQ)[None] >= (2 - jnp.arange(3))[:, None, None]
    return jnp.where((visible[None] & exists)[:, None], b[None], NEG)


def _layer(x, c, w_ada, b_ada, g_pre_mix, g_post_mix, g_pre_ffn, g_post_ffn, w_in, g_q, w_qb,
           g_kv, w_kvb, rel_bias, w_o, w_router, b_router, w_gu, b_gu, w_down, b_down):
    bsz, seq, d = x.shape
    t = bsz * seq
    mod = _ada(c, w_ada, b_ada).reshape(bsz, 6, 1, d)
    sh1, sc1, gt1, sh2, sc2, gt2 = [mod[:, k] for k in range(6)]
    w = _prep_weights(w_in, g_q, w_qb, g_kv, w_kvb, w_o, w_router, b_router)
    tabs = _rope_tables(seq)
    x2 = x.reshape(t, d)

    q, k, v, qc, kc, vc = _proj(x2, sc1, sh1, g_pre_mix.reshape(1, d), w, tabs, seq)
    shp = lambda a: a.reshape(bsz, seq, a.shape[-1])
    oa = _mla(shp(q), shp(k), v).reshape(t, -1)
    ob = _chunk_attn(shp(qc), shp(kc), vc, jnp.swapaxes(_bias_table(rel_bias), 2, 3)).reshape(t, -1)

    tp = t // MOE_PARTS
    eids = jnp.arange(N_EXPERTS, dtype=jnp.int32)[:, None, None]
    p_rows = tp * TOP_K + N_EXPERTS * ROWS
    block_start = jnp.arange(p_rows // ROWS, dtype=jnp.int32) * ROWS
    routed = []
    for part in range(MOE_PARTS):
        x1, h2, top_idx, gates, rank, cnt = _post(oa, ob, x2, gt1, sc2, sh2, g_post_mix.reshape(1, d),
                                                  g_pre_ffn.reshape(1, d), w, seq, part * (tp // TM), tp)
        counts = cnt[:, 0].astype(jnp.int32)
        padded = ((counts + ROWS - 1) // ROWS) * ROWS
        pend = jnp.cumsum(padded)
        pstart = pend - padded
        dest = (jnp.sum(jnp.where(top_idx[None] == eids, pstart[:, None, None], 0), axis=0) + rank).reshape(-1)
        block_exp = jnp.minimum(jnp.sum(pend[None, :] <= block_start[:, None], axis=1),
                                N_EXPERTS - 1).astype(jnp.int32)
        n_used = (pend[-1:] // ROWS).astype(jnp.int32)
        routed.append((x1, gates.T, dest, block_exp, n_used, _dispatch(h2, dest, p_rows)))

    out = None
    for part, (x1, gates_t, dest, block_exp, n_used, xin) in enumerate(routed):
        y = _experts(xin, block_exp, n_used, w_gu, b_gu.reshape(N_EXPERTS, 1, -1),
                     w_down, b_down.reshape(N_EXPERTS, 1, -1))
        yg = _gather_rows(y, dest, tp)
        out = _final(yg, gates_t, x1, gt2, g_post_ffn.reshape(1, d), seq, part * (tp // TM), t, out)
    return out.reshape(bsz, seq, d)


def kernel(x, c, w_ada, b_ada, g_pre_mix, g_post_mix, g_pre_ffn, g_post_ffn, w_in, g_q, w_qb,
           g_kv, w_kvb, rel_bias, w_o, w_router, b_router, w_gu, b_gu, w_down, b_down):
    for l in range(w_ada.shape[0]):
        x = _layer(x, c, w_ada[l], b_ada[l], g_pre_mix[l], g_post_mix[l], g_pre_ffn[l], g_post_ffn[l],
                   w_in[l], g_q[l], w_qb[l], g_kv[l], w_kvb[l], rel_bias[l], w_o[l], w_router[l],
                   b_router[l], w_gu[l], b_gu[l], w_down[l], b_down[l])
    return x
```

```python
import functools
import math

import jax
import jax.numpy as jnp
from jax import lax
from jax.experimental import pallas as pl
from jax.experimental.pallas import tpu as pltpu
from jax.experimental.pallas import tpu_sc as plsc

F32 = jnp.float32
BF16 = jnp.bfloat16
U32 = jnp.uint32

D_MODEL = 1024
CHUNK = 64
EPS = 1e-6
MLA_HEADS = 8
MLA_NOPE = 64
MLA_ROPE = 32
MLA_V = 64
Q_LORA = 256
KV_LORA = 128
ROPE_THETA = 10000.0
CA_HEADS = 8
CA_DIM = 64
CA_LEFT = 8
REL_MAX = 256
N_EXPERTS = 32
TOP_K = 4
D_EXPERT = 1024
SWIGLU_LIMIT = 7.0
SWIGLU_ALPHA = 1.702

LANES = 128
HEAD_PAD = 128
LOG2E = math.log2(math.e)
NEG = -1e30
VMEM_LIMIT = 56 * 1024 * 1024

TM = 512
POST_SUB = 128
TQ = 512
TKW = 2 * TQ
MLA_HPS = 4
MLA_LEAD = 2
CQ = 256
CBAND = 3 * CQ
ROWS = 512
EXPERT_SUB = 512
MOE_PARTS = 2
SC_CORES = 2
SC_WORKERS = SC_CORES * 16
SC_WIN = 128
CA_W = CA_HEADS * CA_DIM
W_IN_COLS = Q_LORA + KV_LORA + 2 * LANES + 3 * CA_W


def _cparams(sem, flags=None):
    return pltpu.CompilerParams(dimension_semantics=sem, vmem_limit_bytes=VMEM_LIMIT, flags=flags)


def _nt_dot(a, b):
    return lax.dot_general(a, b, (((1,), (1,)), ((), ())), preferred_element_type=F32)


def _rms(x, g):
    return x * lax.rsqrt(jnp.mean(x * x, axis=-1, keepdims=True) + EPS) * g


def _pack_rows(x):
    n = x.shape[1] // 2
    lo = lax.bitcast_convert_type(x[:, :n].astype(BF16).astype(F32), U32)
    hi = lax.bitcast_convert_type(x[:, n:].astype(BF16).astype(F32), U32)
    return (lo >> 16) | hi


def _unpack_rows(p):
    lo = lax.bitcast_convert_type(p << 16, F32)
    hi = lax.bitcast_convert_type(p & jnp.uint32(0xFFFF0000), F32)
    return lo, hi


def _ada_kernel(c_ref, w_ref, b_ref, o_ref):
    c = c_ref[...]
    a = (c / (1.0 + jnp.exp(-c))).astype(BF16)
    o_ref[...] = jnp.dot(a, w_ref[...].astype(BF16), preferred_element_type=F32) + b_ref[...]


def _ada(c, w, b):
    bsz, d = c.shape
    n = w.shape[1]
    tn = 1024
    return pl.pallas_call(
        _ada_kernel,
        grid=(n // tn,),
        in_specs=[pl.BlockSpec((bsz, d), lambda j: (0, 0)),
                  pl.BlockSpec((d, tn), lambda j: (0, j)),
                  pl.BlockSpec((1, tn), lambda j: (0, j))],
        out_specs=pl.BlockSpec((bsz, tn), lambda j: (0, j)),
        out_shape=jax.ShapeDtypeStruct((bsz, n), F32),
        compiler_params=_cparams(("arbitrary",)),
        name="ada",
    )(c, w, b.reshape(1, n))


def _proj_kernel(x_ref, sc_ref, sh_ref, g_ref, win_ref, gq_ref, wq1_ref, wq2_ref, gkv_ref,
                 wk_ref, wv_ref, vone_ref, cq_ref, sq_ref, ck_ref, sk_ref,
                 q_out, k_out, v_out, qc_out, kc_out, vc_out):
    h = _rms(x_ref[...], g_ref[...]) * (1.0 + sc_ref[0]) + sh_ref[0]
    z = jnp.dot(h.astype(BF16), win_ref[...], preferred_element_type=F32)
    o = 0
    cq = z[:, o:o + Q_LORA]; o += Q_LORA
    ckv = z[:, o:o + KV_LORA]; o += KV_LORA
    kr1 = z[:, o:o + LANES]; o += LANES
    kr2 = z[:, o:o + LANES]; o += LANES
    qc = z[:, o:o + CA_W]; o += CA_W
    kc = z[:, o:o + CA_W]; o += CA_W
    vc = z[:, o:o + CA_W]

    cqn = _rms(cq, gq_ref[...]).astype(BF16)
    q1 = jnp.dot(cqn, wq1_ref[...], preferred_element_type=F32)
    q2 = jnp.dot(cqn, wq2_ref[...], preferred_element_type=F32)
    ckvn = _rms(ckv, gkv_ref[...]).astype(BF16)
    kn = jnp.dot(ckvn, wk_ref[...], preferred_element_type=F32)
    v_out[0] = (_nt_dot(wv_ref[...], ckvn) + vone_ref[...]).astype(BF16)

    cq_t, sq_t = cq_ref[...], sq_ref[...]
    krope = kr1 * ck_ref[...] + kr2 * sk_ref[...]
    for hd in range(MLA_HEADS):
        sl = slice(hd * HEAD_PAD, (hd + 1) * HEAD_PAD)
        q_out[:, sl] = (q1[:, sl] * cq_t + q2[:, sl] * sq_t).astype(BF16)
        k_out[:, sl] = (kn[:, sl] + krope).astype(BF16)

    qc_out[...] = (qc * (CA_DIM ** -0.5 * LOG2E)).astype(BF16)
    kc_out[...] = kc.astype(BF16)
    vc_out[...] = vc.astype(BF16)


def _proj(x2, sc1, sh1, g_pre, w, tabs, seq):
    t, d = x2.shape
    tpb = seq // TM
    row = lambda i: (i, 0)
    full = lambda i: (0, 0)
    per_b = lambda i: (i // tpb, 0, 0)
    pos = lambda i: (i % tpb, 0)
    hw = MLA_HEADS * HEAD_PAD
    in_specs = [
        pl.BlockSpec((TM, d), row),
        pl.BlockSpec((1, 1, d), per_b), pl.BlockSpec((1, 1, d), per_b),
        pl.BlockSpec((1, d), full),
        pl.BlockSpec((d, W_IN_COLS), full),
        pl.BlockSpec((1, Q_LORA), full),
        pl.BlockSpec((Q_LORA, hw), full), pl.BlockSpec((Q_LORA, hw), full),
        pl.BlockSpec((1, KV_LORA), full),
        pl.BlockSpec((KV_LORA, hw), full), pl.BlockSpec((hw, KV_LORA), full), pl.BlockSpec((hw, 1), full),
        pl.BlockSpec((TM, LANES), pos), pl.BlockSpec((TM, LANES), pos),
        pl.BlockSpec((TM, LANES), pos), pl.BlockSpec((TM, LANES), pos),
    ]
    outs = ((hw, False), (hw, False), (hw, True), (CA_W, False), (CA_W, False), (CA_W, False))
    t_spec = lambda n: pl.BlockSpec((1, n, TM), lambda i: (i // tpb, 0, i % tpb))
    return pl.pallas_call(
        _proj_kernel,
        grid=(t // TM,),
        in_specs=in_specs,
        out_specs=[t_spec(n) if tr else pl.BlockSpec((TM, n), row) for n, tr in outs],
        out_shape=[jax.ShapeDtypeStruct((t // seq, n, seq) if tr else (t, n), BF16) for n, tr in outs],
        compiler_params=_cparams(("arbitrary",)),
        name="proj",
    )(x2, sc1, sh1, g_pre, w["w_in"], w["g_q"], w["wq1"], w["wq2"], w["g_kv"], w["wk"], w["wv"], w["v_one"],
      tabs["cq"], tabs["sq"], tabs["ck"], tabs["sk"])


def _mla_kernel(q_ref, k_ref, v_ref, o_ref):
    i = pl.program_id(2)
    heads = [slice(hh * HEAD_PAD, (hh + 1) * HEAD_PAD) for hh in range(MLA_HPS)]

    def step(off, width, carry, masked=False):
        def score(hs):
            s = _nt_dot(k_ref[0, pl.ds(off, width), hs], q_ref[0, :, hs])
            if masked:
                kc = lax.broadcasted_iota(jnp.int32, (width, TQ), 0) // CHUNK
                qc = lax.broadcasted_iota(jnp.int32, (width, TQ), 1) // CHUNK
                s = jnp.where(kc <= qc, s, NEG)
            return s

        scores = [score(hs) for hs in heads[:MLA_LEAD]]
        new = []
        for hh, hs in enumerate(heads):
            m, acc = carry[hh]
            m_new = jnp.maximum(m, jnp.max(scores[hh], axis=0, keepdims=True))
            p = jnp.exp2(scores[hh] - m_new).astype(BF16)
            if hh + MLA_LEAD < MLA_HPS:
                scores.append(score(heads[hh + MLA_LEAD]))
            pv = jnp.dot(v_ref[0, hs, pl.ds(off, width)], p, preferred_element_type=F32)
            new.append((m_new, jnp.exp2(m - m_new) * acc + pv))
        return tuple(new)

    init = tuple((jnp.full((1, TQ), NEG, F32), jnp.zeros((HEAD_PAD, TQ), F32)) for _ in heads)
    carry = lax.fori_loop(0, i // 2, lambda j, c: step(pl.multiple_of(j * TKW, TKW), TKW, c), init)
    carry = lax.fori_loop(0, i % 2, lambda _, c: step(pl.multiple_of((i - 1) * TQ, TQ), TQ, c), carry)
    carry = step(pl.multiple_of(i * TQ, TQ), TQ, carry, True)
    for pp in range(MLA_HPS // 2):
        pair = [acc[:MLA_V] / acc[MLA_V:MLA_V + 1] for _, acc in (carry[2 * pp], carry[2 * pp + 1])]
        o_ref[0, :, pp * LANES:(pp + 1) * LANES] = jnp.concatenate(pair, axis=0).T.astype(BF16)


def _mla(q, k, v):
    bsz, seq, _ = q.shape
    groups = MLA_HEADS // MLA_HPS
    return pl.pallas_call(
        _mla_kernel,
        grid=(bsz, groups, seq // TQ),
        in_specs=[pl.BlockSpec((1, TQ, MLA_HPS * HEAD_PAD), lambda b, p, i: (b, i, p)),
                  pl.BlockSpec((1, seq, MLA_HPS * HEAD_PAD), lambda b, p, i: (b, 0, p)),
                  pl.BlockSpec((1, MLA_HPS * HEAD_PAD, seq), lambda b, p, i: (b, p, 0))],
        out_specs=pl.BlockSpec((1, TQ, MLA_HPS * MLA_V), lambda b, p, i: (b, i, p)),
        out_shape=jax.ShapeDtypeStruct((bsz, seq, MLA_HEADS * MLA_V), BF16),
        compiler_params=_cparams(("arbitrary", "arbitrary", "arbitrary")),
        name="mla",
    )(q, k, v)


def _chunk_kernel(q_ref, k0_ref, k1_ref, k2_ref, v0_ref, v1_ref, v2_ref, bias_ref, o_ref):
    lane = lax.broadcasted_iota(jnp.int32, (CQ, LANES), 1)
    lo = lane < CA_DIM
    k_refs = (k0_ref, k1_ref, k2_ref)
    v_refs = (v0_ref, v1_ref, v2_ref)
    pairs = CA_HEADS // 2

    def score(p):
        sl = slice(p * LANES, (p + 1) * LANES)
        q = q_ref[0, :, sl]
        zero = jnp.zeros_like(q)
        q2 = jnp.concatenate([jnp.where(lo, q, zero), jnp.where(lo, zero, q)], axis=0)
        return jnp.concatenate([_nt_dot(q2, kr[0, :, sl]) for kr in k_refs], axis=1)

    scores = [score(p) for p in range(pairs)]
    for p in range(pairs):
        sl = slice(p * LANES, (p + 1) * LANES)
        s = scores[p] + bias_ref[0, p]
        m = jnp.max(s, axis=-1, keepdims=True)
        e = jnp.exp2(s - m)
        l = jnp.sum(e, axis=-1, keepdims=True)
        pb = e.astype(BF16)
        o = None
        for cb, vr in enumerate(v_refs):
            part = jnp.dot(pb[:, cb * CQ:(cb + 1) * CQ], vr[0, :, sl], preferred_element_type=F32)
            o = part if o is None else o + part
        o = o / l
        o_ref[0, :, sl] = jnp.where(lo, o[:CQ], o[CQ:]).astype(BF16)


def _chunk_attn(qc, kc, vc, bias):
    bsz, seq, w = qc.shape
    blk = lambda off: pl.BlockSpec((1, CQ, w), lambda b, i: (b, jnp.maximum(i + off, 0), 0))
    return pl.pallas_call(
        _chunk_kernel,
        grid=(bsz, seq // CQ),
        in_specs=[blk(0), blk(-2), blk(-1), blk(0), blk(-2), blk(-1), blk(0),
                  pl.BlockSpec((1, CA_HEADS // 2, 2 * CQ, CBAND), lambda b, i: (jnp.minimum(i, 2), 0, 0, 0))],
        out_specs=pl.BlockSpec((1, CQ, w), lambda b, i: (b, i, 0)),
        out_shape=jax.ShapeDtypeStruct((bsz, seq, w), BF16),
        compiler_params=_cparams(("arbitrary", "arbitrary")),
        name="chunk_attn",
    )(qc, kc, kc, kc, vc, vc, vc, bias)


def _post_kernel(oa_ref, ob_ref, x_ref, gt_ref, sc_ref, sh_ref, gpost_ref, gpre_ref,
                 woa_ref, wob_ref, wr_ref, br_ref,
                 x1_out, h2_out, idx_out, gate_out, rank_out, cnt_out, carry_ref):
    t = pl.program_id(0)

    @pl.when(t == 0)
    def _():
        carry_ref[...] = jnp.zeros_like(carry_ref)

    os = []
    for rs in range(0, TM, POST_SUB):
        rows = slice(rs, rs + POST_SUB)
        o = jnp.dot(oa_ref[rows], woa_ref[...], preferred_element_type=F32)
        os.append(o + jnp.dot(ob_ref[rows], wob_ref[...], preferred_element_type=F32))
    h2s = []
    gain1 = gt_ref[0] * gpost_ref[...]
    gain2 = gpre_ref[...] * (1.0 + sc_ref[0])
    for sub, o in enumerate(os):
        rows = slice(sub * POST_SUB, (sub + 1) * POST_SUB)
        x1 = x_ref[rows] + _rms(o, gain1)
        x1_out[rows] = x1
        h2f = _rms(x1, gain2) + sh_ref[0]
        h2_out[rows] = _pack_rows(h2f)
        h2s.append(h2f.astype(BF16))
    h2 = jnp.concatenate(h2s, axis=0)

    logits = _nt_dot(wr_ref[...], h2) + br_ref[...]
    eid = lax.broadcasted_iota(jnp.int32, (N_EXPERTS, TM), 0)
    vals, idxs = [], []
    work = logits
    for _k in range(TOP_K):
        m = jnp.max(work, axis=0, keepdims=True)
        ix = jnp.min(jnp.where(work == m, eid, N_EXPERTS), axis=0, keepdims=True)
        work = jnp.where(eid == ix, -jnp.inf, work)
        vals.append(m)
        idxs.append(ix)
    es = [jnp.exp(v - vals[0]) for v in vals]
    den = es[0] + es[1] + es[2] + es[3]
    gate_out[...] = jnp.concatenate([e / den for e in es], axis=0)
    idx_out[...] = jnp.concatenate(idxs, axis=0)

    sel = (eid == idxs[0]) | (eid == idxs[1]) | (eid == idxs[2]) | (eid == idxs[3])
    self32 = sel.astype(F32)
    rr = lax.broadcasted_iota(jnp.int32, (TM, TM), 0)
    cc = lax.broadcasted_iota(jnp.int32, (TM, TM), 1)
    upper = (rr < cc).astype(BF16)
    before = jnp.dot(self32.astype(BF16), upper, preferred_element_type=F32)
    before = before + carry_ref[:, 0:1]
    ranks = [jnp.sum(jnp.where(eid == ix, before, 0.0), axis=0, keepdims=True) for ix in idxs]
    rank_out[...] = jnp.concatenate(ranks, axis=0).astype(jnp.int32)
    carry_ref[...] = carry_ref[...] + jnp.sum(self32, axis=1, keepdims=True)
    cnt_out[...] = carry_ref[...]


def _post(oa, ob, x2, gt1, sc2, sh2, g_post, g_pre, w, seq, tile_off, t):
    d = x2.shape[1]
    tpb = seq // TM
    row = lambda i: (i, 0)
    src = lambda i: (i + tile_off, 0)
    col = lambda i: (0, i)
    full = lambda i: (0, 0)
    per_b = lambda i: ((i + tile_off) // tpb, 0, 0)
    hw = oa.shape[1]
    in_specs = [
        pl.BlockSpec((TM, hw), src), pl.BlockSpec((TM, hw), src), pl.BlockSpec((TM, d), src),
        pl.BlockSpec((1, 1, d), per_b), pl.BlockSpec((1, 1, d), per_b), pl.BlockSpec((1, 1, d), per_b),
        pl.BlockSpec((1, d), full), pl.BlockSpec((1, d), full),
        pl.BlockSpec((hw, d), full), pl.BlockSpec((hw, d), full),
        pl.BlockSpec((N_EXPERTS, d), full), pl.BlockSpec((N_EXPERTS, 1), full),
    ]
    out_specs = [
        pl.BlockSpec((TM, d), row), pl.BlockSpec((TM, d // 2), row),
        pl.BlockSpec((TOP_K, TM), col), pl.BlockSpec((TOP_K, TM), col), pl.BlockSpec((TOP_K, TM), col),
        pl.BlockSpec((N_EXPERTS, LANES), full),
    ]
    out_shape = [
        jax.ShapeDtypeStruct((t, d), F32), jax.ShapeDtypeStruct((t, d // 2), U32),
        jax.ShapeDtypeStruct((TOP_K, t), jnp.int32), jax.ShapeDtypeStruct((TOP_K, t), F32),
        jax.ShapeDtypeStruct((TOP_K, t), jnp.int32),
        jax.ShapeDtypeStruct((N_EXPERTS, LANES), F32),
    ]
    return pl.pallas_call(
        _post_kernel,
        grid=(t // TM,),
        in_specs=in_specs,
        out_specs=out_specs,
        out_shape=out_shape,
        scratch_shapes=[pltpu.VMEM((N_EXPERTS, LANES), F32)],
        compiler_params=_cparams(("arbitrary",)),
        name="post",
    )(oa, ob, x2, gt1, sc2, sh2, g_post, g_pre, w["wo_a"], w["wo_b"], w["wr_t"], w["b_r"])


def _expert_kernel(be_ref, nu_ref, x_ref, wgu_ref, bgu_ref, wd_ref, bd_ref, y_ref, wgu_bf, wd_bf):
    j = pl.program_id(0)
    used = j < nu_ref[0]

    @pl.when(used & ((j == 0) | (be_ref[j] != be_ref[jnp.maximum(j - 1, 0)])))
    def _():
        wgu_bf[...] = wgu_ref[0].astype(BF16)
        wd_bf[...] = wd_ref[0].astype(BF16)

    @pl.when(used)
    def _():
        gus = []
        for rs in range(0, ROWS, EXPERT_SUB):
            x_lo, x_hi = _unpack_rows(x_ref[rs:rs + EXPERT_SUB])
            half = x_lo.shape[1]
            gu = jnp.dot(x_lo.astype(BF16), wgu_bf[:half], preferred_element_type=F32)
            gu += jnp.dot(x_hi.astype(BF16), wgu_bf[half:], preferred_element_type=F32)
            gus.append(gu + bgu_ref[0])
        for sub, gu in enumerate(gus):
            rows = slice(sub * EXPERT_SUB, (sub + 1) * EXPERT_SUB)
            gate = jnp.minimum(gu[:, :D_EXPERT], SWIGLU_LIMIT)
            up = jnp.clip(gu[:, D_EXPERT:], -SWIGLU_LIMIT, SWIGLU_LIMIT)
            glu = gate / (1.0 + jnp.exp(-SWIGLU_ALPHA * gate))
            act = ((up + 1.0) * glu).astype(BF16)
            y = jnp.dot(act, wd_bf[...], preferred_element_type=F32) + bd_ref[0]
            y_ref[rows] = _pack_rows(y)

    @pl.when(j >= nu_ref[0])
    def _():
        y_ref[...] = jnp.zeros_like(y_ref)


def _experts(xin, block_exp, n_used, wgu, bgu, wd, bd):
    p_rows, dw = xin.shape
    d = 2 * dw
    nb = p_rows // ROWS
    f2 = wgu.shape[2]
    grid_spec = pltpu.PrefetchScalarGridSpec(
        num_scalar_prefetch=2,
        grid=(nb,),
        in_specs=[
            pl.BlockSpec((ROWS, dw), lambda j, be, nu: (jnp.minimum(j, nu[0] - 1), 0)),
            pl.BlockSpec((1, d, f2), lambda j, be, nu: (be[j], 0, 0)),
            pl.BlockSpec((1, 1, f2), lambda j, be, nu: (be[j], 0, 0)),
            pl.BlockSpec((1, f2 // 2, d), lambda j, be, nu: (be[j], 0, 0)),
            pl.BlockSpec((1, 1, d), lambda j, be, nu: (be[j], 0, 0)),
        ],
        out_specs=pl.BlockSpec((ROWS, dw), lambda j, be, nu: (j, 0)),
        scratch_shapes=[pltpu.VMEM((d, f2), BF16), pltpu.VMEM((f2 // 2, d), BF16)],
    )
    return pl.pallas_call(
        _expert_kernel,
        grid_spec=grid_spec,
        out_shape=jax.ShapeDtypeStruct((p_rows, dw), U32),
        compiler_params=_cparams(("arbitrary",)),
        name="experts",
    )(block_exp, n_used, xin, wgu, bgu, wd, bd)


def _final_kernel(yg_ref, g_ref, x1_ref, gt_ref, gpost_ref, *rest):
    o_ref = rest[-1]
    g = g_ref[...]
    f_lo, f_hi = None, None
    for k in range(TOP_K):
        lo, hi = _unpack_rows(yg_ref[k])
        gk = g[:, k:k + 1]
        f_lo = lo * gk if f_lo is None else f_lo + lo * gk
        f_hi = hi * gk if f_hi is None else f_hi + hi * gk
    f = jnp.concatenate([f_lo, f_hi], axis=1)
    o_ref[...] = x1_ref[...] + gt_ref[0] * _rms(f, gpost_ref[...])


def _final(yg, gates_t, x1, gt2, g_post, seq, tile_off, t_all, prev_out):
    t, d = x1.shape
    tpb = seq // TM
    row = lambda i: (i, 0)
    in_specs = [pl.BlockSpec((TOP_K, TM, d // 2), lambda i: (0, i, 0)),
                pl.BlockSpec((TM, TOP_K), row),
                pl.BlockSpec((TM, d), row),
                pl.BlockSpec((1, 1, d), lambda i: ((i + tile_off) // tpb, 0, 0)),
                pl.BlockSpec((1, d), lambda i: (0, 0))]
    args = [yg, gates_t, x1, gt2, g_post]
    aliases = {}
    if prev_out is not None:
        in_specs.append(pl.BlockSpec(memory_space=pl.ANY))
        args.append(prev_out)
        aliases = {len(args) - 1: 0}
    return pl.pallas_call(
        _final_kernel,
        grid=(t // TM,),
        in_specs=in_specs,
        out_specs=pl.BlockSpec((TM, d), lambda i: (i + tile_off, 0)),
        out_shape=jax.ShapeDtypeStruct((t_all, d), F32),
        input_output_aliases=aliases,
        compiler_params=_cparams(("arbitrary",)),
        name="final",
    )(*args)


def _sc_mesh():
    return plsc.VectorSubcoreMesh(core_axis_name="c", subcore_axis_name="s")


def _sc_worker():
    return lax.axis_index("s") * SC_CORES + lax.axis_index("c")


def _dispatch(h2, dest, p_rows):
    t, dw = h2.shape
    per_w = t // SC_WORKERS
    n_win = per_w // SC_WIN

    @functools.partial(
        pl.kernel, mesh=_sc_mesh(),
        out_type=jax.ShapeDtypeStruct((p_rows, dw), h2.dtype),
        scratch_types=[pltpu.VMEM((TOP_K, SC_WIN), jnp.int32),
                       pltpu.VMEM((SC_WIN, dw), h2.dtype),
                       pltpu.SemaphoreType.DMA],
        name="dispatch",
    )
    def run(h_hbm, d_hbm, o_hbm, idx_v, rows_v, sem):
        wid = _sc_worker()

        @pl.loop(0, n_win)
        def _(wi):
            base = pl.multiple_of(wid * per_w + wi * SC_WIN, SC_WIN)
            for k in range(TOP_K):
                pltpu.sync_copy(d_hbm.at[pl.ds(k * t + base, SC_WIN)], idx_v.at[k])
            pltpu.sync_copy(h_hbm.at[pl.ds(base, SC_WIN)], rows_v)
            for k in range(TOP_K):
                pltpu.async_copy(rows_v, o_hbm.at[idx_v.at[k]], sem).wait()

    return run(h2, dest)


def _gather_rows(y, dest, t):
    _, dw = y.shape
    per_w = t // SC_WORKERS
    n_win = per_w // SC_WIN

    @functools.partial(
        pl.kernel, mesh=_sc_mesh(),
        out_type=jax.ShapeDtypeStruct((TOP_K, t, dw), y.dtype),
        scratch_types=[pltpu.VMEM((SC_WIN,), jnp.int32),
                       pltpu.VMEM((SC_WIN, dw), y.dtype),
                       pltpu.SemaphoreType.DMA],
        name="gather_rows",
    )
    def run(y_hbm, d_hbm, o_hbm, idx_v, rows_v, sem):
        wid = _sc_worker()

        @pl.loop(0, n_win)
        def _(wi):
            base = pl.multiple_of(wid * per_w + wi * SC_WIN, SC_WIN)
            for k in range(TOP_K):
                pltpu.sync_copy(d_hbm.at[pl.ds(k * t + base, SC_WIN)], idx_v)
                pltpu.async_copy(y_hbm.at[idx_v], rows_v, sem).wait()
                pltpu.sync_copy(rows_v, o_hbm.at[k, pl.ds(base, SC_WIN)])

    return run(y, dest)


def _prep_weights(w_in, g_q, w_qb, g_kv, w_kvb, w_o, w_router, b_router):
    d = w_in.shape[0]
    o = 0
    w_cq = w_in[:, o:o + Q_LORA]; o += Q_LORA
    w_ckv = w_in[:, o:o + KV_LORA]; o += KV_LORA
    w_kr = w_in[:, o:o + MLA_ROPE]; o += MLA_ROPE
    w_ca = w_in[:, o:]
    half = MLA_ROPE // 2
    zpad = lambda n: jnp.zeros((d, n), w_in.dtype)
    x1, x2 = w_kr[:, :half], w_kr[:, half:]
    tail = HEAD_PAD - MLA_NOPE - MLA_ROPE
    w_kr1 = jnp.concatenate([zpad(MLA_NOPE), x1, x2, zpad(tail)], axis=1)
    w_kr2 = jnp.concatenate([zpad(MLA_NOPE), -x2, x1, zpad(tail)], axis=1)
    w_in_all = jnp.concatenate([w_cq, w_ckv, w_kr1, w_kr2, w_ca], axis=1).astype(BF16)

    wq = w_qb.reshape(Q_LORA, MLA_HEADS, MLA_NOPE + MLA_ROPE)
    qn, q1, q2 = wq[..., :MLA_NOPE], wq[..., MLA_NOPE:MLA_NOPE + half], wq[..., MLA_NOPE + half:]
    zq = jnp.zeros((Q_LORA, MLA_HEADS, tail), w_qb.dtype)
    wq1 = jnp.concatenate([qn, q1, q2, zq], axis=-1).reshape(Q_LORA, -1).astype(BF16)
    wq2 = jnp.concatenate([jnp.zeros_like(qn), -q2, q1, zq], axis=-1).reshape(Q_LORA, -1).astype(BF16)

    wkv = w_kvb.reshape(KV_LORA, MLA_HEADS, MLA_NOPE + MLA_V)
    kn = wkv[..., :MLA_NOPE]
    wk = jnp.concatenate([kn, jnp.zeros((KV_LORA, MLA_HEADS, HEAD_PAD - MLA_NOPE), w_kvb.dtype)],
                         axis=-1).reshape(KV_LORA, -1).astype(BF16)
    wv = jnp.concatenate([wkv[..., MLA_NOPE:], jnp.zeros((KV_LORA, MLA_HEADS, HEAD_PAD - MLA_V), w_kvb.dtype)],
                         axis=-1).reshape(KV_LORA, -1).T.astype(BF16)
    v_one = jnp.tile((jnp.arange(HEAD_PAD) == MLA_V).astype(F32), MLA_HEADS).reshape(-1, 1)
    mla_w = MLA_HEADS * MLA_V
    return {
        "w_in": w_in_all, "g_q": g_q.reshape(1, -1), "wq1": wq1, "wq2": wq2,
        "g_kv": g_kv.reshape(1, -1), "wk": wk, "wv": wv, "v_one": v_one,
        "wo_a": w_o[:mla_w].astype(BF16), "wo_b": w_o[mla_w:].astype(BF16),
        "wr_t": w_router.T.astype(BF16), "b_r": b_router.reshape(-1, 1),
    }


def _rope_tables(seq):
    half = MLA_ROPE // 2
    inv_freq = ROPE_THETA ** (-jnp.arange(half, dtype=F32) / half)
    ang = jnp.arange(seq, dtype=F32)[:, None] * inv_freq[None, :]
    cos, sin = jnp.cos(ang), jnp.sin(ang)
    tail = HEAD_PAD - MLA_NOPE - MLA_ROPE
    ones = jnp.ones((seq, MLA_NOPE), F32)
    zn = jnp.zeros((seq, MLA_NOPE), F32)
    zt = jnp.zeros((seq, tail), F32)
    qs = (MLA_NOPE + MLA_ROPE) ** -0.5 * LOG2E
    return {
        "cq": jnp.concatenate([ones, cos, cos, zt], axis=1) * qs,
        "sq": jnp.concatenate([zn, sin, sin, zt], axis=1) * qs,
        "ck": jnp.concatenate([zn, cos, cos, zt], axis=1),
        "sk": jnp.concatenate([zn, sin, sin, zt], axis=1),
    }


def _bias_table(rel_bias):
    n = CQ + CBAND - 1
    rel = (CBAND - 1) - jnp.arange(n)
    diag = rel_bias.astype(F32)[:, jnp.clip(rel, -(CHUNK - 1), REL_MAX) + (CHUNK - 1)] * LOG2E
    diag = jnp.concatenate([diag, jnp.zeros((diag.shape[0], 1), F32)], axis=1)
    b = jnp.tile(diag, (1, CQ))[:, :CQ * n].reshape(-1, CQ, n)[:, :, CQ - 1:]
    r = jnp.arange(CQ)[:, None]
    c = jnp.arange(CBAND)[None, :]
    dchunk = r // CHUNK - (c // CHUNK - CA_LEFT)
    visible = (dchunk >= 0) & (dchunk <= CA_LEFT)
    exists = (c // CQ)[None] >= (2 - jnp.arange(3))[:, None, None]
    return jnp.where((visible[None] & exists)[:, None], b[None], NEG)


def _layer(x, c, w_ada, b_ada, g_pre_mix, g_post_mix, g_pre_ffn, g_post_ffn, w_in, g_q, w_qb,
           g_kv, w_kvb, rel_bias, w_o, w_router, b_router, w_gu, b_gu, w_down, b_down):
    bsz, seq, d = x.shape
    t = bsz * seq
    mod = _ada(c, w_ada, b_ada).reshape(bsz, 6, 1, d)
    sh1, sc1, gt1, sh2, sc2, gt2 = [mod[:, k] for k in range(6)]
    w = _prep_weights(w_in, g_q, w_qb, g_kv, w_kvb, w_o, w_router, b_router)
    tabs = _rope_tables(seq)
    x2 = x.reshape(t, d)

    q, k, v, qc, kc, vc = _proj(x2, sc1, sh1, g_pre_mix.reshape(1, d), w, tabs, seq)
    shp = lambda a: a.reshape(bsz, seq, a.shape[-1])
    oa = _mla(shp(q), shp(k), v).reshape(t, -1)
    bias = _bias_table(rel_bias).reshape(3, CA_HEADS // 2, 2 * CQ, CBAND)
    ob = _chunk_attn(shp(qc), shp(kc), shp(vc), bias).reshape(t, -1)

    tp = t // MOE_PARTS
    eids = jnp.arange(N_EXPERTS, dtype=jnp.int32)[:, None, None]
    p_rows = tp * TOP_K + N_EXPERTS * ROWS
    block_start = jnp.arange(p_rows // ROWS, dtype=jnp.int32) * ROWS
    routed = []
    for part in range(MOE_PARTS):
        x1, h2, top_idx, gates, rank, cnt = _post(oa, ob, x2, gt1, sc2, sh2, g_post_mix.reshape(1, d),
                                                  g_pre_ffn.reshape(1, d), w, seq, part * (tp // TM), tp)
        counts = cnt[:, 0].astype(jnp.int32)
        padded = ((counts + ROWS - 1) // ROWS) * ROWS
        pend = jnp.cumsum(padded)
        pstart = pend - padded
        dest = (jnp.sum(jnp.where(top_idx[None] == eids, pstart[:, None, None], 0), axis=0) + rank).reshape(-1)
        block_exp = jnp.minimum(jnp.sum(pend[None, :] <= block_start[:, None], axis=1),
                                N_EXPERTS - 1).astype(jnp.int32)
        n_used = (pend[-1:] // ROWS).astype(jnp.int32)
        routed.append((x1, gates.T, dest, block_exp, n_used, _dispatch(h2, dest, p_rows)))

    out = None
    for part, (x1, gates_t, dest, block_exp, n_used, xin) in enumerate(routed):
        y = _experts(xin, block_exp, n_used, w_gu, b_gu.reshape(N_EXPERTS, 1, -1),
                     w_down, b_down.reshape(N_EXPERTS, 1, -1))
        yg = _gather_rows(y, dest, tp)
        out = _final(yg, gates_t, x1, gt2, g_post_ffn.reshape(1, d), seq, part * (tp // TM), t, out)
    return out.reshape(bsz, seq, d)


def kernel(x, c, w_ada, b_ada, g_pre_mix, g_post_mix, g_pre_ffn, g_post_ffn, w_in, g_q, w_qb,
           g_kv, w_kvb, rel_bias, w_o, w_router, b_router, w_gu, b_gu, w_down, b_down):
    for l in range(w_ada.shape[0]):
        x = _layer(x, c, w_ada[l], b_ada[l], g_pre_mix[l], g_post_mix[l], g_pre_ffn[l], g_post_ffn[l],
                   w_in[l], g_q[l], w_qb[l], g_kv[l], w_kvb[l], rel_bias[l], w_o[l], w_router[l],
                   b_router[l], w_gu[l], b_gu[l], w_down[l], b_down[l])
    return x
```

```python
import functools
import math

import jax
import jax.numpy as jnp
from jax import lax
from jax.experimental import pallas as pl
from jax.experimental.pallas import tpu as pltpu
from jax.experimental.pallas import tpu_sc as plsc

F32 = jnp.float32
BF16 = jnp.bfloat16
U32 = jnp.uint32

D_MODEL = 1024
CHUNK = 64
EPS = 1e-6
MLA_HEADS = 8
MLA_NOPE = 64
MLA_ROPE = 32
MLA_V = 64
Q_LORA = 256
KV_LORA = 128
ROPE_THETA = 10000.0
CA_HEADS = 8
CA_DIM = 64
CA_LEFT = 8
REL_MAX = 256
N_EXPERTS = 32
TOP_K = 4
D_EXPERT = 1024
SWIGLU_LIMIT = 7.0
SWIGLU_ALPHA = 1.702

LANES = 128
HEAD_PAD = 128
LOG2E = math.log2(math.e)
NEG = -1e30
VMEM_LIMIT = 56 * 1024 * 1024

TM = 512
POST_SUB = 128
PTM = 1024
PROJ_SUB = 1024
TQ = 512
TKW = 2 * TQ
MLA_HPS = 4
MLA_LEAD = 2
CQ = 256
CBAND = 3 * CQ
ROWS = 512
EXPERT_SUB = 256
MOE_PARTS = 2
SC_CORES = 2
SC_WORKERS = SC_CORES * 16
SC_WIN = 128
CA_W = CA_HEADS * CA_DIM
W_IN_COLS = Q_LORA + KV_LORA + LANES + 3 * CA_W


def _cparams(sem, flags=None):
    return pltpu.CompilerParams(dimension_semantics=sem, vmem_limit_bytes=VMEM_LIMIT, flags=flags)


def _nt_dot(a, b):
    return lax.dot_general(a, b, (((1,), (1,)), ((), ())), preferred_element_type=F32)


def _rms(x, g):
    return x * lax.rsqrt(jnp.mean(x * x, axis=-1, keepdims=True) + EPS) * g


def _pack_rows(x):
    n = x.shape[1] // 2
    lo = lax.bitcast_convert_type(x[:, :n].astype(BF16).astype(F32), U32)
    hi = lax.bitcast_convert_type(x[:, n:].astype(BF16).astype(F32), U32)
    return (lo >> 16) | hi


def _unpack_rows(p):
    lo = lax.bitcast_convert_type(p << 16, F32)
    hi = lax.bitcast_convert_type(p & jnp.uint32(0xFFFF0000), F32)
    return lo, hi


def _ada_kernel(c_ref, w_ref, b_ref, o_ref):
    c = c_ref[...]
    a = (c / (1.0 + jnp.exp(-c))).astype(BF16)
    o_ref[...] = jnp.dot(a, w_ref[...].astype(BF16), preferred_element_type=F32) + b_ref[...]


def _ada(c, w, b):
    bsz, d = c.shape
    n = w.shape[1]
    tn = 1024
    return pl.pallas_call(
        _ada_kernel,
        grid=(n // tn,),
        in_specs=[pl.BlockSpec((bsz, d), lambda j: (0, 0)),
                  pl.BlockSpec((d, tn), lambda j: (0, j)),
                  pl.BlockSpec((1, tn), lambda j: (0, j))],
        out_specs=pl.BlockSpec((bsz, tn), lambda j: (0, j)),
        out_shape=jax.ShapeDtypeStruct((bsz, n), F32),
        compiler_params=_cparams(("arbitrary",)),
        name="ada",
    )(c, w, b.reshape(1, n))


def _swap_halves(x):
    width = x.shape[1]
    half = MLA_ROPE // 2
    lane = lax.broadcasted_iota(jnp.int32, x.shape, 1) % HEAD_PAD
    return jnp.where(lane < MLA_NOPE + half, pltpu.roll(x, width - half, axis=1), pltpu.roll(x, half, axis=1))


def _proj_kernel(x_ref, sc_ref, sh_ref, g_ref, win_ref, gq_ref, wq_ref, gkv_ref,
                 wk_ref, wv_ref, vone_ref, cq_ref, sq_ref, ck_ref, sk_ref,
                 q_out, k_out, v_out, qc_out, kc_out, vc_out):
    zs = []
    for rs in range(0, PTM, PROJ_SUB):
        h = _rms(x_ref[rs:rs + PROJ_SUB], g_ref[...]) * (1.0 + sc_ref[0]) + sh_ref[0]
        zs.append(jnp.dot(h.astype(BF16), win_ref[...], preferred_element_type=F32))
    for sub, z in enumerate(zs):
        rows = slice(sub * PROJ_SUB, (sub + 1) * PROJ_SUB)
        o = 0
        cq = z[:, o:o + Q_LORA]; o += Q_LORA
        ckv = z[:, o:o + KV_LORA]; o += KV_LORA
        kr = z[:, o:o + LANES]; o += LANES
        qc = z[:, o:o + CA_W]; o += CA_W
        kc = z[:, o:o + CA_W]; o += CA_W
        vc = z[:, o:o + CA_W]

        cqn = _rms(cq, gq_ref[...]).astype(BF16)
        q = jnp.dot(cqn, wq_ref[...], preferred_element_type=F32)
        ckvn = _rms(ckv, gkv_ref[...]).astype(BF16)
        kn = jnp.dot(ckvn, wk_ref[...], preferred_element_type=F32)
        v_out[0, :, rows] = (_nt_dot(wv_ref[...], ckvn) + vone_ref[...]).astype(BF16)

        q_sw = _swap_halves(q)
        cq_t, sq_t = cq_ref[rows], sq_ref[rows]
        krope = kr * ck_ref[rows] + _swap_halves(kr) * sk_ref[rows]
        for hd in range(MLA_HEADS):
            sl = slice(hd * HEAD_PAD, (hd + 1) * HEAD_PAD)
            q_out[rows, sl] = (q[:, sl] * cq_t + q_sw[:, sl] * sq_t).astype(BF16)
            k_out[rows, sl] = (kn[:, sl] + krope).astype(BF16)

        qc_out[rows] = (qc * (CA_DIM ** -0.5 * LOG2E)).astype(BF16)
        kc_out[rows] = kc.astype(BF16)
        vc_out[0, :, rows] = vc.T.astype(BF16)


def _proj(x2, sc1, sh1, g_pre, w, tabs, seq):
    t, d = x2.shape
    tpb = seq // PTM
    row = lambda i: (i, 0)
    full = lambda i: (0, 0)
    per_b = lambda i: (i // tpb, 0, 0)
    pos = lambda i: (i % tpb, 0)
    hw = MLA_HEADS * HEAD_PAD
    in_specs = [
        pl.BlockSpec((PTM, d), row),
        pl.BlockSpec((1, 1, d), per_b), pl.BlockSpec((1, 1, d), per_b),
        pl.BlockSpec((1, d), full),
        pl.BlockSpec((d, W_IN_COLS), full),
        pl.BlockSpec((1, Q_LORA), full),
        pl.BlockSpec((Q_LORA, hw), full),
        pl.BlockSpec((1, KV_LORA), full),
        pl.BlockSpec((KV_LORA, hw), full), pl.BlockSpec((hw, KV_LORA), full), pl.BlockSpec((hw, 1), full),
        pl.BlockSpec((PTM, LANES), pos), pl.BlockSpec((PTM, LANES), pos),
        pl.BlockSpec((PTM, LANES), pos), pl.BlockSpec((PTM, LANES), pos),
    ]
    outs = ((hw, False), (hw, False), (hw, True), (CA_W, False), (CA_W, False), (CA_W, True))
    t_spec = lambda n: pl.BlockSpec((1, n, PTM), lambda i: (i // tpb, 0, i % tpb))
    return pl.pallas_call(
        _proj_kernel,
        grid=(t // PTM,),
        in_specs=in_specs,
        out_specs=[t_spec(n) if tr else pl.BlockSpec((PTM, n), row) for n, tr in outs],
        out_shape=[jax.ShapeDtypeStruct((t // seq, n, seq) if tr else (t, n), BF16) for n, tr in outs],
        compiler_params=_cparams(("arbitrary",)),
        name="proj",
    )(x2, sc1, sh1, g_pre, w["w_in"], w["g_q"], w["wq"], w["g_kv"], w["wk"], w["wv"], w["v_one"],
      tabs["cq"], tabs["sq"], tabs["ck"], tabs["sk"])


def _mla_kernel(q_ref, k_ref, v_ref, o_ref):
    i = pl.program_id(2)
    heads = [slice(hh * HEAD_PAD, (hh + 1) * HEAD_PAD) for hh in range(MLA_HPS)]

    def step(off, width, carry, masked=False):
        def score(hs):
            s = _nt_dot(k_ref[0, pl.ds(off, width), hs], q_ref[0, :, hs])
            if masked:
                kc = lax.broadcasted_iota(jnp.int32, (width, TQ), 0) // CHUNK
                qc = lax.broadcasted_iota(jnp.int32, (width, TQ), 1) // CHUNK
                s = jnp.where(kc <= qc, s, NEG)
            return s

        scores = [score(hs) for hs in heads[:MLA_LEAD]]
        new = []
        for hh, hs in enumerate(heads):
            m, acc = carry[hh]
            m_new = jnp.maximum(m, jnp.max(scores[hh], axis=0, keepdims=True))
            p = jnp.exp2(scores[hh] - m_new).astype(BF16)
            if hh + MLA_LEAD < MLA_HPS:
                scores.append(score(heads[hh + MLA_LEAD]))
            pv = jnp.dot(v_ref[0, hs, pl.ds(off, width)], p, preferred_element_type=F32)
            new.append((m_new, jnp.exp2(m - m_new) * acc + pv))
        return tuple(new)

    init = tuple((jnp.full((1, TQ), NEG, F32), jnp.zeros((HEAD_PAD, TQ), F32)) for _ in heads)
    carry = lax.fori_loop(0, i // 2, lambda j, c: step(pl.multiple_of(j * TKW, TKW), TKW, c), init)
    carry = lax.fori_loop(0, i % 2, lambda _, c: step(pl.multiple_of((i - 1) * TQ, TQ), TQ, c), carry)
    carry = step(pl.multiple_of(i * TQ, TQ), TQ, carry, True)
    for pp in range(MLA_HPS // 2):
        pair = [acc[:MLA_V] / acc[MLA_V:MLA_V + 1] for _, acc in (carry[2 * pp], carry[2 * pp + 1])]
        o_ref[0, :, pp * LANES:(pp + 1) * LANES] = jnp.concatenate(pair, axis=0).T.astype(BF16)


def _mla(q, k, v):
    bsz, seq, _ = q.shape
    groups = MLA_HEADS // MLA_HPS
    return pl.pallas_call(
        _mla_kernel,
        grid=(bsz, groups, seq // TQ),
        in_specs=[pl.BlockSpec((1, TQ, MLA_HPS * HEAD_PAD), lambda b, p, i: (b, i, p)),
                  pl.BlockSpec((1, seq, MLA_HPS * HEAD_PAD), lambda b, p, i: (b, 0, p)),
                  pl.BlockSpec((1, MLA_HPS * HEAD_PAD, seq), lambda b, p, i: (b, p, 0))],
        out_specs=pl.BlockSpec((1, TQ, MLA_HPS * MLA_V), lambda b, p, i: (b, i, p)),
        out_shape=jax.ShapeDtypeStruct((bsz, seq, MLA_HEADS * MLA_V), BF16),
        compiler_params=_cparams(("arbitrary", "arbitrary", "arbitrary")),
        name="mla",
    )(q, k, v)


def _chunk_kernel(q_ref, k0_ref, k1_ref, k2_ref, v0_ref, v1_ref, v2_ref, bias_ref, o_ref):
    lane = lax.broadcasted_iota(jnp.int32, (CQ, LANES), 1)
    lo = lane < CA_DIM
    row = lax.broadcasted_iota(jnp.int32, (LANES, CQ), 0)
    top = row < CA_DIM
    k_refs = (k0_ref, k1_ref, k2_ref)
    v_refs = (v0_ref, v1_ref, v2_ref)

    def score(head):
        sl = slice((head // 2) * LANES, (head // 2 + 1) * LANES)
        q = q_ref[0, :, sl]
        qm = jnp.where(lo if head % 2 == 0 else jnp.logical_not(lo), q, jnp.zeros_like(q))
        return jnp.concatenate([_nt_dot(kr[0, :, sl], qm) for kr in k_refs], axis=0)

    scores = [score(h) for h in range(CA_HEADS)]
    for p in range(CA_HEADS // 2):
        sl = slice(p * LANES, (p + 1) * LANES)
        vts = [vr[0, sl, :] for vr in v_refs]
        outs = []
        for hh in range(2):
            mine = top if hh == 0 else jnp.logical_not(top)
            den_row = CA_DIM if hh == 0 else 0
            ones_row = (row == den_row).astype(BF16)
            s = scores[2 * p + hh] + bias_ref[0, 2 * p + hh]
            m = jnp.max(s, axis=0, keepdims=True)
            pb = jnp.exp2(s - m).astype(BF16)
            o = None
            for cb in range(3):
                part = jnp.dot(jnp.where(mine, vts[cb], ones_row), pb[cb * CQ:(cb + 1) * CQ],
                               preferred_element_type=F32)
                o = part if o is None else o + part
            o = o / o[den_row:den_row + 1]
            outs.append(o[:CA_DIM] if hh == 0 else o[CA_DIM:])
        o_ref[0, :, sl] = jnp.concatenate(outs, axis=0).T.astype(BF16)


def _chunk_attn(qc, kc, vct, bias):
    bsz, seq, w = qc.shape
    blk = lambda off: pl.BlockSpec((1, CQ, w), lambda b, i: (b, jnp.maximum(i + off, 0), 0))
    blk_t = lambda off: pl.BlockSpec((1, w, CQ), lambda b, i: (b, 0, jnp.maximum(i + off, 0)))
    return pl.pallas_call(
        _chunk_kernel,
        grid=(bsz, seq // CQ),
        in_specs=[blk(0), blk(-2), blk(-1), blk(0), blk_t(-2), blk_t(-1), blk_t(0),
                  pl.BlockSpec((1, CA_HEADS, CBAND, CQ), lambda b, i: (jnp.minimum(i, 2), 0, 0, 0))],
        out_specs=pl.BlockSpec((1, CQ, w), lambda b, i: (b, i, 0)),
        out_shape=jax.ShapeDtypeStruct((bsz, seq, w), BF16),
        compiler_params=_cparams(("arbitrary", "arbitrary")),
        name="chunk_attn",
    )(qc, kc, kc, kc, vct, vct, vct, bias)


def _post_kernel(oa_ref, ob_ref, x_ref, gt_ref, sc_ref, sh_ref, gpost_ref, gpre_ref,
                 woa_ref, wob_ref, wr_ref, br_ref,
                 x1_out, h2_out, idx_out, gate_out, rank_out, cnt_out, carry_ref):
    t = pl.program_id(0)

    @pl.when(t == 0)
    def _():
        carry_ref[...] = jnp.zeros_like(carry_ref)

    os = []
    for rs in range(0, TM, POST_SUB):
        rows = slice(rs, rs + POST_SUB)
        o = jnp.dot(oa_ref[rows], woa_ref[...], preferred_element_type=F32)
        os.append(o + jnp.dot(ob_ref[rows], wob_ref[...], preferred_element_type=F32))
    h2s = []
    gain1 = gt_ref[0] * gpost_ref[...]
    gain2 = gpre_ref[...] * (1.0 + sc_ref[0])
    for sub, o in enumerate(os):
        rows = slice(sub * POST_SUB, (sub + 1) * POST_SUB)
        x1 = x_ref[rows] + _rms(o, gain1)
        x1_out[rows] = x1
        h2f = _rms(x1, gain2) + sh_ref[0]
        h2_out[rows] = _pack_rows(h2f)
        h2s.append(h2f.astype(BF16))
    h2 = jnp.concatenate(h2s, axis=0)

    logits = _nt_dot(wr_ref[...], h2) + br_ref[...]
    eid = lax.broadcasted_iota(jnp.int32, (N_EXPERTS, TM), 0)
    vals, idxs = [], []
    work = logits
    for _k in range(TOP_K):
        m = jnp.max(work, axis=0, keepdims=True)
        ix = jnp.min(jnp.where(work == m, eid, N_EXPERTS), axis=0, keepdims=True)
        work = jnp.where(eid == ix, -jnp.inf, work)
        vals.append(m)
        idxs.append(ix)
    es = [jnp.exp(v - vals[0]) for v in vals]
    den = es[0] + es[1] + es[2] + es[3]
    gate_out[...] = jnp.concatenate([e / den for e in es], axis=0)
    idx_out[...] = jnp.concatenate(idxs, axis=0)

    sel = (eid == idxs[0]) | (eid == idxs[1]) | (eid == idxs[2]) | (eid == idxs[3])
    self32 = sel.astype(F32)
    rr = lax.broadcasted_iota(jnp.int32, (TM, TM), 0)
    cc = lax.broadcasted_iota(jnp.int32, (TM, TM), 1)
    upper = (rr < cc).astype(BF16)
    before = jnp.dot(self32.astype(BF16), upper, preferred_element_type=F32)
    before = before + carry_ref[:, 0:1]
    ranks = [jnp.sum(jnp.where(eid == ix, before, 0.0), axis=0, keepdims=True) for ix in idxs]
    rank_out[...] = jnp.concatenate(ranks, axis=0).astype(jnp.int32)
    carry_ref[...] = carry_ref[...] + jnp.sum(self32, axis=1, keepdims=True)
    cnt_out[...] = carry_ref[...]


def _post(oa, ob, x2, gt1, sc2, sh2, g_post, g_pre, w, seq, tile_off, t):
    d = x2.shape[1]
    tpb = seq // TM
    row = lambda i: (i, 0)
    src = lambda i: (i + tile_off, 0)
    col = lambda i: (0, i)
    full = lambda i: (0, 0)
    per_b = lambda i: ((i + tile_off) // tpb, 0, 0)
    hw = oa.shape[1]
    in_specs = [
        pl.BlockSpec((TM, hw), src), pl.BlockSpec((TM, hw), src), pl.BlockSpec((TM, d), src),
        pl.BlockSpec((1, 1, d), per_b), pl.BlockSpec((1, 1, d), per_b), pl.BlockSpec((1, 1, d), per_b),
        pl.BlockSpec((1, d), full), pl.BlockSpec((1, d), full),
        pl.BlockSpec((hw, d), full), pl.BlockSpec((hw, d), full),
        pl.BlockSpec((N_EXPERTS, d), full), pl.BlockSpec((N_EXPERTS, 1), full),
    ]
    out_specs = [
        pl.BlockSpec((TM, d), row), pl.BlockSpec((TM, d // 2), row),
        pl.BlockSpec((TOP_K, TM), col), pl.BlockSpec((TOP_K, TM), col), pl.BlockSpec((TOP_K, TM), col),
        pl.BlockSpec((N_EXPERTS, LANES), full),
    ]
    out_shape = [
        jax.ShapeDtypeStruct((t, d), F32), jax.ShapeDtypeStruct((t, d // 2), U32),
        jax.ShapeDtypeStruct((TOP_K, t), jnp.int32), jax.ShapeDtypeStruct((TOP_K, t), F32),
        jax.ShapeDtypeStruct((TOP_K, t), jnp.int32),
        jax.ShapeDtypeStruct((N_EXPERTS, LANES), F32),
    ]
    return pl.pallas_call(
        _post_kernel,
        grid=(t // TM,),
        in_specs=in_specs,
        out_specs=out_specs,
        out_shape=out_shape,
        scratch_shapes=[pltpu.VMEM((N_EXPERTS, LANES), F32)],
        compiler_params=_cparams(("arbitrary",)),
        name="post",
    )(oa, ob, x2, gt1, sc2, sh2, g_post, g_pre, w["wo_a"], w["wo_b"], w["wr_t"], w["b_r"])


def _expert_kernel(be_ref, nu_ref, x_ref, wgu_ref, bgu_ref, wd_ref, bd_ref, y_ref, wgu_bf, wd_bf):
    j = pl.program_id(0)
    used = j < nu_ref[0]

    @pl.when(used & ((j == 0) | (be_ref[j] != be_ref[jnp.maximum(j - 1, 0)])))
    def _():
        wgu_bf[...] = wgu_ref[0].astype(BF16)
        wd_bf[...] = wd_ref[0].astype(BF16)

    @pl.when(used)
    def _():
        gus = []
        for rs in range(0, ROWS, EXPERT_SUB):
            x_lo, x_hi = _unpack_rows(x_ref[rs:rs + EXPERT_SUB])
            half = x_lo.shape[1]
            gu = jnp.dot(x_lo.astype(BF16), wgu_bf[:half], preferred_element_type=F32)
            gu += jnp.dot(x_hi.astype(BF16), wgu_bf[half:], preferred_element_type=F32)
            gus.append(gu + bgu_ref[0])
        for sub, gu in enumerate(gus):
            rows = slice(sub * EXPERT_SUB, (sub + 1) * EXPERT_SUB)
            gate = jnp.minimum(gu[:, :D_EXPERT], SWIGLU_LIMIT)
            up = jnp.clip(gu[:, D_EXPERT:], -SWIGLU_LIMIT, SWIGLU_LIMIT)
            glu = gate / (1.0 + jnp.exp(-SWIGLU_ALPHA * gate))
            act = ((up + 1.0) * glu).astype(BF16)
            y = jnp.dot(act, wd_bf[...], preferred_element_type=F32) + bd_ref[0]
            y_ref[rows] = _pack_rows(y)

    @pl.when(j >= nu_ref[0])
    def _():
        y_ref[...] = jnp.zeros_like(y_ref)


def _experts(xin, block_exp, n_used, wgu, bgu, wd, bd):
    p_rows, dw = xin.shape
    d = 2 * dw
    nb = p_rows // ROWS
    f2 = wgu.shape[2]
    grid_spec = pltpu.PrefetchScalarGridSpec(
        num_scalar_prefetch=2,
        grid=(nb,),
        in_specs=[
            pl.BlockSpec((ROWS, dw), lambda j, be, nu: (jnp.minimum(j, nu[0] - 1), 0)),
            pl.BlockSpec((1, d, f2), lambda j, be, nu: (be[j], 0, 0)),
            pl.BlockSpec((1, 1, f2), lambda j, be, nu: (be[j], 0, 0)),
            pl.BlockSpec((1, f2 // 2, d), lambda j, be, nu: (be[j], 0, 0)),
            pl.BlockSpec((1, 1, d), lambda j, be, nu: (be[j], 0, 0)),
        ],
        out_specs=pl.BlockSpec((ROWS, dw), lambda j, be, nu: (j, 0)),
        scratch_shapes=[pltpu.VMEM((d, f2), BF16), pltpu.VMEM((f2 // 2, d), BF16)],
    )
    return pl.pallas_call(
        _expert_kernel,
        grid_spec=grid_spec,
        out_shape=jax.ShapeDtypeStruct((p_rows, dw), U32),
        compiler_params=_cparams(("arbitrary",)),
        name="experts",
    )(block_exp, n_used, xin, wgu, bgu, wd, bd)


def _final_kernel(yg_ref, g_ref, x1_ref, gt_ref, gpost_ref, *rest):
    o_ref = rest[-1]
    g = g_ref[...]
    f_lo, f_hi = None, None
    for k in range(TOP_K):
        lo, hi = _unpack_rows(yg_ref[k])
        gk = g[:, k:k + 1]
        f_lo = lo * gk if f_lo is None else f_lo + lo * gk
        f_hi = hi * gk if f_hi is None else f_hi + hi * gk
    f = jnp.concatenate([f_lo, f_hi], axis=1)
    o_ref[...] = x1_ref[...] + gt_ref[0] * _rms(f, gpost_ref[...])


def _final(yg, gates_t, x1, gt2, g_post, seq, tile_off, t_all, prev_out):
    t, d = x1.shape
    tpb = seq // TM
    row = lambda i: (i, 0)
    in_specs = [pl.BlockSpec((TOP_K, TM, d // 2), lambda i: (0, i, 0)),
                pl.BlockSpec((TM, TOP_K), row),
                pl.BlockSpec((TM, d), row),
                pl.BlockSpec((1, 1, d), lambda i: ((i + tile_off) // tpb, 0, 0)),
                pl.BlockSpec((1, d), lambda i: (0, 0))]
    args = [yg, gates_t, x1, gt2, g_post]
    aliases = {}
    if prev_out is not None:
        in_specs.append(pl.BlockSpec(memory_space=pl.ANY))
        args.append(prev_out)
        aliases = {len(args) - 1: 0}
    return pl.pallas_call(
        _final_kernel,
        grid=(t // TM,),
        in_specs=in_specs,
        out_specs=pl.BlockSpec((TM, d), lambda i: (i + tile_off, 0)),
        out_shape=jax.ShapeDtypeStruct((t_all, d), F32),
        input_output_aliases=aliases,
        compiler_params=_cparams(("arbitrary",)),
        name="final",
    )(*args)


def _sc_mesh():
    return plsc.VectorSubcoreMesh(core_axis_name="c", subcore_axis_name="s")


def _sc_worker():
    return lax.axis_index("s") * SC_CORES + lax.axis_index("c")


def _dispatch(h2, dest, p_rows):
    t, dw = h2.shape
    per_w = t // SC_WORKERS
    n_win = per_w // SC_WIN

    @functools.partial(
        pl.kernel, mesh=_sc_mesh(),
        out_type=jax.ShapeDtypeStruct((p_rows, dw), h2.dtype),
        scratch_types=[pltpu.VMEM((TOP_K, SC_WIN), jnp.int32),
                       pltpu.VMEM((SC_WIN, dw), h2.dtype),
                       pltpu.SemaphoreType.DMA],
        name="dispatch",
    )
    def run(h_hbm, d_hbm, o_hbm, idx_v, rows_v, sem):
        wid = _sc_worker()

        @pl.loop(0, n_win)
        def _(wi):
            base = pl.multiple_of(wid * per_w + wi * SC_WIN, SC_WIN)
            for k in range(TOP_K):
                pltpu.sync_copy(d_hbm.at[pl.ds(k * t + base, SC_WIN)], idx_v.at[k])
            pltpu.sync_copy(h_hbm.at[pl.ds(base, SC_WIN)], rows_v)
            for k in range(TOP_K):
                pltpu.async_copy(rows_v, o_hbm.at[idx_v.at[k]], sem).wait()

    return run(h2, dest)


def _gather_rows(y, dest, t):
    _, dw = y.shape
    per_w = t // SC_WORKERS
    n_win = per_w // SC_WIN

    @functools.partial(
        pl.kernel, mesh=_sc_mesh(),
        out_type=jax.ShapeDtypeStruct((TOP_K, t, dw), y.dtype),
        scratch_types=[pltpu.VMEM((SC_WIN,), jnp.int32),
                       pltpu.VMEM((SC_WIN, dw), y.dtype),
                       pltpu.SemaphoreType.DMA],
        name="gather_rows",
    )
    def run(y_hbm, d_hbm, o_hbm, idx_v, rows_v, sem):
        wid = _sc_worker()

        @pl.loop(0, n_win)
        def _(wi):
            base = pl.multiple_of(wid * per_w + wi * SC_WIN, SC_WIN)
            for k in range(TOP_K):
                pltpu.sync_copy(d_hbm.at[pl.ds(k * t + base, SC_WIN)], idx_v)
                pltpu.async_copy(y_hbm.at[idx_v], rows_v, sem).wait()
                pltpu.sync_copy(rows_v, o_hbm.at[k, pl.ds(base, SC_WIN)])

    return run(y, dest)


def _prep_weights(w_in, g_q, w_qb, g_kv, w_kvb, w_o, w_router, b_router):
    d = w_in.shape[0]
    o = 0
    w_cq = w_in[:, o:o + Q_LORA]; o += Q_LORA
    w_ckv = w_in[:, o:o + KV_LORA]; o += KV_LORA
    w_kr = w_in[:, o:o + MLA_ROPE]; o += MLA_ROPE
    w_ca = w_in[:, o:]
    half = MLA_ROPE // 2
    zpad = lambda n: jnp.zeros((d, n), w_in.dtype)
    tail = HEAD_PAD - MLA_NOPE - MLA_ROPE
    w_kr_pad = jnp.concatenate([zpad(MLA_NOPE), w_kr, zpad(tail)], axis=1)
    w_in_all = jnp.concatenate([w_cq, w_ckv, w_kr_pad, w_ca], axis=1).astype(BF16)

    wq = w_qb.reshape(Q_LORA, MLA_HEADS, MLA_NOPE + MLA_ROPE)
    zq = jnp.zeros((Q_LORA, MLA_HEADS, tail), w_qb.dtype)
    wq_pad = jnp.concatenate([wq, zq], axis=-1).reshape(Q_LORA, -1).astype(BF16)

    wkv = w_kvb.reshape(KV_LORA, MLA_HEADS, MLA_NOPE + MLA_V)
    kn = wkv[..., :MLA_NOPE]
    wk = jnp.concatenate([kn, jnp.zeros((KV_LORA, MLA_HEADS, HEAD_PAD - MLA_NOPE), w_kvb.dtype)],
                         axis=-1).reshape(KV_LORA, -1).astype(BF16)
    wv = jnp.concatenate([wkv[..., MLA_NOPE:], jnp.zeros((KV_LORA, MLA_HEADS, HEAD_PAD - MLA_V), w_kvb.dtype)],
                         axis=-1).reshape(KV_LORA, -1).T.astype(BF16)
    v_one = jnp.tile((jnp.arange(HEAD_PAD) == MLA_V).astype(F32), MLA_HEADS).reshape(-1, 1)
    mla_w = MLA_HEADS * MLA_V
    return {
        "w_in": w_in_all, "g_q": g_q.reshape(1, -1), "wq": wq_pad,
        "g_kv": g_kv.reshape(1, -1), "wk": wk, "wv": wv, "v_one": v_one,
        "wo_a": w_o[:mla_w].astype(BF16), "wo_b": w_o[mla_w:].astype(BF16),
        "wr_t": w_router.T.astype(BF16), "b_r": b_router.reshape(-1, 1),
    }


def _rope_tables(seq):
    half = MLA_ROPE // 2
    inv_freq = ROPE_THETA ** (-jnp.arange(half, dtype=F32) / half)
    ang = jnp.arange(seq, dtype=F32)[:, None] * inv_freq[None, :]
    cos, sin = jnp.cos(ang), jnp.sin(ang)
    tail = HEAD_PAD - MLA_NOPE - MLA_ROPE
    ones = jnp.ones((seq, MLA_NOPE), F32)
    zn = jnp.zeros((seq, MLA_NOPE), F32)
    zt = jnp.zeros((seq, tail), F32)
    qs = (MLA_NOPE + MLA_ROPE) ** -0.5 * LOG2E
    return {
        "cq": jnp.concatenate([ones, cos, cos, zt], axis=1) * qs,
        "sq": jnp.concatenate([zn, -sin, sin, zt], axis=1) * qs,
        "ck": jnp.concatenate([zn, cos, cos, zt], axis=1),
        "sk": jnp.concatenate([zn, -sin, sin, zt], axis=1),
    }


def _bias_table(rel_bias):
    n = CQ + CBAND - 1
    rel = (CBAND - 1) - jnp.arange(n)
    diag = rel_bias.astype(F32)[:, jnp.clip(rel, -(CHUNK - 1), REL_MAX) + (CHUNK - 1)] * LOG2E
    diag = jnp.concatenate([diag, jnp.zeros((diag.shape[0], 1), F32)], axis=1)
    b = jnp.tile(diag, (1, CQ))[:, :CQ * n].reshape(-1, CQ, n)[:, :, CQ - 1:]
    r = jnp.arange(CQ)[:, None]
    c = jnp.arange(CBAND)[None, :]
    dchunk = r // CHUNK - (c // CHUNK - CA_LEFT)
    visible = (dchunk >= 0) & (dchunk <= CA_LEFT)
    exists = (c // CQ)[None] >= (2 - jnp.arange(3))[:, None, None]
    return jnp.where((visible[None] & exists)[:, None], b[None], NEG)


def _layer(x, c, w_ada, b_ada, g_pre_mix, g_post_mix, g_pre_ffn, g_post_ffn, w_in, g_q, w_qb,
           g_kv, w_kvb, rel_bias, w_o, w_router, b_router, w_gu, b_gu, w_down, b_down):
    bsz, seq, d = x.shape
    t = bsz * seq
    mod = _ada(c, w_ada, b_ada).reshape(bsz, 6, 1, d)
    sh1, sc1, gt1, sh2, sc2, gt2 = [mod[:, k] for k in range(6)]
    w = _prep_weights(w_in, g_q, w_qb, g_kv, w_kvb, w_o, w_router, b_router)
    tabs = _rope_tables(seq)
    x2 = x.reshape(t, d)

    q, k, v, qc, kc, vc = _proj(x2, sc1, sh1, g_pre_mix.reshape(1, d), w, tabs, seq)
    shp = lambda a: a.reshape(bsz, seq, a.shape[-1])
    oa = _mla(shp(q), shp(k), v).reshape(t, -1)
    ob = _chunk_attn(shp(qc), shp(kc), vc, jnp.swapaxes(_bias_table(rel_bias), 2, 3)).reshape(t, -1)

    tp = t // MOE_PARTS
    eids = jnp.arange(N_EXPERTS, dtype=jnp.int32)[:, None, None]
    p_rows = tp * TOP_K + N_EXPERTS * ROWS
    block_start = jnp.arange(p_rows // ROWS, dtype=jnp.int32) * ROWS
    routed = []
    for part in range(MOE_PARTS):
        x1, h2, top_idx, gates, rank, cnt = _post(oa, ob, x2, gt1, sc2, sh2, g_post_mix.reshape(1, d),
                                                  g_pre_ffn.reshape(1, d), w, seq, part * (tp // TM), tp)
        counts = cnt[:, 0].astype(jnp.int32)
        padded = ((counts + ROWS - 1) // ROWS) * ROWS
        pend = jnp.cumsum(padded)
        pstart = pend - padded
        dest = (jnp.sum(jnp.where(top_idx[None] == eids, pstart[:, None, None], 0), axis=0) + rank).reshape(-1)
        block_exp = jnp.minimum(jnp.sum(pend[None, :] <= block_start[:, None], axis=1),
                                N_EXPERTS - 1).astype(jnp.int32)
        n_used = (pend[-1:] // ROWS).astype(jnp.int32)
        routed.append((x1, gates.T, dest, block_exp, n_used, _dispatch(h2, dest, p_rows)))

    out = None
    for part, (x1, gates_t, dest, block_exp, n_used, xin) in enumerate(routed):
        y = _experts(xin, block_exp, n_used, w_gu, b_gu.reshape(N_EXPERTS, 1, -1),
                     w_down, b_down.reshape(N_EXPERTS, 1, -1))
        yg = _gather_rows(y, dest, tp)
        out = _final(yg, gates_t, x1, gt2, g_post_ffn.reshape(1, d), seq, part * (tp // TM), t, out)
    return out.reshape(bsz, seq, d)


def kernel(x, c, w_ada, b_ada, g_pre_mix, g_post_mix, g_pre_ffn, g_post_ffn, w_in, g_q, w_qb,
           g_kv, w_kvb, rel_bias, w_o, w_router, b_router, w_gu, b_gu, w_down, b_down):
    for l in range(w_ada.shape[0]):
        x = _layer(x, c, w_ada[l], b_ada[l], g_pre_mix[l], g_post_mix[l], g_pre_ffn[l], g_post_ffn[l],
                   w_in[l], g_q[l], w_qb[l], g_kv[l], w_kvb[l], rel_bias[l], w_o[l], w_router[l],
                   b_router[l], w_gu[l], b_gu[l], w_down[l], b_down[l])
    return x
```

```python
import functools
import math

import jax
import jax.numpy as jnp
from jax import lax
from jax.experimental import pallas as pl
from jax.experimental.pallas import tpu as pltpu
from jax.experimental.pallas import tpu_sc as plsc

F32 = jnp.float32
BF16 = jnp.bfloat16
U32 = jnp.uint32

D_MODEL = 1024
CHUNK = 64
EPS = 1e-6
MLA_HEADS = 8
MLA_NOPE = 64
MLA_ROPE = 32
MLA_V = 64
Q_LORA = 256
KV_LORA = 128
ROPE_THETA = 10000.0
CA_HEADS = 8
CA_DIM = 64
CA_LEFT = 8
REL_MAX = 256
N_EXPERTS = 32
TOP_K = 4
D_EXPERT = 1024
SWIGLU_LIMIT = 7.0
SWIGLU_ALPHA = 1.702

LANES = 128
HEAD_PAD = 128
LOG2E = math.log2(math.e)
NEG = -1e30
VMEM_LIMIT = 56 * 1024 * 1024

TM = 1024
POST_SUB = 128
PTM = 1024
PROJ_SUB = 1024
TQ = 512
TKW = 2 * TQ
MLA_HPS = 4
MLA_LEAD = 2
CQ = 256
CBAND = 3 * CQ
ROWS = 512
EXPERT_SUB = 256
MOE_PARTS = 2
SC_CORES = 2
SC_WORKERS = SC_CORES * 16
SC_WIN = 128
CA_W = CA_HEADS * CA_DIM
W_IN_COLS = Q_LORA + KV_LORA + LANES + 3 * CA_W


def _cparams(sem, flags=None):
    return pltpu.CompilerParams(dimension_semantics=sem, vmem_limit_bytes=VMEM_LIMIT, flags=flags)


def _nt_dot(a, b):
    return lax.dot_general(a, b, (((1,), (1,)), ((), ())), preferred_element_type=F32)


def _rms(x, g):
    return x * lax.rsqrt(jnp.mean(x * x, axis=-1, keepdims=True) + EPS) * g


def _pack_rows(x):
    n = x.shape[1] // 2
    lo = lax.bitcast_convert_type(x[:, :n].astype(BF16).astype(F32), U32)
    hi = lax.bitcast_convert_type(x[:, n:].astype(BF16).astype(F32), U32)
    return (lo >> 16) | hi


def _unpack_rows(p):
    lo = lax.bitcast_convert_type(p << 16, F32)
    hi = lax.bitcast_convert_type(p & jnp.uint32(0xFFFF0000), F32)
    return lo, hi


def _ada_kernel(c_ref, w_ref, b_ref, o_ref):
    c = c_ref[...]
    a = (c / (1.0 + jnp.exp(-c))).astype(BF16)
    o_ref[...] = jnp.dot(a, w_ref[...].astype(BF16), preferred_element_type=F32) + b_ref[...]


def _ada(c, w, b):
    bsz, d = c.shape
    n = w.shape[1]
    tn = 1024
    return pl.pallas_call(
        _ada_kernel,
        grid=(n // tn,),
        in_specs=[pl.BlockSpec((bsz, d), lambda j: (0, 0)),
                  pl.BlockSpec((d, tn), lambda j: (0, j)),
                  pl.BlockSpec((1, tn), lambda j: (0, j))],
        out_specs=pl.BlockSpec((bsz, tn), lambda j: (0, j)),
        out_shape=jax.ShapeDtypeStruct((bsz, n), F32),
        compiler_params=_cparams(("arbitrary",)),
        name="ada",
    )(c, w, b.reshape(1, n))


def _swap_halves(x):
    width = x.shape[1]
    half = MLA_ROPE // 2
    lane = lax.broadcasted_iota(jnp.int32, x.shape, 1) % HEAD_PAD
    return jnp.where(lane < MLA_NOPE + half, pltpu.roll(x, width - half, axis=1), pltpu.roll(x, half, axis=1))


def _proj_kernel(x_ref, sc_ref, sh_ref, g_ref, win_ref, gq_ref, wq_ref, gkv_ref,
                 wk_ref, wv_ref, vone_ref, cq_ref, sq_ref, ck_ref, sk_ref,
                 q_out, k_out, v_out, qc_out, kc_out, vc_out):
    zs = []
    for rs in range(0, PTM, PROJ_SUB):
        h = _rms(x_ref[rs:rs + PROJ_SUB], g_ref[...]) * (1.0 + sc_ref[0]) + sh_ref[0]
        zs.append(jnp.dot(h.astype(BF16), win_ref[...], preferred_element_type=F32))
    for sub, z in enumerate(zs):
        rows = slice(sub * PROJ_SUB, (sub + 1) * PROJ_SUB)
        o = 0
        cq = z[:, o:o + Q_LORA]; o += Q_LORA
        ckv = z[:, o:o + KV_LORA]; o += KV_LORA
        kr = z[:, o:o + LANES]; o += LANES
        qc = z[:, o:o + CA_W]; o += CA_W
        kc = z[:, o:o + CA_W]; o += CA_W
        vc = z[:, o:o + CA_W]

        cqn = _rms(cq, gq_ref[...]).astype(BF16)
        q = jnp.dot(cqn, wq_ref[...], preferred_element_type=F32)
        ckvn = _rms(ckv, gkv_ref[...]).astype(BF16)
        kn = jnp.dot(ckvn, wk_ref[...], preferred_element_type=F32)
        v_out[0, :, rows] = (_nt_dot(wv_ref[...], ckvn) + vone_ref[...]).astype(BF16)

        q_sw = _swap_halves(q)
        cq_t, sq_t = cq_ref[rows], sq_ref[rows]
        krope = kr * ck_ref[rows] + _swap_halves(kr) * sk_ref[rows]
        for hd in range(MLA_HEADS):
            sl = slice(hd * HEAD_PAD, (hd + 1) * HEAD_PAD)
            q_out[rows, sl] = (q[:, sl] * cq_t + q_sw[:, sl] * sq_t).astype(BF16)
            k_out[rows, sl] = (kn[:, sl] + krope).astype(BF16)

        qc_out[rows] = (qc * (CA_DIM ** -0.5 * LOG2E)).astype(BF16)
        kc_out[rows] = kc.astype(BF16)
        vc_out[0, :, rows] = vc.T.astype(BF16)


def _proj(x2, sc1, sh1, g_pre, w, tabs, seq):
    t, d = x2.shape
    tpb = seq // PTM
    row = lambda i: (i, 0)
    full = lambda i: (0, 0)
    per_b = lambda i: (i // tpb, 0, 0)
    pos = lambda i: (i % tpb, 0)
    hw = MLA_HEADS * HEAD_PAD
    in_specs = [
        pl.BlockSpec((PTM, d), row),
        pl.BlockSpec((1, 1, d), per_b), pl.BlockSpec((1, 1, d), per_b),
        pl.BlockSpec((1, d), full),
        pl.BlockSpec((d, W_IN_COLS), full),
        pl.BlockSpec((1, Q_LORA), full),
        pl.BlockSpec((Q_LORA, hw), full),
        pl.BlockSpec((1, KV_LORA), full),
        pl.BlockSpec((KV_LORA, hw), full), pl.BlockSpec((hw, KV_LORA), full), pl.BlockSpec((hw, 1), full),
        pl.BlockSpec((PTM, LANES), pos), pl.BlockSpec((PTM, LANES), pos),
        pl.BlockSpec((PTM, LANES), pos), pl.BlockSpec((PTM, LANES), pos),
    ]
    outs = ((hw, False), (hw, False), (hw, True), (CA_W, False), (CA_W, False), (CA_W, True))
    t_spec = lambda n: pl.BlockSpec((1, n, PTM), lambda i: (i // tpb, 0, i % tpb))
    return pl.pallas_call(
        _proj_kernel,
        grid=(t // PTM,),
        in_specs=in_specs,
        out_specs=[t_spec(n) if tr else pl.BlockSpec((PTM, n), row) for n, tr in outs],
        out_shape=[jax.ShapeDtypeStruct((t // seq, n, seq) if tr else (t, n), BF16) for n, tr in outs],
        compiler_params=_cparams(("arbitrary",)),
        name="proj",
    )(x2, sc1, sh1, g_pre, w["w_in"], w["g_q"], w["wq"], w["g_kv"], w["wk"], w["wv"], w["v_one"],
      tabs["cq"], tabs["sq"], tabs["ck"], tabs["sk"])


def _mla_kernel(q_ref, k_ref, v_ref, o_ref):
    i = pl.program_id(2)
    heads = [slice(hh * HEAD_PAD, (hh + 1) * HEAD_PAD) for hh in range(MLA_HPS)]

    def step(off, width, carry, masked=False):
        if masked:
            kc = lax.broadcasted_iota(jnp.int32, (width, TQ), 0) // CHUNK
            qc = lax.broadcasted_iota(jnp.int32, (width, TQ), 1) // CHUNK
            visible = kc <= qc

        def score(hs):
            s = _nt_dot(k_ref[0, pl.ds(off, width), hs], q_ref[0, :, hs])
            return jnp.where(visible, s, NEG) if masked else s

        scores = [score(hs) for hs in heads[:MLA_LEAD]]
        new = []
        for hh, hs in enumerate(heads):
            m, acc = carry[hh]
            m_new = jnp.maximum(m, jnp.max(scores[hh], axis=0, keepdims=True))
            p = jnp.exp2(scores[hh] - m_new).astype(BF16)
            if hh + MLA_LEAD < MLA_HPS:
                scores.append(score(heads[hh + MLA_LEAD]))
            pv = jnp.dot(v_ref[0, hs, pl.ds(off, width)], p, preferred_element_type=F32)
            new.append((m_new, jnp.exp2(m - m_new) * acc + pv))
        return tuple(new)

    init = tuple((jnp.full((1, TQ), NEG, F32), jnp.zeros((HEAD_PAD, TQ), F32)) for _ in heads)
    carry = lax.fori_loop(0, i // 2, lambda j, c: step(pl.multiple_of(j * TKW, TKW), TKW, c), init)
    carry = lax.fori_loop(0, i % 2, lambda _, c: step(pl.multiple_of((i - 1) * TQ, TQ), TQ, c), carry)
    carry = step(pl.multiple_of(i * TQ, TQ), TQ, carry, True)
    for pp in range(MLA_HPS // 2):
        pair = [acc[:MLA_V] / acc[MLA_V:MLA_V + 1] for _, acc in (carry[2 * pp], carry[2 * pp + 1])]
        o_ref[0, :, pp * LANES:(pp + 1) * LANES] = jnp.concatenate(pair, axis=0).T.astype(BF16)


def _mla(q, k, v):
    bsz, seq, _ = q.shape
    groups = MLA_HEADS // MLA_HPS
    return pl.pallas_call(
        _mla_kernel,
        grid=(bsz, groups, seq // TQ),
        in_specs=[pl.BlockSpec((1, TQ, MLA_HPS * HEAD_PAD), lambda b, p, i: (b, i, p)),
                  pl.BlockSpec((1, seq, MLA_HPS * HEAD_PAD), lambda b, p, i: (b, 0, p)),
                  pl.BlockSpec((1, MLA_HPS * HEAD_PAD, seq), lambda b, p, i: (b, p, 0))],
        out_specs=pl.BlockSpec((1, TQ, MLA_HPS * MLA_V), lambda b, p, i: (b, i, p)),
        out_shape=jax.ShapeDtypeStruct((bsz, seq, MLA_HEADS * MLA_V), BF16),
        compiler_params=_cparams(("arbitrary", "arbitrary", "arbitrary")),
        name="mla",
    )(q, k, v)


def _chunk_kernel(q_ref, k0_ref, k1_ref, k2_ref, v0_ref, v1_ref, v2_ref, bias_ref, o_ref):
    lane = lax.broadcasted_iota(jnp.int32, (CQ, LANES), 1)
    lo = lane < CA_DIM
    row = lax.broadcasted_iota(jnp.int32, (LANES, CQ), 0)
    top = row < CA_DIM
    k_refs = (k0_ref, k1_ref, k2_ref)
    v_refs = (v0_ref, v1_ref, v2_ref)

    def score(head):
        sl = slice((head // 2) * LANES, (head // 2 + 1) * LANES)
        q = q_ref[0, :, sl]
        qm = jnp.where(lo if head % 2 == 0 else jnp.logical_not(lo), q, jnp.zeros_like(q))
        return jnp.concatenate([_nt_dot(kr[0, :, sl], qm) for kr in k_refs], axis=0)

    scores = [score(h) for h in range(CA_HEADS)]
    for p in range(CA_HEADS // 2):
        sl = slice(p * LANES, (p + 1) * LANES)
        vts = [vr[0, sl, :] for vr in v_refs]
        outs = []
        for hh in range(2):
            mine = top if hh == 0 else jnp.logical_not(top)
            den_row = CA_DIM if hh == 0 else 0
            ones_row = (row == den_row).astype(BF16)
            s = scores[2 * p + hh] + bias_ref[0, 2 * p + hh]
            m = jnp.max(s, axis=0, keepdims=True)
            pb = jnp.exp2(s - m).astype(BF16)
            o = None
            for cb in range(3):
                part = jnp.dot(jnp.where(mine, vts[cb], ones_row), pb[cb * CQ:(cb + 1) * CQ],
                               preferred_element_type=F32)
                o = part if o is None else o + part
            o = o / o[den_row:den_row + 1]
            outs.append(o[:CA_DIM] if hh == 0 else o[CA_DIM:])
        o_ref[0, :, sl] = jnp.concatenate(outs, axis=0).T.astype(BF16)


def _chunk_attn(qc, kc, vct, bias):
    bsz, seq, w = qc.shape
    blk = lambda off: pl.BlockSpec((1, CQ, w), lambda b, i: (b, jnp.maximum(i + off, 0), 0))
    blk_t = lambda off: pl.BlockSpec((1, w, CQ), lambda b, i: (b, 0, jnp.maximum(i + off, 0)))
    return pl.pallas_call(
        _chunk_kernel,
        grid=(bsz, seq // CQ),
        in_specs=[blk(0), blk(-2), blk(-1), blk(0), blk_t(-2), blk_t(-1), blk_t(0),
                  pl.BlockSpec((1, CA_HEADS, CBAND, CQ), lambda b, i: (jnp.minimum(i, 2), 0, 0, 0))],
        out_specs=pl.BlockSpec((1, CQ, w), lambda b, i: (b, i, 0)),
        out_shape=jax.ShapeDtypeStruct((bsz, seq, w), BF16),
        compiler_params=_cparams(("arbitrary", "arbitrary")),
        name="chunk_attn",
    )(qc, kc, kc, kc, vct, vct, vct, bias)


def _post_kernel(oa_ref, ob_ref, x_ref, gt_ref, sc_ref, sh_ref, gpost_ref, gpre_ref,
                 woa_ref, wob_ref, wr_ref, br_ref,
                 x1_out, h2_out, idx_out, gate_out, rank_out, cnt_out, carry_ref):
    t = pl.program_id(0)

    @pl.when(t == 0)
    def _():
        carry_ref[...] = jnp.zeros_like(carry_ref)

    os = []
    for rs in range(0, TM, POST_SUB):
        rows = slice(rs, rs + POST_SUB)
        o = jnp.dot(oa_ref[rows], woa_ref[...], preferred_element_type=F32)
        os.append(o + jnp.dot(ob_ref[rows], wob_ref[...], preferred_element_type=F32))
    h2s = []
    gain1 = gt_ref[0] * gpost_ref[...]
    gain2 = gpre_ref[...] * (1.0 + sc_ref[0])
    for sub, o in enumerate(os):
        rows = slice(sub * POST_SUB, (sub + 1) * POST_SUB)
        x1 = x_ref[rows] + _rms(o, gain1)
        x1_out[rows] = x1
        h2f = _rms(x1, gain2) + sh_ref[0]
        h2_out[rows] = _pack_rows(h2f)
        h2s.append(h2f.astype(BF16))
    h2 = jnp.concatenate(h2s, axis=0)

    logits = _nt_dot(wr_ref[...], h2) + br_ref[...]
    eid = lax.broadcasted_iota(jnp.int32, (N_EXPERTS, TM), 0)
    vals, idxs = [], []
    work = logits
    for _k in range(TOP_K):
        m = jnp.max(work, axis=0, keepdims=True)
        ix = jnp.min(jnp.where(work == m, eid, N_EXPERTS), axis=0, keepdims=True)
        work = jnp.where(eid == ix, -jnp.inf, work)
        vals.append(m)
        idxs.append(ix)
    es = [jnp.exp(v - vals[0]) for v in vals]
    den = es[0] + es[1] + es[2] + es[3]
    gate_out[...] = jnp.concatenate([e / den for e in es], axis=0)
    idx_out[...] = jnp.concatenate(idxs, axis=0)

    sel = (eid == idxs[0]) | (eid == idxs[1]) | (eid == idxs[2]) | (eid == idxs[3])
    self32 = sel.astype(F32)
    rr = lax.broadcasted_iota(jnp.int32, (TM, TM), 0)
    cc = lax.broadcasted_iota(jnp.int32, (TM, TM), 1)
    upper = (rr < cc).astype(BF16)
    before = jnp.dot(self32.astype(BF16), upper, preferred_element_type=F32)
    before = before + carry_ref[:, 0:1]
    ranks = [jnp.sum(jnp.where(eid == ix, before, 0.0), axis=0, keepdims=True) for ix in idxs]
    rank_out[...] = jnp.concatenate(ranks, axis=0).astype(jnp.int32)
    carry_ref[...] = carry_ref[...] + jnp.sum(self32, axis=1, keepdims=True)
    cnt_out[...] = carry_ref[...]


def _post(oa, ob, x2, gt1, sc2, sh2, g_post, g_pre, w, seq, tile_off, t):
    d = x2.shape[1]
    tpb = seq // TM
    row = lambda i: (i, 0)
    src = lambda i: (i + tile_off, 0)
    col = lambda i: (0, i)
    full = lambda i: (0, 0)
    per_b = lambda i: ((i + tile_off) // tpb, 0, 0)
    hw = oa.shape[1]
    in_specs = [
        pl.BlockSpec((TM, hw), src), pl.BlockSpec((TM, hw), src), pl.BlockSpec((TM, d), src),
        pl.BlockSpec((1, 1, d), per_b), pl.BlockSpec((1, 1, d), per_b), pl.BlockSpec((1, 1, d), per_b),
        pl.BlockSpec((1, d), full), pl.BlockSpec((1, d), full),
        pl.BlockSpec((hw, d), full), pl.BlockSpec((hw, d), full),
        pl.BlockSpec((N_EXPERTS, d), full), pl.BlockSpec((N_EXPERTS, 1), full),
    ]
    out_specs = [
        pl.BlockSpec((TM, d), row), pl.BlockSpec((TM, d // 2), row),
        pl.BlockSpec((TOP_K, TM), col), pl.BlockSpec((TOP_K, TM), col), pl.BlockSpec((TOP_K, TM), col),
        pl.BlockSpec((N_EXPERTS, LANES), full),
    ]
    out_shape = [
        jax.ShapeDtypeStruct((t, d), F32), jax.ShapeDtypeStruct((t, d // 2), U32),
        jax.ShapeDtypeStruct((TOP_K, t), jnp.int32), jax.ShapeDtypeStruct((TOP_K, t), F32),
        jax.ShapeDtypeStruct((TOP_K, t), jnp.int32),
        jax.ShapeDtypeStruct((N_EXPERTS, LANES), F32),
    ]
    return pl.pallas_call(
        _post_kernel,
        grid=(t // TM,),
        in_specs=in_specs,
        out_specs=out_specs,
        out_shape=out_shape,
        scratch_shapes=[pltpu.VMEM((N_EXPERTS, LANES), F32)],
        compiler_params=_cparams(("arbitrary",)),
        name="post",
    )(oa, ob, x2, gt1, sc2, sh2, g_post, g_pre, w["wo_a"], w["wo_b"], w["wr_t"], w["b_r"])


def _expert_kernel(be_ref, nu_ref, x_ref, wgu_ref, bgu_ref, wd_ref, bd_ref, y_ref, wgu_bf, wd_bf):
    j = pl.program_id(0)
    used = j < nu_ref[0]

    @pl.when(used & ((j == 0) | (be_ref[j] != be_ref[jnp.maximum(j - 1, 0)])))
    def _():
        wgu_bf[...] = wgu_ref[0].astype(BF16)
        wd_bf[...] = wd_ref[0].astype(BF16)

    @pl.when(used)
    def _():
        gus = []
        for rs in range(0, ROWS, EXPERT_SUB):
            x_lo, x_hi = _unpack_rows(x_ref[rs:rs + EXPERT_SUB])
            half = x_lo.shape[1]
            gu = jnp.dot(x_lo.astype(BF16), wgu_bf[:half], preferred_element_type=F32)
            gu += jnp.dot(x_hi.astype(BF16), wgu_bf[half:], preferred_element_type=F32)
            gus.append(gu + bgu_ref[0])
        for sub, gu in enumerate(gus):
            rows = slice(sub * EXPERT_SUB, (sub + 1) * EXPERT_SUB)
            gate = jnp.minimum(gu[:, :D_EXPERT], SWIGLU_LIMIT)
            up = jnp.clip(gu[:, D_EXPERT:], -SWIGLU_LIMIT, SWIGLU_LIMIT)
            glu = gate / (1.0 + jnp.exp(-SWIGLU_ALPHA * gate))
            act = ((up + 1.0) * glu).astype(BF16)
            y = jnp.dot(act, wd_bf[...], preferred_element_type=F32) + bd_ref[0]
            y_ref[rows] = _pack_rows(y)

    @pl.when(j >= nu_ref[0])
    def _():
        y_ref[...] = jnp.zeros_like(y_ref)


def _experts(xin, block_exp, n_used, wgu, bgu, wd, bd):
    p_rows, dw = xin.shape
    d = 2 * dw
    nb = p_rows // ROWS
    f2 = wgu.shape[2]
    grid_spec = pltpu.PrefetchScalarGridSpec(
        num_scalar_prefetch=2,
        grid=(nb,),
        in_specs=[
            pl.BlockSpec((ROWS, dw), lambda j, be, nu: (jnp.minimum(j, nu[0] - 1), 0)),
            pl.BlockSpec((1, d, f2), lambda j, be, nu: (be[j], 0, 0)),
            pl.BlockSpec((1, 1, f2), lambda j, be, nu: (be[j], 0, 0)),
            pl.BlockSpec((1, f2 // 2, d), lambda j, be, nu: (be[j], 0, 0)),
            pl.BlockSpec((1, 1, d), lambda j, be, nu: (be[j], 0, 0)),
        ],
        out_specs=pl.BlockSpec((ROWS, dw), lambda j, be, nu: (j, 0)),
        scratch_shapes=[pltpu.VMEM((d, f2), BF16), pltpu.VMEM((f2 // 2, d), BF16)],
    )
    return pl.pallas_call(
        _expert_kernel,
        grid_spec=grid_spec,
        out_shape=jax.ShapeDtypeStruct((p_rows, dw), U32),
        compiler_params=_cparams(("arbitrary",)),
        name="experts",
    )(block_exp, n_used, xin, wgu, bgu, wd, bd)


def _final_kernel(yg_ref, g_ref, x1_ref, gt_ref, gpost_ref, *rest):
    o_ref = rest[-1]
    g = g_ref[...]
    f_lo, f_hi = None, None
    for k in range(TOP_K):
        lo, hi = _unpack_rows(yg_ref[k])
        gk = g[:, k:k + 1]
        f_lo = lo * gk if f_lo is None else f_lo + lo * gk
        f_hi = hi * gk if f_hi is None else f_hi + hi * gk
    f = jnp.concatenate([f_lo, f_hi], axis=1)
    o_ref[...] = x1_ref[...] + gt_ref[0] * _rms(f, gpost_ref[...])


def _final(yg, gates_t, x1, gt2, g_post, seq, tile_off, t_all, prev_out):
    t, d = x1.shape
    tpb = seq // TM
    row = lambda i: (i, 0)
    in_specs = [pl.BlockSpec((TOP_K, TM, d // 2), lambda i: (0, i, 0)),
                pl.BlockSpec((TM, TOP_K), row),
                pl.BlockSpec((TM, d), row),
                pl.BlockSpec((1, 1, d), lambda i: ((i + tile_off) // tpb, 0, 0)),
                pl.BlockSpec((1, d), lambda i: (0, 0))]
    args = [yg, gates_t, x1, gt2, g_post]
    aliases = {}
    if prev_out is not None:
        in_specs.append(pl.BlockSpec(memory_space=pl.ANY))
        args.append(prev_out)
        aliases = {len(args) - 1: 0}
    return pl.pallas_call(
        _final_kernel,
        grid=(t // TM,),
        in_specs=in_specs,
        out_specs=pl.BlockSpec((TM, d), lambda i: (i + tile_off, 0)),
        out_shape=jax.ShapeDtypeStruct((t_all, d), F32),
        input_output_aliases=aliases,
        compiler_params=_cparams(("arbitrary",)),
        name="final",
    )(*args)


def _sc_mesh():
    return plsc.VectorSubcoreMesh(core_axis_name="c", subcore_axis_name="s")


def _sc_worker():
    return lax.axis_index("s") * SC_CORES + lax.axis_index("c")


def _dispatch(h2, dest, p_rows):
    t, dw = h2.shape
    per_w = t // SC_WORKERS
    n_win = per_w // SC_WIN

    @functools.partial(
        pl.kernel, mesh=_sc_mesh(),
        out_type=jax.ShapeDtypeStruct((p_rows, dw), h2.dtype),
        scratch_types=[pltpu.VMEM((TOP_K, SC_WIN), jnp.int32),
                       pltpu.VMEM((SC_WIN, dw), h2.dtype),
                       pltpu.SemaphoreType.DMA],
        name="dispatch",
    )
    def run(h_hbm, d_hbm, o_hbm, idx_v, rows_v, sem):
        wid = _sc_worker()

        @pl.loop(0, n_win)
        def _(wi):
            base = pl.multiple_of(wid * per_w + wi * SC_WIN, SC_WIN)
            for k in range(TOP_K):
                pltpu.sync_copy(d_hbm.at[pl.ds(k * t + base, SC_WIN)], idx_v.at[k])
            pltpu.sync_copy(h_hbm.at[pl.ds(base, SC_WIN)], rows_v)
            for k in range(TOP_K):
                pltpu.async_copy(rows_v, o_hbm.at[idx_v.at[k]], sem).wait()

    return run(h2, dest)


def _gather_rows(y, dest, t):
    _, dw = y.shape
    per_w = t // SC_WORKERS
    n_win = per_w // SC_WIN

    @functools.partial(
        pl.kernel, mesh=_sc_mesh(),
        out_type=jax.ShapeDtypeStruct((TOP_K, t, dw), y.dtype),
        scratch_types=[pltpu.VMEM((SC_WIN,), jnp.int32),
                       pltpu.VMEM((SC_WIN, dw), y.dtype),
                       pltpu.SemaphoreType.DMA],
        name="gather_rows",
    )
    def run(y_hbm, d_hbm, o_hbm, idx_v, rows_v, sem):
        wid = _sc_worker()

        @pl.loop(0, n_win)
        def _(wi):
            base = pl.multiple_of(wid * per_w + wi * SC_WIN, SC_WIN)
            for k in range(TOP_K):
                pltpu.sync_copy(d_hbm.at[pl.ds(k * t + base, SC_WIN)], idx_v)
                pltpu.async_copy(y_hbm.at[idx_v], rows_v, sem).wait()
                pltpu.sync_copy(rows_v, o_hbm.at[k, pl.ds(base, SC_WIN)])

    return run(y, dest)


def _prep_weights(w_in, g_q, w_qb, g_kv, w_kvb, w_o, w_router, b_router):
    d = w_in.shape[0]
    o = 0
    w_cq = w_in[:, o:o + Q_LORA]; o += Q_LORA
    w_ckv = w_in[:, o:o + KV_LORA]; o += KV_LORA
    w_kr = w_in[:, o:o + MLA_ROPE]; o += MLA_ROPE
    w_ca = w_in[:, o:]
    half = MLA_ROPE // 2
    zpad = lambda n: jnp.zeros((d, n), w_in.dtype)
    tail = HEAD_PAD - MLA_NOPE - MLA_ROPE
    w_kr_pad = jnp.concatenate([zpad(MLA_NOPE), w_kr, zpad(tail)], axis=1)
    w_in_all = jnp.concatenate([w_cq, w_ckv, w_kr_pad, w_ca], axis=1).astype(BF16)

    wq = w_qb.reshape(Q_LORA, MLA_HEADS, MLA_NOPE + MLA_ROPE)
    zq = jnp.zeros((Q_LORA, MLA_HEADS, tail), w_qb.dtype)
    wq_pad = jnp.concatenate([wq, zq], axis=-1).reshape(Q_LORA, -1).astype(BF16)

    wkv = w_kvb.reshape(KV_LORA, MLA_HEADS, MLA_NOPE + MLA_V)
    kn = wkv[..., :MLA_NOPE]
    wk = jnp.concatenate([kn, jnp.zeros((KV_LORA, MLA_HEADS, HEAD_PAD - MLA_NOPE), w_kvb.dtype)],
                         axis=-1).reshape(KV_LORA, -1).astype(BF16)
    wv = jnp.concatenate([wkv[..., MLA_NOPE:], jnp.zeros((KV_LORA, MLA_HEADS, HEAD_PAD - MLA_V), w_kvb.dtype)],
                         axis=-1).reshape(KV_LORA, -1).T.astype(BF16)
    v_one = jnp.tile((jnp.arange(HEAD_PAD) == MLA_V).astype(F32), MLA_HEADS).reshape(-1, 1)
    mla_w = MLA_HEADS * MLA_V
    return {
        "w_in": w_in_all, "g_q": g_q.reshape(1, -1), "wq": wq_pad,
        "g_kv": g_kv.reshape(1, -1), "wk": wk, "wv": wv, "v_one": v_one,
        "wo_a": w_o[:mla_w].astype(BF16), "wo_b": w_o[mla_w:].astype(BF16),
        "wr_t": w_router.T.astype(BF16), "b_r": b_router.reshape(-1, 1),
    }


def _rope_tables(seq):
    half = MLA_ROPE // 2
    inv_freq = ROPE_THETA ** (-jnp.arange(half, dtype=F32) / half)
    ang = jnp.arange(seq, dtype=F32)[:, None] * inv_freq[None, :]
    cos, sin = jnp.cos(ang), jnp.sin(ang)
    tail = HEAD_PAD - MLA_NOPE - MLA_ROPE
    ones = jnp.ones((seq, MLA_NOPE), F32)
    zn = jnp.zeros((seq, MLA_NOPE), F32)
    zt = jnp.zeros((seq, tail), F32)
    qs = (MLA_NOPE + MLA_ROPE) ** -0.5 * LOG2E
    return {
        "cq": jnp.concatenate([ones, cos, cos, zt], axis=1) * qs,
        "sq": jnp.concatenate([zn, -sin, sin, zt], axis=1) * qs,
        "ck": jnp.concatenate([zn, cos, cos, zt], axis=1),
        "sk": jnp.concatenate([zn, -sin, sin, zt], axis=1),
    }


def _bias_table(rel_bias):
    n = CQ + CBAND - 1
    rel = (CBAND - 1) - jnp.arange(n)
    diag = rel_bias.astype(F32)[:, jnp.clip(rel, -(CHUNK - 1), REL_MAX) + (CHUNK - 1)] * LOG2E
    diag = jnp.concatenate([diag, jnp.zeros((diag.shape[0], 1), F32)], axis=1)
    b = jnp.tile(diag, (1, CQ))[:, :CQ * n].reshape(-1, CQ, n)[:, :, CQ - 1:]
    r = jnp.arange(CQ)[:, None]
    c = jnp.arange(CBAND)[None, :]
    dchunk = r // CHUNK - (c // CHUNK - CA_LEFT)
    visible = (dchunk >= 0) & (dchunk <= CA_LEFT)
    exists = (c // CQ)[None] >= (2 - jnp.arange(3))[:, None, None]
    return jnp.where((visible[None] & exists)[:, None], b[None], NEG)


def _layer(x, c, w_ada, b_ada, g_pre_mix, g_post_mix, g_pre_ffn, g_post_ffn, w_in, g_q, w_qb,
           g_kv, w_kvb, rel_bias, w_o, w_router, b_router, w_gu, b_gu, w_down, b_down):
    bsz, seq, d = x.shape
    t = bsz * seq
    mod = _ada(c, w_ada, b_ada).reshape(bsz, 6, 1, d)
    sh1, sc1, gt1, sh2, sc2, gt2 = [mod[:, k] for k in range(6)]
    w = _prep_weights(w_in, g_q, w_qb, g_kv, w_kvb, w_o, w_router, b_router)
    tabs = _rope_tables(seq)
    x2 = x.reshape(t, d)

    q, k, v, qc, kc, vc = _proj(x2, sc1, sh1, g_pre_mix.reshape(1, d), w, tabs, seq)
    shp = lambda a: a.reshape(bsz, seq, a.shape[-1])
    oa = _mla(shp(q), shp(k), v).reshape(t, -1)
    ob = _chunk_attn(shp(qc), shp(kc), vc, jnp.swapaxes(_bias_table(rel_bias), 2, 3)).reshape(t, -1)

    tp = t // MOE_PARTS
    eids = jnp.arange(N_EXPERTS, dtype=jnp.int32)[:, None, None]
    p_rows = tp * TOP_K + N_EXPERTS * ROWS
    block_start = jnp.arange(p_rows // ROWS, dtype=jnp.int32) * ROWS
    routed = []
    for part in range(MOE_PARTS):
        x1, h2, top_idx, gates, rank, cnt = _post(oa, ob, x2, gt1, sc2, sh2, g_post_mix.reshape(1, d),
                                                  g_pre_ffn.reshape(1, d), w, seq, part * (tp // TM), tp)
        counts = cnt[:, 0].astype(jnp.int32)
        padded = ((counts + ROWS - 1) // ROWS) * ROWS
        pend = jnp.cumsum(padded)
        pstart = pend - padded
        dest = (jnp.sum(jnp.where(top_idx[None] == eids, pstart[:, None, None], 0), axis=0) + rank).reshape(-1)
        block_exp = jnp.minimum(jnp.sum(pend[None, :] <= block_start[:, None], axis=1),
                                N_EXPERTS - 1).astype(jnp.int32)
        n_used = (pend[-1:] // ROWS).astype(jnp.int32)
        routed.append((x1, gates.T, dest, block_exp, n_used, _dispatch(h2, dest, p_rows)))

    out = None
    for part, (x1, gates_t, dest, block_exp, n_used, xin) in enumerate(routed):
        y = _experts(xin, block_exp, n_used, w_gu, b_gu.reshape(N_EXPERTS, 1, -1),
                     w_down, b_down.reshape(N_EXPERTS, 1, -1))
        yg = _gather_rows(y, dest, tp)
        out = _final(yg, gates_t, x1, gt2, g_post_ffn.reshape(1, d), seq, part * (tp // TM), t, out)
    return out.reshape(bsz, seq, d)


def kernel(x, c, w_ada, b_ada, g_pre_mix, g_post_mix, g_pre_ffn, g_post_ffn, w_in, g_q, w_qb,
           g_kv, w_kvb, rel_bias, w_o, w_router, b_router, w_gu, b_gu, w_down, b_down):
    for l in range(w_ada.shape[0]):
        x = _layer(x, c, w_ada[l], b_ada[l], g_pre_mix[l], g_post_mix[l], g_pre_ffn[l], g_post_ffn[l],
                   w_in[l], g_q[l], w_qb[l], g_kv[l], w_kvb[l], rel_bias[l], w_o[l], w_router[l],
                   b_router[l], w_gu[l], b_gu[l], w_down[l], b_down[l])
    return x
```

```python
import functools
import math

import jax
import jax.numpy as jnp
from jax import lax
from jax.experimental import pallas as pl
from jax.experimental.pallas import tpu as pltpu
from jax.experimental.pallas import tpu_sc as plsc

F32 = jnp.float32
BF16 = jnp.bfloat16
U32 = jnp.uint32

D_MODEL = 1024
CHUNK = 64
EPS = 1e-6
MLA_HEADS = 8
MLA_NOPE = 64
MLA_ROPE = 32
MLA_V = 64
Q_LORA = 256
KV_LORA = 128
ROPE_THETA = 10000.0
CA_HEADS = 8
CA_DIM = 64
CA_LEFT = 8
REL_MAX = 256
N_EXPERTS = 32
TOP_K = 4
D_EXPERT = 1024
SWIGLU_LIMIT = 7.0
SWIGLU_ALPHA = 1.702

LANES = 128
HEAD_PAD = 128
LOG2E = math.log2(math.e)
NEG = -1e30
VMEM_LIMIT = 56 * 1024 * 1024

TM = 1024
POST_SUB = 128
PTM = 1024
PROJ_SUB = 1024
TQ = 512
TKW = 2 * TQ
MLA_HPS = 4
MLA_LEAD = 2
CQ = 256
CBAND = 3 * CQ
CSTEP = 2 * CQ
ROWS = 512
EXPERT_SUB = 256
MOE_PARTS = 2
SC_CORES = 2
SC_WORKERS = SC_CORES * 16
SC_WIN = 128
CA_W = CA_HEADS * CA_DIM
W_IN_COLS = Q_LORA + KV_LORA + LANES + 3 * CA_W


def _cparams(sem, flags=None):
    return pltpu.CompilerParams(dimension_semantics=sem, vmem_limit_bytes=VMEM_LIMIT, flags=flags)


def _nt_dot(a, b):
    return lax.dot_general(a, b, (((1,), (1,)), ((), ())), preferred_element_type=F32)


def _rms(x, g):
    return x * lax.rsqrt(jnp.mean(x * x, axis=-1, keepdims=True) + EPS) * g


def _pack_rows(x):
    n = x.shape[1] // 2
    lo = lax.bitcast_convert_type(x[:, :n].astype(BF16).astype(F32), U32)
    hi = lax.bitcast_convert_type(x[:, n:].astype(BF16).astype(F32), U32)
    return (lo >> 16) | hi


def _unpack_rows(p):
    lo = lax.bitcast_convert_type(p << 16, F32)
    hi = lax.bitcast_convert_type(p & jnp.uint32(0xFFFF0000), F32)
    return lo, hi


def _ada_kernel(c_ref, w_ref, b_ref, o_ref):
    c = c_ref[...]
    a = (c / (1.0 + jnp.exp(-c))).astype(BF16)
    o_ref[...] = jnp.dot(a, w_ref[...].astype(BF16), preferred_element_type=F32) + b_ref[...]


def _ada(c, w, b):
    bsz, d = c.shape
    n = w.shape[1]
    tn = 1024
    return pl.pallas_call(
        _ada_kernel,
        grid=(n // tn,),
        in_specs=[pl.BlockSpec((bsz, d), lambda j: (0, 0)),
                  pl.BlockSpec((d, tn), lambda j: (0, j)),
                  pl.BlockSpec((1, tn), lambda j: (0, j))],
        out_specs=pl.BlockSpec((bsz, tn), lambda j: (0, j)),
        out_shape=jax.ShapeDtypeStruct((bsz, n), F32),
        compiler_params=_cparams(("arbitrary",)),
        name="ada",
    )(c, w, b.reshape(1, n))


def _swap_halves(x):
    width = x.shape[1]
    half = MLA_ROPE // 2
    lane = lax.broadcasted_iota(jnp.int32, x.shape, 1) % HEAD_PAD
    return jnp.where(lane < MLA_NOPE + half, pltpu.roll(x, width - half, axis=1), pltpu.roll(x, half, axis=1))


def _proj_kernel(x_ref, sc_ref, sh_ref, g_ref, win_ref, gq_ref, wq_ref, gkv_ref,
                 wk_ref, wv_ref, vone_ref, cq_ref, sq_ref, ck_ref, sk_ref,
                 q_out, k_out, v_out, qc_out, kc_out, vc_out):
    zs = []
    for rs in range(0, PTM, PROJ_SUB):
        h = _rms(x_ref[rs:rs + PROJ_SUB], g_ref[...]) * (1.0 + sc_ref[0]) + sh_ref[0]
        zs.append(jnp.dot(h.astype(BF16), win_ref[...], preferred_element_type=F32))
    for sub, z in enumerate(zs):
        rows = slice(sub * PROJ_SUB, (sub + 1) * PROJ_SUB)
        o = 0
        cq = z[:, o:o + Q_LORA]; o += Q_LORA
        ckv = z[:, o:o + KV_LORA]; o += KV_LORA
        kr = z[:, o:o + LANES]; o += LANES
        qc = z[:, o:o + CA_W]; o += CA_W
        kc = z[:, o:o + CA_W]; o += CA_W
        vc = z[:, o:o + CA_W]

        cqn = _rms(cq, gq_ref[...]).astype(BF16)
        q = jnp.dot(cqn, wq_ref[...], preferred_element_type=F32)
        ckvn = _rms(ckv, gkv_ref[...]).astype(BF16)
        kn = jnp.dot(ckvn, wk_ref[...], preferred_element_type=F32)
        v_out[0, :, rows] = (_nt_dot(wv_ref[...], ckvn) + vone_ref[...]).astype(BF16)

        q_sw = _swap_halves(q)
        cq_t, sq_t = cq_ref[rows], sq_ref[rows]
        krope = kr * ck_ref[rows] + _swap_halves(kr) * sk_ref[rows]
        for hd in range(MLA_HEADS):
            sl = slice(hd * HEAD_PAD, (hd + 1) * HEAD_PAD)
            q_out[rows, sl] = (q[:, sl] * cq_t + q_sw[:, sl] * sq_t).astype(BF16)
            k_out[rows, sl] = (kn[:, sl] + krope).astype(BF16)

        qc_out[rows] = (qc * (CA_DIM ** -0.5 * LOG2E)).astype(BF16)
        kc_out[rows] = kc.astype(BF16)
        vc_out[0, :, rows] = vc.T.astype(BF16)


def _proj(x2, sc1, sh1, g_pre, w, tabs, seq):
    t, d = x2.shape
    tpb = seq // PTM
    row = lambda i: (i, 0)
    full = lambda i: (0, 0)
    per_b = lambda i: (i // tpb, 0, 0)
    pos = lambda i: (i % tpb, 0)
    hw = MLA_HEADS * HEAD_PAD
    in_specs = [
        pl.BlockSpec((PTM, d), row),
        pl.BlockSpec((1, 1, d), per_b), pl.BlockSpec((1, 1, d), per_b),
        pl.BlockSpec((1, d), full),
        pl.BlockSpec((d, W_IN_COLS), full),
        pl.BlockSpec((1, Q_LORA), full),
        pl.BlockSpec((Q_LORA, hw), full),
        pl.BlockSpec((1, KV_LORA), full),
        pl.BlockSpec((KV_LORA, hw), full), pl.BlockSpec((hw, KV_LORA), full), pl.BlockSpec((hw, 1), full),
        pl.BlockSpec((PTM, LANES), pos), pl.BlockSpec((PTM, LANES), pos),
        pl.BlockSpec((PTM, LANES), pos), pl.BlockSpec((PTM, LANES), pos),
    ]
    outs = ((hw, False), (hw, False), (hw, True), (CA_W, False), (CA_W, False), (CA_W, True))
    t_spec = lambda n: pl.BlockSpec((1, n, PTM), lambda i: (i // tpb, 0, i % tpb))
    return pl.pallas_call(
        _proj_kernel,
        grid=(t // PTM,),
        in_specs=in_specs,
        out_specs=[t_spec(n) if tr else pl.BlockSpec((PTM, n), row) for n, tr in outs],
        out_shape=[jax.ShapeDtypeStruct((t // seq, n, seq) if tr else (t, n), BF16) for n, tr in outs],
        compiler_params=_cparams(("arbitrary",)),
        name="proj",
    )(x2, sc1, sh1, g_pre, w["w_in"], w["g_q"], w["wq"], w["g_kv"], w["wk"], w["wv"], w["v_one"],
      tabs["cq"], tabs["sq"], tabs["ck"], tabs["sk"])


def _mla_kernel(q_ref, k_ref, v_ref, o_ref):
    i = pl.program_id(2)
    heads = [slice(hh * HEAD_PAD, (hh + 1) * HEAD_PAD) for hh in range(MLA_HPS)]

    def step(off, width, carry, masked=False):
        if masked:
            kc = lax.broadcasted_iota(jnp.int32, (width, TQ), 0) // CHUNK
            qc = lax.broadcasted_iota(jnp.int32, (width, TQ), 1) // CHUNK
            visible = kc <= qc

        def score(hs):
            s = _nt_dot(k_ref[0, pl.ds(off, width), hs], q_ref[0, :, hs])
            return jnp.where(visible, s, NEG) if masked else s

        scores = [score(hs) for hs in heads[:MLA_LEAD]]
        new = []
        for hh, hs in enumerate(heads):
            m, acc = carry[hh]
            m_new = jnp.maximum(m, jnp.max(scores[hh], axis=0, keepdims=True))
            p = jnp.exp2(scores[hh] - m_new).astype(BF16)
            if hh + MLA_LEAD < MLA_HPS:
                scores.append(score(heads[hh + MLA_LEAD]))
            pv = jnp.dot(v_ref[0, hs, pl.ds(off, width)], p, preferred_element_type=F32)
            new.append((m_new, jnp.exp2(m - m_new) * acc + pv))
        return tuple(new)

    init = tuple((jnp.full((1, TQ), NEG, F32), jnp.zeros((HEAD_PAD, TQ), F32)) for _ in heads)
    carry = lax.fori_loop(0, i // 2, lambda j, c: step(pl.multiple_of(j * TKW, TKW), TKW, c), init)
    carry = lax.fori_loop(0, i % 2, lambda _, c: step(pl.multiple_of((i - 1) * TQ, TQ), TQ, c), carry)
    carry = step(pl.multiple_of(i * TQ, TQ), TQ, carry, True)
    for pp in range(MLA_HPS // 2):
        pair = [acc[:MLA_V] / acc[MLA_V:MLA_V + 1] for _, acc in (carry[2 * pp], carry[2 * pp + 1])]
        o_ref[0, :, pp * LANES:(pp + 1) * LANES] = jnp.concatenate(pair, axis=0).T.astype(BF16)


def _mla(q, k, v):
    bsz, seq, _ = q.shape
    groups = MLA_HEADS // MLA_HPS
    return pl.pallas_call(
        _mla_kernel,
        grid=(bsz, groups, seq // TQ),
        in_specs=[pl.BlockSpec((1, TQ, MLA_HPS * HEAD_PAD), lambda b, p, i: (b, i, p)),
                  pl.BlockSpec((1, seq, MLA_HPS * HEAD_PAD), lambda b, p, i: (b, 0, p)),
                  pl.BlockSpec((1, MLA_HPS * HEAD_PAD, seq), lambda b, p, i: (b, p, 0))],
        out_specs=pl.BlockSpec((1, TQ, MLA_HPS * MLA_V), lambda b, p, i: (b, i, p)),
        out_shape=jax.ShapeDtypeStruct((bsz, seq, MLA_HEADS * MLA_V), BF16),
        compiler_params=_cparams(("arbitrary", "arbitrary", "arbitrary")),
        name="mla",
    )(q, k, v)


def _chunk_kernel(q_ref, kp_ref, kc_ref, vp_ref, vc_ref, bias0_ref, bias1_ref, o_ref):
    lane = lax.broadcasted_iota(jnp.int32, (CQ, LANES), 1)
    lo = lane < CA_DIM
    row = lax.broadcasted_iota(jnp.int32, (LANES, CQ), 0)
    top = row < CA_DIM
    bias_refs = (bias0_ref, bias1_ref)
    subs = CSTEP // CQ

    def key_rows(j):
        return (kp_ref if j < subs else kc_ref), slice((j % subs) * CQ, (j % subs + 1) * CQ)

    def score(sub, head):
        sl = slice((head // 2) * LANES, (head // 2 + 1) * LANES)
        q = q_ref[0, sub * CQ:(sub + 1) * CQ, sl]
        qm = jnp.where(lo if head % 2 == 0 else jnp.logical_not(lo), q, jnp.zeros_like(q))
        parts = []
        for cb in range(3):
            ref, rows = key_rows(sub + cb)
            parts.append(_nt_dot(ref[0, rows, sl], qm))
        return jnp.concatenate(parts, axis=0)

    scores = {(sub, h): score(sub, h) for sub in range(subs) for h in range(CA_HEADS)}
    for sub in range(subs):
        for p in range(CA_HEADS // 2):
            sl = slice(p * LANES, (p + 1) * LANES)
            vts = []
            for cb in range(3):
                ref, cols = key_rows(sub + cb)
                vts.append((vp_ref if ref is kp_ref else vc_ref)[0, sl, cols])
            outs = []
            for hh in range(2):
                mine = top if hh == 0 else jnp.logical_not(top)
                den_row = CA_DIM if hh == 0 else 0
                ones_row = (row == den_row).astype(BF16)
                s = scores[(sub, 2 * p + hh)] + bias_refs[sub][0, 2 * p + hh]
                m = jnp.max(s, axis=0, keepdims=True)
                pb = jnp.exp2(s - m).astype(BF16)
                o = None
                for cb in range(3):
                    part = jnp.dot(jnp.where(mine, vts[cb], ones_row), pb[cb * CQ:(cb + 1) * CQ],
                                   preferred_element_type=F32)
                    o = part if o is None else o + part
                o = o / o[den_row:den_row + 1]
                outs.append(o[:CA_DIM] if hh == 0 else o[CA_DIM:])
            o_ref[0, sub * CQ:(sub + 1) * CQ, sl] = jnp.concatenate(outs, axis=0).T.astype(BF16)


def _chunk_attn(qc, kc, vct, bias):
    bsz, seq, w = qc.shape
    subs = CSTEP // CQ
    prev = lambda i: jnp.maximum(i - 1, 0)
    table = lambda s: pl.BlockSpec((1, CA_HEADS, CBAND, CQ), lambda b, i: (jnp.minimum(subs * i + s, 2), 0, 0, 0))
    return pl.pallas_call(
        _chunk_kernel,
        grid=(bsz, seq // CSTEP),
        in_specs=[pl.BlockSpec((1, CSTEP, w), lambda b, i: (b, i, 0)),
                  pl.BlockSpec((1, CSTEP, w), lambda b, i: (b, prev(i), 0)),
                  pl.BlockSpec((1, CSTEP, w), lambda b, i: (b, i, 0)),
                  pl.BlockSpec((1, w, CSTEP), lambda b, i: (b, 0, prev(i))),
                  pl.BlockSpec((1, w, CSTEP), lambda b, i: (b, 0, i)),
                  table(0), table(1)],
        out_specs=pl.BlockSpec((1, CSTEP, w), lambda b, i: (b, i, 0)),
        out_shape=jax.ShapeDtypeStruct((bsz, seq, w), BF16),
        compiler_params=_cparams(("arbitrary", "arbitrary")),
        name="chunk_attn",
    )(qc, kc, kc, vct, vct, bias, bias)


def _post_kernel(oa_ref, ob_ref, x_ref, gt_ref, sc_ref, sh_ref, gpost_ref, gpre_ref,
                 woa_ref, wob_ref, wr_ref, br_ref,
                 x1_out, h2_out, idx_out, gate_out, rank_out, cnt_out, carry_ref):
    t = pl.program_id(0)

    @pl.when(t == 0)
    def _():
        carry_ref[...] = jnp.zeros_like(carry_ref)

    os = []
    for rs in range(0, TM, POST_SUB):
        rows = slice(rs, rs + POST_SUB)
        o = jnp.dot(oa_ref[rows], woa_ref[...], preferred_element_type=F32)
        os.append(o + jnp.dot(ob_ref[rows], wob_ref[...], preferred_element_type=F32))
    h2s = []
    gain1 = gt_ref[0] * gpost_ref[...]
    gain2 = gpre_ref[...] * (1.0 + sc_ref[0])
    for sub, o in enumerate(os):
        rows = slice(sub * POST_SUB, (sub + 1) * POST_SUB)
        x1 = x_ref[rows] + _rms(o, gain1)
        x1_out[rows] = x1
        h2f = _rms(x1, gain2) + sh_ref[0]
        h2_out[rows] = _pack_rows(h2f)
        h2s.append(h2f.astype(BF16))
    h2 = jnp.concatenate(h2s, axis=0)

    logits = _nt_dot(wr_ref[...], h2) + br_ref[...]
    eid = lax.broadcasted_iota(jnp.int32, (N_EXPERTS, TM), 0)
    vals, idxs = [], []
    work = logits
    for _k in range(TOP_K):
        m = jnp.max(work, axis=0, keepdims=True)
        ix = jnp.min(jnp.where(work == m, eid, N_EXPERTS), axis=0, keepdims=True)
        work = jnp.where(eid == ix, -jnp.inf, work)
        vals.append(m)
        idxs.append(ix)
    es = [jnp.exp(v - vals[0]) for v in vals]
    den = es[0] + es[1] + es[2] + es[3]
    gate_out[...] = jnp.concatenate([e / den for e in es], axis=0)
    idx_out[...] = jnp.concatenate(idxs, axis=0)

    sel = (eid == idxs[0]) | (eid == idxs[1]) | (eid == idxs[2]) | (eid == idxs[3])
    self32 = sel.astype(F32)
    rr = lax.broadcasted_iota(jnp.int32, (TM, TM), 0)
    cc = lax.broadcasted_iota(jnp.int32, (TM, TM), 1)
    upper = (rr < cc).astype(BF16)
    before = jnp.dot(self32.astype(BF16), upper, preferred_element_type=F32)
    before = before + carry_ref[:, 0:1]
    ranks = [jnp.sum(jnp.where(eid == ix, before, 0.0), axis=0, keepdims=True) for ix in idxs]
    rank_out[...] = jnp.concatenate(ranks, axis=0).astype(jnp.int32)
    carry_ref[...] = carry_ref[...] + jnp.sum(self32, axis=1, keepdims=True)
    cnt_out[...] = carry_ref[...]


def _post(oa, ob, x2, gt1, sc2, sh2, g_post, g_pre, w, seq, tile_off, t):
    d = x2.shape[1]
    tpb = seq // TM
    row = lambda i: (i, 0)
    src = lambda i: (i + tile_off, 0)
    col = lambda i: (0, i)
    full = lambda i: (0, 0)
    per_b = lambda i: ((i + tile_off) // tpb, 0, 0)
    hw = oa.shape[1]
    in_specs = [
        pl.BlockSpec((TM, hw), src), pl.BlockSpec((TM, hw), src), pl.BlockSpec((TM, d), src),
        pl.BlockSpec((1, 1, d), per_b), pl.BlockSpec((1, 1, d), per_b), pl.BlockSpec((1, 1, d), per_b),
        pl.BlockSpec((1, d), full), pl.BlockSpec((1, d), full),
        pl.BlockSpec((hw, d), full), pl.BlockSpec((hw, d), full),
        pl.BlockSpec((N_EXPERTS, d), full), pl.BlockSpec((N_EXPERTS, 1), full),
    ]
    out_specs = [
        pl.BlockSpec((TM, d), row), pl.BlockSpec((TM, d // 2), row),
        pl.BlockSpec((TOP_K, TM), col), pl.BlockSpec((TOP_K, TM), col), pl.BlockSpec((TOP_K, TM), col),
        pl.BlockSpec((N_EXPERTS, LANES), full),
    ]
    out_shape = [
        jax.ShapeDtypeStruct((t, d), F32), jax.ShapeDtypeStruct((t, d // 2), U32),
        jax.ShapeDtypeStruct((TOP_K, t), jnp.int32), jax.ShapeDtypeStruct((TOP_K, t), F32),
        jax.ShapeDtypeStruct((TOP_K, t), jnp.int32),
        jax.ShapeDtypeStruct((N_EXPERTS, LANES), F32),
    ]
    return pl.pallas_call(
        _post_kernel,
        grid=(t // TM,),
        in_specs=in_specs,
        out_specs=out_specs,
        out_shape=out_shape,
        scratch_shapes=[pltpu.VMEM((N_EXPERTS, LANES), F32)],
        compiler_params=_cparams(("arbitrary",)),
        name="post",
    )(oa, ob, x2, gt1, sc2, sh2, g_post, g_pre, w["wo_a"], w["wo_b"], w["wr_t"], w["b_r"])


def _expert_kernel(be_ref, nu_ref, x_ref, wgu_ref, bgu_ref, wd_ref, bd_ref, y_ref, wgu_bf, wd_bf):
    j = pl.program_id(0)
    used = j < nu_ref[0]

    @pl.when(used & ((j == 0) | (be_ref[j] != be_ref[jnp.maximum(j - 1, 0)])))
    def _():
        wgu_bf[...] = wgu_ref[0].astype(BF16)
        wd_bf[...] = wd_ref[0].astype(BF16)

    @pl.when(used)
    def _():
        gus = []
        for rs in range(0, ROWS, EXPERT_SUB):
            x_lo, x_hi = _unpack_rows(x_ref[rs:rs + EXPERT_SUB])
            half = x_lo.shape[1]
            gu = jnp.dot(x_lo.astype(BF16), wgu_bf[:half], preferred_element_type=F32)
            gu += jnp.dot(x_hi.astype(BF16), wgu_bf[half:], preferred_element_type=F32)
            gus.append(gu + bgu_ref[0])
        for sub, gu in enumerate(gus):
            rows = slice(sub * EXPERT_SUB, (sub + 1) * EXPERT_SUB)
            gate = jnp.minimum(gu[:, :D_EXPERT], SWIGLU_LIMIT)
            up = jnp.clip(gu[:, D_EXPERT:], -SWIGLU_LIMIT, SWIGLU_LIMIT)
            glu = gate / (1.0 + jnp.exp(-SWIGLU_ALPHA * gate))
            act = ((up + 1.0) * glu).astype(BF16)
            y = jnp.dot(act, wd_bf[...], preferred_element_type=F32) + bd_ref[0]
            y_ref[rows] = _pack_rows(y)

    @pl.when(j >= nu_ref[0])
    def _():
        y_ref[...] = jnp.zeros_like(y_ref)


def _experts(xin, block_exp, n_used, wgu, bgu, wd, bd):
    p_rows, dw = xin.shape
    d = 2 * dw
    nb = p_rows // ROWS
    f2 = wgu.shape[2]
    grid_spec = pltpu.PrefetchScalarGridSpec(
        num_scalar_prefetch=2,
        grid=(nb,),
        in_specs=[
            pl.BlockSpec((ROWS, dw), lambda j, be, nu: (jnp.minimum(j, nu[0] - 1), 0)),
            pl.BlockSpec((1, d, f2), lambda j, be, nu: (be[j], 0, 0)),
            pl.BlockSpec((1, 1, f2), lambda j, be, nu: (be[j], 0, 0)),
            pl.BlockSpec((1, f2 // 2, d), lambda j, be, nu: (be[j], 0, 0)),
            pl.BlockSpec((1, 1, d), lambda j, be, nu: (be[j], 0, 0)),
        ],
        out_specs=pl.BlockSpec((ROWS, dw), lambda j, be, nu: (j, 0)),
        scratch_shapes=[pltpu.VMEM((d, f2), BF16), pltpu.VMEM((f2 // 2, d), BF16)],
    )
    return pl.pallas_call(
        _expert_kernel,
        grid_spec=grid_spec,
        out_shape=jax.ShapeDtypeStruct((p_rows, dw), U32),
        compiler_params=_cparams(("arbitrary",)),
        name="experts",
    )(block_exp, n_used, xin, wgu, bgu, wd, bd)


def _final_kernel(yg_ref, g_ref, x1_ref, gt_ref, gpost_ref, *rest):
    o_ref = rest[-1]
    g = g_ref[...]
    f_lo, f_hi = None, None
    for k in range(TOP_K):
        lo, hi = _unpack_rows(yg_ref[k])
        gk = g[:, k:k + 1]
        f_lo = lo * gk if f_lo is None else f_lo + lo * gk
        f_hi = hi * gk if f_hi is None else f_hi + hi * gk
    f = jnp.concatenate([f_lo, f_hi], axis=1)
    o_ref[...] = x1_ref[...] + gt_ref[0] * _rms(f, gpost_ref[...])


def _final(yg, gates_t, x1, gt2, g_post, seq, tile_off, t_all, prev_out):
    t, d = x1.shape
    tpb = seq // TM
    row = lambda i: (i, 0)
    in_specs = [pl.BlockSpec((TOP_K, TM, d // 2), lambda i: (0, i, 0)),
                pl.BlockSpec((TM, TOP_K), row),
                pl.BlockSpec((TM, d), row),
                pl.BlockSpec((1, 1, d), lambda i: ((i + tile_off) // tpb, 0, 0)),
                pl.BlockSpec((1, d), lambda i: (0, 0))]
    args = [yg, gates_t, x1, gt2, g_post]
    aliases = {}
    if prev_out is not None:
        in_specs.append(pl.BlockSpec(memory_space=pl.ANY))
        args.append(prev_out)
        aliases = {len(args) - 1: 0}
    return pl.pallas_call(
        _final_kernel,
        grid=(t // TM,),
        in_specs=in_specs,
        out_specs=pl.BlockSpec((TM, d), lambda i: (i + tile_off, 0)),
        out_shape=jax.ShapeDtypeStruct((t_all, d), F32),
        input_output_aliases=aliases,
        compiler_params=_cparams(("arbitrary",)),
        name="final",
    )(*args)


def _sc_mesh():
    return plsc.VectorSubcoreMesh(core_axis_name="c", subcore_axis_name="s")


def _sc_worker():
    return lax.axis_index("s") * SC_CORES + lax.axis_index("c")


def _dispatch(h2, dest, p_rows):
    t, dw = h2.shape
    per_w = t // SC_WORKERS
    n_win = per_w // SC_WIN

    @functools.partial(
        pl.kernel, mesh=_sc_mesh(),
        out_type=jax.ShapeDtypeStruct((p_rows, dw), h2.dtype),
        scratch_types=[pltpu.VMEM((TOP_K, SC_WIN), jnp.int32),
                       pltpu.VMEM((SC_WIN, dw), h2.dtype),
                       pltpu.SemaphoreType.DMA],
        name="dispatch",
    )
    def run(h_hbm, d_hbm, o_hbm, idx_v, rows_v, sem):
        wid = _sc_worker()

        @pl.loop(0, n_win)
        def _(wi):
            base = pl.multiple_of(wid * per_w + wi * SC_WIN, SC_WIN)
            for k in range(TOP_K):
                pltpu.sync_copy(d_hbm.at[pl.ds(k * t + base, SC_WIN)], idx_v.at[k])
            pltpu.sync_copy(h_hbm.at[pl.ds(base, SC_WIN)], rows_v)
            for k in range(TOP_K):
                pltpu.async_copy(rows_v, o_hbm.at[idx_v.at[k]], sem).wait()

    return run(h2, dest)


def _gather_rows(y, dest, t):
    _, dw = y.shape
    per_w = t // SC_WORKERS
    n_win = per_w // SC_WIN

    @functools.partial(
        pl.kernel, mesh=_sc_mesh(),
        out_type=jax.ShapeDtypeStruct((TOP_K, t, dw), y.dtype),
        scratch_types=[pltpu.VMEM((SC_WIN,), jnp.int32),
                       pltpu.VMEM((SC_WIN, dw), y.dtype),
                       pltpu.SemaphoreType.DMA],
        name="gather_rows",
    )
    def run(y_hbm, d_hbm, o_hbm, idx_v, rows_v, sem):
        wid = _sc_worker()

        @pl.loop(0, n_win)
        def _(wi):
            base = pl.multiple_of(wid * per_w + wi * SC_WIN, SC_WIN)
            for k in range(TOP_K):
                pltpu.sync_copy(d_hbm.at[pl.ds(k * t + base, SC_WIN)], idx_v)
                pltpu.async_copy(y_hbm.at[idx_v], rows_v, sem).wait()
                pltpu.sync_copy(rows_v, o_hbm.at[k, pl.ds(base, SC_WIN)])

    return run(y, dest)


def _prep_weights(w_in, g_q, w_qb, g_kv, w_kvb, w_o, w_router, b_router):
    d = w_in.shape[0]
    o = 0
    w_cq = w_in[:, o:o + Q_LORA]; o += Q_LORA
    w_ckv = w_in[:, o:o + KV_LORA]; o += KV_LORA
    w_kr = w_in[:, o:o + MLA_ROPE]; o += MLA_ROPE
    w_ca = w_in[:, o:]
    half = MLA_ROPE // 2
    zpad = lambda n: jnp.zeros((d, n), w_in.dtype)
    tail = HEAD_PAD - MLA_NOPE - MLA_ROPE
    w_kr_pad = jnp.concatenate([zpad(MLA_NOPE), w_kr, zpad(tail)], axis=1)
    w_in_all = jnp.concatenate([w_cq, w_ckv, w_kr_pad, w_ca], axis=1).astype(BF16)

    wq = w_qb.reshape(Q_LORA, MLA_HEADS, MLA_NOPE + MLA_ROPE)
    zq = jnp.zeros((Q_LORA, MLA_HEADS, tail), w_qb.dtype)
    wq_pad = jnp.concatenate([wq, zq], axis=-1).reshape(Q_LORA, -1).astype(BF16)

    wkv = w_kvb.reshape(KV_LORA, MLA_HEADS, MLA_NOPE + MLA_V)
    kn = wkv[..., :MLA_NOPE]
    wk = jnp.concatenate([kn, jnp.zeros((KV_LORA, MLA_HEADS, HEAD_PAD - MLA_NOPE), w_kvb.dtype)],
                         axis=-1).reshape(KV_LORA, -1).astype(BF16)
    wv = jnp.concatenate([wkv[..., MLA_NOPE:], jnp.zeros((KV_LORA, MLA_HEADS, HEAD_PAD - MLA_V), w_kvb.dtype)],
                         axis=-1).reshape(KV_LORA, -1).T.astype(BF16)
    v_one = jnp.tile((jnp.arange(HEAD_PAD) == MLA_V).astype(F32), MLA_HEADS).reshape(-1, 1)
    mla_w = MLA_HEADS * MLA_V
    return {
        "w_in": w_in_all, "g_q": g_q.reshape(1, -1), "wq": wq_pad,
        "g_kv": g_kv.reshape(1, -1), "wk": wk, "wv": wv, "v_one": v_one,
        "wo_a": w_o[:mla_w].astype(BF16), "wo_b": w_o[mla_w:].astype(BF16),
        "wr_t": w_router.T.astype(BF16), "b_r": b_router.reshape(-1, 1),
    }


def _rope_tables(seq):
    half = MLA_ROPE // 2
    inv_freq = ROPE_THETA ** (-jnp.arange(half, dtype=F32) / half)
    ang = jnp.arange(seq, dtype=F32)[:, None] * inv_freq[None, :]
    cos, sin = jnp.cos(ang), jnp.sin(ang)
    tail = HEAD_PAD - MLA_NOPE - MLA_ROPE
    ones = jnp.ones((seq, MLA_NOPE), F32)
    zn = jnp.zeros((seq, MLA_NOPE), F32)
    zt = jnp.zeros((seq, tail), F32)
    qs = (MLA_NOPE + MLA_ROPE) ** -0.5 * LOG2E
    return {
        "cq": jnp.concatenate([ones, cos, cos, zt], axis=1) * qs,
        "sq": jnp.concatenate([zn, -sin, sin, zt], axis=1) * qs,
        "ck": jnp.concatenate([zn, cos, cos, zt], axis=1),
        "sk": jnp.concatenate([zn, -sin, sin, zt], axis=1),
    }


def _bias_table(rel_bias):
    n = CQ + CBAND - 1
    rel = (CBAND - 1) - jnp.arange(n)
    diag = rel_bias.astype(F32)[:, jnp.clip(rel, -(CHUNK - 1), REL_MAX) + (CHUNK - 1)] * LOG2E
    diag = jnp.concatenate([diag, jnp.zeros((diag.shape[0], 1), F32)], axis=1)
    b = jnp.tile(diag, (1, CQ))[:, :CQ * n].reshape(-1, CQ, n)[:, :, CQ - 1:]
    r = jnp.arange(CQ)[:, None]
    c = jnp.arange(CBAND)[None, :]
    dchunk = r // CHUNK - (c // CHUNK - CA_LEFT)
    visible = (dchunk >= 0) & (dchunk <= CA_LEFT)
    exists = (c // CQ)[None] >= (2 - jnp.arange(3))[:, None, None]
    return jnp.where((visible[None] & exists)[:, None], b[None], NEG)


def _layer(x, c, w_ada, b_ada, g_pre_mix, g_post_mix, g_pre_ffn, g_post_ffn, w_in, g_q, w_qb,
           g_kv, w_kvb, rel_bias, w_o, w_router, b_router, w_gu, b_gu, w_down, b_down):
    bsz, seq, d = x.shape
    t = bsz * seq
    mod = _ada(c, w_ada, b_ada).reshape(bsz, 6, 1, d)
    sh1, sc1, gt1, sh2, sc2, gt2 = [mod[:, k] for k in range(6)]
    w = _prep_weights(w_in, g_q, w_qb, g_kv, w_kvb, w_o, w_router, b_router)
    tabs = _rope_tables(seq)
    x2 = x.reshape(t, d)

    q, k, v, qc, kc, vc = _proj(x2, sc1, sh1, g_pre_mix.reshape(1, d), w, tabs, seq)
    shp = lambda a: a.reshape(bsz, seq, a.shape[-1])
    oa = _mla(shp(q), shp(k), v).reshape(t, -1)
    ob = _chunk_attn(shp(qc), shp(kc), vc, jnp.swapaxes(_bias_table(rel_bias), 2, 3)).reshape(t, -1)

    tp = t // MOE_PARTS
    eids = jnp.arange(N_EXPERTS, dtype=jnp.int32)[:, None, None]
    p_rows = tp * TOP_K + N_EXPERTS * ROWS
    block_start = jnp.arange(p_rows // ROWS, dtype=jnp.int32) * ROWS
    routed = []
    for part in range(MOE_PARTS):
        x1, h2, top_idx, gates, rank, cnt = _post(oa, ob, x2, gt1, sc2, sh2, g_post_mix.reshape(1, d),
                                                  g_pre_ffn.reshape(1, d), w, seq, part * (tp // TM), tp)
        counts = cnt[:, 0].astype(jnp.int32)
        padded = ((counts + ROWS - 1) // ROWS) * ROWS
        pend = jnp.cumsum(padded)
        pstart = pend - padded
        dest = (jnp.sum(jnp.where(top_idx[None] == eids, pstart[:, None, None], 0), axis=0) + rank).reshape(-1)
        block_exp = jnp.minimum(jnp.sum(pend[None, :] <= block_start[:, None], axis=1),
                                N_EXPERTS - 1).astype(jnp.int32)
        n_used = (pend[-1:] // ROWS).astype(jnp.int32)
        routed.append((x1, gates.T, dest, block_exp, n_used, _dispatch(h2, dest, p_rows)))

    out = None
    for part, (x1, gates_t, dest, block_exp, n_used, xin) in enumerate(routed):
        y = _experts(xin, block_exp, n_used, w_gu, b_gu.reshape(N_EXPERTS, 1, -1),
                     w_down, b_down.reshape(N_EXPERTS, 1, -1))
        yg = _gather_rows(y, dest, tp)
        out = _final(yg, gates_t, x1, gt2, g_post_ffn.reshape(1, d), seq, part * (tp // TM), t, out)
    return out.reshape(bsz, seq, d)


def kernel(x, c, w_ada, b_ada, g_pre_mix, g_post_mix, g_pre_ffn, g_post_ffn, w_in, g_q, w_qb,
           g_kv, w_kvb, rel_bias, w_o, w_router, b_router, w_gu, b_gu, w_down, b_down):
    for l in range(w_ada.shape[0]):
        x = _layer(x, c, w_ada[l], b_ada[l], g_pre_mix[l], g_post_mix[l], g_pre_ffn[l], g_post_ffn[l],
                   w_in[l], g_q[l], w_qb[l], g_kv[l], w_kvb[l], rel_bias[l], w_o[l], w_router[l],
                   b_router[l], w_gu[l], b_gu[l], w_down[l], b_down[l])
    return x
```

```python
import functools
import math

import jax
import jax.numpy as jnp
from jax import lax
from jax.experimental import pallas as pl
from jax.experimental.pallas import tpu as pltpu
from jax.experimental.pallas import tpu_sc as plsc

F32 = jnp.float32
BF16 = jnp.bfloat16
U32 = jnp.uint32

D_MODEL = 1024
CHUNK = 64
EPS = 1e-6
MLA_HEADS = 8
MLA_NOPE = 64
MLA_ROPE = 32
MLA_V = 64
Q_LORA = 256
KV_LORA = 128
ROPE_THETA = 10000.0
CA_HEADS = 8
CA_DIM = 64
CA_LEFT = 8
REL_MAX = 256
N_EXPERTS = 32
TOP_K = 4
D_EXPERT = 1024
SWIGLU_LIMIT = 7.0
SWIGLU_ALPHA = 1.702

LANES = 128
HEAD_PAD = 128
LOG2E = math.log2(math.e)
NEG = -1e30
VMEM_LIMIT = 56 * 1024 * 1024

TM = 1024
POST_SUB = 128
PTM = 1024
PROJ_SUB = 1024
TQ = 512
TKW = 2 * TQ
MLA_HPS = 4
MLA_LEAD = 2
CQ = 256
CBAND = 3 * CQ
CSTEP = 2 * CQ
ROWS = 512
EXPERT_SUB = 256
MOE_PARTS = 2
SC_CORES = 2
SC_WORKERS = SC_CORES * 16
SC_WIN = 128
CA_W = CA_HEADS * CA_DIM
W_IN_COLS = Q_LORA + KV_LORA + LANES + 3 * CA_W


def _cparams(sem, flags=None):
    return pltpu.CompilerParams(dimension_semantics=sem, vmem_limit_bytes=VMEM_LIMIT, flags=flags)


def _nt_dot(a, b):
    return lax.dot_general(a, b, (((1,), (1,)), ((), ())), preferred_element_type=F32)


def _rms(x, g):
    return x * lax.rsqrt(jnp.mean(x * x, axis=-1, keepdims=True) + EPS) * g


def _pack_rows(x):
    n = x.shape[1] // 2
    lo = lax.bitcast_convert_type(x[:, :n].astype(BF16).astype(F32), U32)
    hi = lax.bitcast_convert_type(x[:, n:].astype(BF16).astype(F32), U32)
    return (lo >> 16) | hi


def _unpack_rows(p):
    lo = lax.bitcast_convert_type(p << 16, F32)
    hi = lax.bitcast_convert_type(p & jnp.uint32(0xFFFF0000), F32)
    return lo, hi


def _ada_kernel(c_ref, w_ref, b_ref, o_ref):
    c = c_ref[...]
    a = (c / (1.0 + jnp.exp(-c))).astype(BF16)
    o_ref[...] = jnp.dot(a, w_ref[...].astype(BF16), preferred_element_type=F32) + b_ref[...]


def _ada(c, w, b):
    bsz, d = c.shape
    n = w.shape[1]
    tn = 1024
    return pl.pallas_call(
        _ada_kernel,
        grid=(n // tn,),
        in_specs=[pl.BlockSpec((bsz, d), lambda j: (0, 0)),
                  pl.BlockSpec((d, tn), lambda j: (0, j)),
                  pl.BlockSpec((1, tn), lambda j: (0, j))],
        out_specs=pl.BlockSpec((bsz, tn), lambda j: (0, j)),
        out_shape=jax.ShapeDtypeStruct((bsz, n), F32),
        compiler_params=_cparams(("arbitrary",)),
        name="ada",
    )(c, w, b.reshape(1, n))


def _swap_halves(x):
    width = x.shape[1]
    half = MLA_ROPE // 2
    lane = lax.broadcasted_iota(jnp.int32, x.shape, 1) % HEAD_PAD
    return jnp.where(lane < MLA_NOPE + half, pltpu.roll(x, width - half, axis=1), pltpu.roll(x, half, axis=1))


def _proj_kernel(x_ref, sc_ref, sh_ref, g_ref, win_ref, gq_ref, wq_ref, gkv_ref,
                 wk_ref, wv_ref, vone_ref, cq_ref, sq_ref, ck_ref, sk_ref,
                 q_out, k_out, v_out, qc_out, kc_out, vc_out):
    zs = []
    for rs in range(0, PTM, PROJ_SUB):
        h = _rms(x_ref[rs:rs + PROJ_SUB], g_ref[...]) * (1.0 + sc_ref[0]) + sh_ref[0]
        zs.append(jnp.dot(h.astype(BF16), win_ref[...], preferred_element_type=F32))
    for sub, z in enumerate(zs):
        rows = slice(sub * PROJ_SUB, (sub + 1) * PROJ_SUB)
        o = 0
        cq = z[:, o:o + Q_LORA]; o += Q_LORA
        ckv = z[:, o:o + KV_LORA]; o += KV_LORA
        kr = z[:, o:o + LANES]; o += LANES
        qc = z[:, o:o + CA_W]; o += CA_W
        kc = z[:, o:o + CA_W]; o += CA_W
        vc = z[:, o:o + CA_W]

        cqn = _rms(cq, gq_ref[...]).astype(BF16)
        q = jnp.dot(cqn, wq_ref[...], preferred_element_type=F32)
        ckvn = _rms(ckv, gkv_ref[...]).astype(BF16)
        kn = jnp.dot(ckvn, wk_ref[...], preferred_element_type=F32)
        v_out[0, :, rows] = (_nt_dot(wv_ref[...], ckvn) + vone_ref[...]).astype(BF16)

        q_sw = _swap_halves(q)
        cq_t, sq_t = cq_ref[rows], sq_ref[rows]
        krope = kr * ck_ref[rows] + _swap_halves(kr) * sk_ref[rows]
        for hd in range(MLA_HEADS):
            sl = slice(hd * HEAD_PAD, (hd + 1) * HEAD_PAD)
            q_out[rows, sl] = (q[:, sl] * cq_t + q_sw[:, sl] * sq_t).astype(BF16)
            k_out[rows, sl] = (kn[:, sl] + krope).astype(BF16)

        qc_out[rows] = (qc * (CA_DIM ** -0.5 * LOG2E)).astype(BF16)
        kc_out[rows] = kc.astype(BF16)
        vc_out[0, :, rows] = vc.T.astype(BF16)


def _proj(x2, sc1, sh1, g_pre, w, tabs, seq, batch_off, t):
    d = x2.shape[1]
    tpb = seq // PTM
    row = lambda i: (i, 0)
    full = lambda i: (0, 0)
    per_b = lambda i: (i // tpb + batch_off, 0, 0)
    pos = lambda i: (i % tpb, 0)
    hw = MLA_HEADS * HEAD_PAD
    in_specs = [
        pl.BlockSpec((PTM, d), lambda i: (i + batch_off * tpb, 0)),
        pl.BlockSpec((1, 1, d), per_b), pl.BlockSpec((1, 1, d), per_b),
        pl.BlockSpec((1, d), full),
        pl.BlockSpec((d, W_IN_COLS), full),
        pl.BlockSpec((1, Q_LORA), full),
        pl.BlockSpec((Q_LORA, hw), full),
        pl.BlockSpec((1, KV_LORA), full),
        pl.BlockSpec((KV_LORA, hw), full), pl.BlockSpec((hw, KV_LORA), full), pl.BlockSpec((hw, 1), full),
        pl.BlockSpec((PTM, LANES), pos), pl.BlockSpec((PTM, LANES), pos),
        pl.BlockSpec((PTM, LANES), pos), pl.BlockSpec((PTM, LANES), pos),
    ]
    outs = ((hw, False), (hw, False), (hw, True), (CA_W, False), (CA_W, False), (CA_W, True))
    t_spec = lambda n: pl.BlockSpec((1, n, PTM), lambda i: (i // tpb, 0, i % tpb))
    return pl.pallas_call(
        _proj_kernel,
        grid=(t // PTM,),
        in_specs=in_specs,
        out_specs=[t_spec(n) if tr else pl.BlockSpec((PTM, n), row) for n, tr in outs],
        out_shape=[jax.ShapeDtypeStruct((t // seq, n, seq) if tr else (t, n), BF16) for n, tr in outs],
        compiler_params=_cparams(("arbitrary",)),
        name="proj",
    )(x2, sc1, sh1, g_pre, w["w_in"], w["g_q"], w["wq"], w["g_kv"], w["wk"], w["wv"], w["v_one"],
      tabs["cq"], tabs["sq"], tabs["ck"], tabs["sk"])


def _mla_kernel(q_ref, k_ref, v_ref, o_ref):
    i = pl.program_id(2)
    heads = [slice(hh * HEAD_PAD, (hh + 1) * HEAD_PAD) for hh in range(MLA_HPS)]

    def step(off, width, carry, masked=False):
        if masked:
            kc = lax.broadcasted_iota(jnp.int32, (width, TQ), 0) // CHUNK
            qc = lax.broadcasted_iota(jnp.int32, (width, TQ), 1) // CHUNK
            visible = kc <= qc

        def score(hs):
            s = _nt_dot(k_ref[0, pl.ds(off, width), hs], q_ref[0, :, hs])
            return jnp.where(visible, s, NEG) if masked else s

        scores = [score(hs) for hs in heads[:MLA_LEAD]]
        new = []
        for hh, hs in enumerate(heads):
            m, acc = carry[hh]
            m_new = jnp.maximum(m, jnp.max(scores[hh], axis=0, keepdims=True))
            p = jnp.exp2(scores[hh] - m_new).astype(BF16)
            if hh + MLA_LEAD < MLA_HPS:
                scores.append(score(heads[hh + MLA_LEAD]))
            pv = jnp.dot(v_ref[0, hs, pl.ds(off, width)], p, preferred_element_type=F32)
            new.append((m_new, jnp.exp2(m - m_new) * acc + pv))
        return tuple(new)

    init = tuple((jnp.full((1, TQ), NEG, F32), jnp.zeros((HEAD_PAD, TQ), F32)) for _ in heads)
    carry = lax.fori_loop(0, i // 2, lambda j, c: step(pl.multiple_of(j * TKW, TKW), TKW, c), init)
    carry = lax.fori_loop(0, i % 2, lambda _, c: step(pl.multiple_of((i - 1) * TQ, TQ), TQ, c), carry)
    carry = step(pl.multiple_of(i * TQ, TQ), TQ, carry, True)
    for pp in range(MLA_HPS // 2):
        pair = [acc[:MLA_V] / acc[MLA_V:MLA_V + 1] for _, acc in (carry[2 * pp], carry[2 * pp + 1])]
        o_ref[0, :, pp * LANES:(pp + 1) * LANES] = jnp.concatenate(pair, axis=0).T.astype(BF16)


def _mla(q, k, v):
    bsz, seq, _ = q.shape
    groups = MLA_HEADS // MLA_HPS
    return pl.pallas_call(
        _mla_kernel,
        grid=(bsz, groups, seq // TQ),
        in_specs=[pl.BlockSpec((1, TQ, MLA_HPS * HEAD_PAD), lambda b, p, i: (b, i, p)),
                  pl.BlockSpec((1, seq, MLA_HPS * HEAD_PAD), lambda b, p, i: (b, 0, p)),
                  pl.BlockSpec((1, MLA_HPS * HEAD_PAD, seq), lambda b, p, i: (b, p, 0))],
        out_specs=pl.BlockSpec((1, TQ, MLA_HPS * MLA_V), lambda b, p, i: (b, i, p)),
        out_shape=jax.ShapeDtypeStruct((bsz, seq, MLA_HEADS * MLA_V), BF16),
        compiler_params=_cparams(("arbitrary", "arbitrary", "arbitrary")),
        name="mla",
    )(q, k, v)


def _chunk_kernel(q_ref, kp_ref, kc_ref, vp_ref, vc_ref, bias0_ref, bias1_ref, o_ref):
    lane = lax.broadcasted_iota(jnp.int32, (CQ, LANES), 1)
    lo = lane < CA_DIM
    row = lax.broadcasted_iota(jnp.int32, (LANES, CQ), 0)
    top = row < CA_DIM
    bias_refs = (bias0_ref, bias1_ref)
    subs = CSTEP // CQ

    def key_rows(j):
        return (kp_ref if j < subs else kc_ref), slice((j % subs) * CQ, (j % subs + 1) * CQ)

    def score(sub, head):
        sl = slice((head // 2) * LANES, (head // 2 + 1) * LANES)
        q = q_ref[0, sub * CQ:(sub + 1) * CQ, sl]
        qm = jnp.where(lo if head % 2 == 0 else jnp.logical_not(lo), q, jnp.zeros_like(q))
        parts = []
        for cb in range(3):
            ref, rows = key_rows(sub + cb)
            parts.append(_nt_dot(ref[0, rows, sl], qm))
        return jnp.concatenate(parts, axis=0)

    scores = {(sub, h): score(sub, h) for sub in range(subs) for h in range(CA_HEADS)}
    for sub in range(subs):
        for p in range(CA_HEADS // 2):
            sl = slice(p * LANES, (p + 1) * LANES)
            vts = []
            for cb in range(3):
                ref, cols = key_rows(sub + cb)
                vts.append((vp_ref if ref is kp_ref else vc_ref)[0, sl, cols])
            outs = []
            for hh in range(2):
                mine = top if hh == 0 else jnp.logical_not(top)
                den_row = CA_DIM if hh == 0 else 0
                ones_row = (row == den_row).astype(BF16)
                s = scores[(sub, 2 * p + hh)] + bias_refs[sub][0, 2 * p + hh]
                m = jnp.max(s, axis=0, keepdims=True)
                pb = jnp.exp2(s - m).astype(BF16)
                o = None
                for cb in range(3):
                    part = jnp.dot(jnp.where(mine, vts[cb], ones_row), pb[cb * CQ:(cb + 1) * CQ],
                                   preferred_element_type=F32)
                    o = part if o is None else o + part
                o = o / o[den_row:den_row + 1]
                outs.append(o[:CA_DIM] if hh == 0 else o[CA_DIM:])
            o_ref[0, sub * CQ:(sub + 1) * CQ, sl] = jnp.concatenate(outs, axis=0).T.astype(BF16)


def _chunk_attn(qc, kc, vct, bias):
    bsz, seq, w = qc.shape
    subs = CSTEP // CQ
    prev = lambda i: jnp.maximum(i - 1, 0)
    table = lambda s: pl.BlockSpec((1, CA_HEADS, CBAND, CQ), lambda b, i: (jnp.minimum(subs * i + s, 2), 0, 0, 0))
    return pl.pallas_call(
        _chunk_kernel,
        grid=(bsz, seq // CSTEP),
        in_specs=[pl.BlockSpec((1, CSTEP, w), lambda b, i: (b, i, 0)),
                  pl.BlockSpec((1, CSTEP, w), lambda b, i: (b, prev(i), 0)),
                  pl.BlockSpec((1, CSTEP, w), lambda b, i: (b, i, 0)),
                  pl.BlockSpec((1, w, CSTEP), lambda b, i: (b, 0, prev(i))),
                  pl.BlockSpec((1, w, CSTEP), lambda b, i: (b, 0, i)),
                  table(0), table(1)],
        out_specs=pl.BlockSpec((1, CSTEP, w), lambda b, i: (b, i, 0)),
        out_shape=jax.ShapeDtypeStruct((bsz, seq, w), BF16),
        compiler_params=_cparams(("arbitrary", "arbitrary")),
        name="chunk_attn",
    )(qc, kc, kc, vct, vct, bias, bias)


def _post_kernel(oa_ref, ob_ref, x_ref, gt_ref, sc_ref, sh_ref, gpost_ref, gpre_ref,
                 woa_ref, wob_ref, wr_ref, br_ref,
                 x1_out, h2_out, idx_out, gate_out, rank_out, cnt_out, carry_ref):
    t = pl.program_id(0)

    @pl.when(t == 0)
    def _():
        carry_ref[...] = jnp.zeros_like(carry_ref)

    os = []
    for rs in range(0, TM, POST_SUB):
        rows = slice(rs, rs + POST_SUB)
        o = jnp.dot(oa_ref[rows], woa_ref[...], preferred_element_type=F32)
        os.append(o + jnp.dot(ob_ref[rows], wob_ref[...], preferred_element_type=F32))
    h2s = []
    gain1 = gt_ref[0] * gpost_ref[...]
    gain2 = gpre_ref[...] * (1.0 + sc_ref[0])
    for sub, o in enumerate(os):
        rows = slice(sub * POST_SUB, (sub + 1) * POST_SUB)
        x1 = x_ref[rows] + _rms(o, gain1)
        x1_out[rows] = x1
        h2f = _rms(x1, gain2) + sh_ref[0]
        h2_out[rows] = _pack_rows(h2f)
        h2s.append(h2f.astype(BF16))
    h2 = jnp.concatenate(h2s, axis=0)

    logits = _nt_dot(wr_ref[...], h2) + br_ref[...]
    eid = lax.broadcasted_iota(jnp.int32, (N_EXPERTS, TM), 0)
    vals, idxs = [], []
    work = logits
    for _k in range(TOP_K):
        m = jnp.max(work, axis=0, keepdims=True)
        ix = jnp.min(jnp.where(work == m, eid, N_EXPERTS), axis=0, keepdims=True)
        work = jnp.where(eid == ix, -jnp.inf, work)
        vals.append(m)
        idxs.append(ix)
    es = [jnp.exp(v - vals[0]) for v in vals]
    den = es[0] + es[1] + es[2] + es[3]
    gate_out[...] = jnp.concatenate([e / den for e in es], axis=0)
    idx_out[...] = jnp.concatenate(idxs, axis=0)

    sel = (eid == idxs[0]) | (eid == idxs[1]) | (eid == idxs[2]) | (eid == idxs[3])
    self32 = sel.astype(F32)
    rr = lax.broadcasted_iota(jnp.int32, (TM, TM), 0)
    cc = lax.broadcasted_iota(jnp.int32, (TM, TM), 1)
    upper = (rr < cc).astype(BF16)
    before = jnp.dot(self32.astype(BF16), upper, preferred_element_type=F32)
    before = before + carry_ref[:, 0:1]
    ranks = [jnp.sum(jnp.where(eid == ix, before, 0.0), axis=0, keepdims=True) for ix in idxs]
    rank_out[...] = jnp.concatenate(ranks, axis=0).astype(jnp.int32)
    carry_ref[...] = carry_ref[...] + jnp.sum(self32, axis=1, keepdims=True)
    cnt_out[...] = carry_ref[...]


def _post(oa, ob, x2, gt1, sc2, sh2, g_post, g_pre, w, seq, tile_off, t):
    d = x2.shape[1]
    tpb = seq // TM
    row = lambda i: (i, 0)
    src = lambda i: (i + tile_off, 0)
    col = lambda i: (0, i)
    full = lambda i: (0, 0)
    per_b = lambda i: ((i + tile_off) // tpb, 0, 0)
    hw = oa.shape[1]
    in_specs = [
        pl.BlockSpec((TM, hw), row), pl.BlockSpec((TM, hw), row), pl.BlockSpec((TM, d), src),
        pl.BlockSpec((1, 1, d), per_b), pl.BlockSpec((1, 1, d), per_b), pl.BlockSpec((1, 1, d), per_b),
        pl.BlockSpec((1, d), full), pl.BlockSpec((1, d), full),
        pl.BlockSpec((hw, d), full), pl.BlockSpec((hw, d), full),
        pl.BlockSpec((N_EXPERTS, d), full), pl.BlockSpec((N_EXPERTS, 1), full),
    ]
    out_specs = [
        pl.BlockSpec((TM, d), row), pl.BlockSpec((TM, d // 2), row),
        pl.BlockSpec((TOP_K, TM), col), pl.BlockSpec((TOP_K, TM), col), pl.BlockSpec((TOP_K, TM), col),
        pl.BlockSpec((N_EXPERTS, LANES), full),
    ]
    out_shape = [
        jax.ShapeDtypeStruct((t, d), F32), jax.ShapeDtypeStruct((t, d // 2), U32),
        jax.ShapeDtypeStruct((TOP_K, t), jnp.int32), jax.ShapeDtypeStruct((TOP_K, t), F32),
        jax.ShapeDtypeStruct((TOP_K, t), jnp.int32),
        jax.ShapeDtypeStruct((N_EXPERTS, LANES), F32),
    ]
    return pl.pallas_call(
        _post_kernel,
        grid=(t // TM,),
        in_specs=in_specs,
        out_specs=out_specs,
        out_shape=out_shape,
        scratch_shapes=[pltpu.VMEM((N_EXPERTS, LANES), F32)],
        compiler_params=_cparams(("arbitrary",)),
        name="post",
    )(oa, ob, x2, gt1, sc2, sh2, g_post, g_pre, w["wo_a"], w["wo_b"], w["wr_t"], w["b_r"])


def _expert_kernel(be_ref, nu_ref, x_ref, wgu_ref, bgu_ref, wd_ref, bd_ref, y_ref, wgu_bf, wd_bf):
    j = pl.program_id(0)
    used = j < nu_ref[0]

    @pl.when(used & ((j == 0) | (be_ref[j] != be_ref[jnp.maximum(j - 1, 0)])))
    def _():
        wgu_bf[...] = wgu_ref[0].astype(BF16)
        wd_bf[...] = wd_ref[0].astype(BF16)

    @pl.when(used)
    def _():
        gus = []
        for rs in range(0, ROWS, EXPERT_SUB):
            x_lo, x_hi = _unpack_rows(x_ref[rs:rs + EXPERT_SUB])
            half = x_lo.shape[1]
            gu = jnp.dot(x_lo.astype(BF16), wgu_bf[:half], preferred_element_type=F32)
            gu += jnp.dot(x_hi.astype(BF16), wgu_bf[half:], preferred_element_type=F32)
            gus.append(gu + bgu_ref[0])
        for sub, gu in enumerate(gus):
            rows = slice(sub * EXPERT_SUB, (sub + 1) * EXPERT_SUB)
            gate = jnp.minimum(gu[:, :D_EXPERT], SWIGLU_LIMIT)
            up = jnp.clip(gu[:, D_EXPERT:], -SWIGLU_LIMIT, SWIGLU_LIMIT)
            glu = gate / (1.0 + jnp.exp(-SWIGLU_ALPHA * gate))
            act = ((up + 1.0) * glu).astype(BF16)
            y = jnp.dot(act, wd_bf[...], preferred_element_type=F32) + bd_ref[0]
            y_ref[rows] = _pack_rows(y)

    @pl.when(j >= nu_ref[0])
    def _():
        y_ref[...] = jnp.zeros_like(y_ref)


def _experts(xin, block_exp, n_used, wgu, bgu, wd, bd):
    p_rows, dw = xin.shape
    d = 2 * dw
    nb = p_rows // ROWS
    f2 = wgu.shape[2]
    grid_spec = pltpu.PrefetchScalarGridSpec(
        num_scalar_prefetch=2,
        grid=(nb,),
        in_specs=[
            pl.BlockSpec((ROWS, dw), lambda j, be, nu: (jnp.minimum(j, nu[0] - 1), 0)),
            pl.BlockSpec((1, d, f2), lambda j, be, nu: (be[j], 0, 0)),
            pl.BlockSpec((1, 1, f2), lambda j, be, nu: (be[j], 0, 0)),
            pl.BlockSpec((1, f2 // 2, d), lambda j, be, nu: (be[j], 0, 0)),
            pl.BlockSpec((1, 1, d), lambda j, be, nu: (be[j], 0, 0)),
        ],
        out_specs=pl.BlockSpec((ROWS, dw), lambda j, be, nu: (j, 0)),
        scratch_shapes=[pltpu.VMEM((d, f2), BF16), pltpu.VMEM((f2 // 2, d), BF16)],
    )
    return pl.pallas_call(
        _expert_kernel,
        grid_spec=grid_spec,
        out_shape=jax.ShapeDtypeStruct((p_rows, dw), U32),
        compiler_params=_cparams(("arbitrary",)),
        name="experts",
    )(block_exp, n_used, xin, wgu, bgu, wd, bd)


def _final_kernel(yg_ref, g_ref, x1_ref, gt_ref, gpost_ref, *rest):
    o_ref = rest[-1]
    g = g_ref[...]
    f_lo, f_hi = None, None
    for k in range(TOP_K):
        lo, hi = _unpack_rows(yg_ref[k])
        gk = g[:, k:k + 1]
        f_lo = lo * gk if f_lo is None else f_lo + lo * gk
        f_hi = hi * gk if f_hi is None else f_hi + hi * gk
    f = jnp.concatenate([f_lo, f_hi], axis=1)
    o_ref[...] = x1_ref[...] + gt_ref[0] * _rms(f, gpost_ref[...])


def _final(yg, gates_t, x1, gt2, g_post, seq, tile_off, t_all, prev_out):
    t, d = x1.shape
    tpb = seq // TM
    row = lambda i: (i, 0)
    in_specs = [pl.BlockSpec((TOP_K, TM, d // 2), lambda i: (0, i, 0)),
                pl.BlockSpec((TM, TOP_K), row),
                pl.BlockSpec((TM, d), row),
                pl.BlockSpec((1, 1, d), lambda i: ((i + tile_off) // tpb, 0, 0)),
                pl.BlockSpec((1, d), lambda i: (0, 0))]
    args = [yg, gates_t, x1, gt2, g_post]
    aliases = {}
    if prev_out is not None:
        in_specs.append(pl.BlockSpec(memory_space=pl.ANY))
        args.append(prev_out)
        aliases = {len(args) - 1: 0}
    return pl.pallas_call(
        _final_kernel,
        grid=(t // TM,),
        in_specs=in_specs,
        out_specs=pl.BlockSpec((TM, d), lambda i: (i + tile_off, 0)),
        out_shape=jax.ShapeDtypeStruct((t_all, d), F32),
        input_output_aliases=aliases,
        compiler_params=_cparams(("arbitrary",)),
        name="final",
    )(*args)


def _sc_mesh():
    return plsc.VectorSubcoreMesh(core_axis_name="c", subcore_axis_name="s")


def _sc_worker():
    return lax.axis_index("s") * SC_CORES + lax.axis_index("c")


def _dispatch(h2, dest, p_rows):
    t, dw = h2.shape
    per_w = t // SC_WORKERS
    n_win = per_w // SC_WIN

    @functools.partial(
        pl.kernel, mesh=_sc_mesh(),
        out_type=jax.ShapeDtypeStruct((p_rows, dw), h2.dtype),
        scratch_types=[pltpu.VMEM((TOP_K, SC_WIN), jnp.int32),
                       pltpu.VMEM((SC_WIN, dw), h2.dtype),
                       pltpu.SemaphoreType.DMA],
        name="dispatch",
    )
    def run(h_hbm, d_hbm, o_hbm, idx_v, rows_v, sem):
        wid = _sc_worker()

        @pl.loop(0, n_win)
        def _(wi):
            base = pl.multiple_of(wid * per_w + wi * SC_WIN, SC_WIN)
            for k in range(TOP_K):
                pltpu.sync_copy(d_hbm.at[pl.ds(k * t + base, SC_WIN)], idx_v.at[k])
            pltpu.sync_copy(h_hbm.at[pl.ds(base, SC_WIN)], rows_v)
            for k in range(TOP_K):
                pltpu.async_copy(rows_v, o_hbm.at[idx_v.at[k]], sem).wait()

    return run(h2, dest)


def _gather_rows(y, dest, t):
    _, dw = y.shape
    per_w = t // SC_WORKERS
    n_win = per_w // SC_WIN

    @functools.partial(
        pl.kernel, mesh=_sc_mesh(),
        out_type=jax.ShapeDtypeStruct((TOP_K, t, dw), y.dtype),
        scratch_types=[pltpu.VMEM((SC_WIN,), jnp.int32),
                       pltpu.VMEM((SC_WIN, dw), y.dtype),
                       pltpu.SemaphoreType.DMA],
        name="gather_rows",
    )
    def run(y_hbm, d_hbm, o_hbm, idx_v, rows_v, sem):
        wid = _sc_worker()

        @pl.loop(0, n_win)
        def _(wi):
            base = pl.multiple_of(wid * per_w + wi * SC_WIN, SC_WIN)
            for k in range(TOP_K):
                pltpu.sync_copy(d_hbm.at[pl.ds(k * t + base, SC_WIN)], idx_v)
                pltpu.async_copy(y_hbm.at[idx_v], rows_v, sem).wait()
                pltpu.sync_copy(rows_v, o_hbm.at[k, pl.ds(base, SC_WIN)])

    return run(y, dest)


def _prep_weights(w_in, g_q, w_qb, g_kv, w_kvb, w_o, w_router, b_router):
    d = w_in.shape[0]
    o = 0
    w_cq = w_in[:, o:o + Q_LORA]; o += Q_LORA
    w_ckv = w_in[:, o:o + KV_LORA]; o += KV_LORA
    w_kr = w_in[:, o:o + MLA_ROPE]; o += MLA_ROPE
    w_ca = w_in[:, o:]
    half = MLA_ROPE // 2
    zpad = lambda n: jnp.zeros((d, n), w_in.dtype)
    tail = HEAD_PAD - MLA_NOPE - MLA_ROPE
    w_kr_pad = jnp.concatenate([zpad(MLA_NOPE), w_kr, zpad(tail)], axis=1)
    w_in_all = jnp.concatenate([w_cq, w_ckv, w_kr_pad, w_ca], axis=1).astype(BF16)

    wq = w_qb.reshape(Q_LORA, MLA_HEADS, MLA_NOPE + MLA_ROPE)
    zq = jnp.zeros((Q_LORA, MLA_HEADS, tail), w_qb.dtype)
    wq_pad = jnp.concatenate([wq, zq], axis=-1).reshape(Q_LORA, -1).astype(BF16)

    wkv = w_kvb.reshape(KV_LORA, MLA_HEADS, MLA_NOPE + MLA_V)
    kn = wkv[..., :MLA_NOPE]
    wk = jnp.concatenate([kn, jnp.zeros((KV_LORA, MLA_HEADS, HEAD_PAD - MLA_NOPE), w_kvb.dtype)],
                         axis=-1).reshape(KV_LORA, -1).astype(BF16)
    wv = jnp.concatenate([wkv[..., MLA_NOPE:], jnp.zeros((KV_LORA, MLA_HEADS, HEAD_PAD - MLA_V), w_kvb.dtype)],
                         axis=-1).reshape(KV_LORA, -1).T.astype(BF16)
    v_one = jnp.tile((jnp.arange(HEAD_PAD) == MLA_V).astype(F32), MLA_HEADS).reshape(-1, 1)
    mla_w = MLA_HEADS * MLA_V
    return {
        "w_in": w_in_all, "g_q": g_q.reshape(1, -1), "wq": wq_pad,
        "g_kv": g_kv.reshape(1, -1), "wk": wk, "wv": wv, "v_one": v_one,
        "wo_a": w_o[:mla_w].astype(BF16), "wo_b": w_o[mla_w:].astype(BF16),
        "wr_t": w_router.T.astype(BF16), "b_r": b_router.reshape(-1, 1),
    }


def _rope_tables(seq):
    half = MLA_ROPE // 2
    inv_freq = ROPE_THETA ** (-jnp.arange(half, dtype=F32) / half)
    ang = jnp.arange(seq, dtype=F32)[:, None] * inv_freq[None, :]
    cos, sin = jnp.cos(ang), jnp.sin(ang)
    tail = HEAD_PAD - MLA_NOPE - MLA_ROPE
    ones = jnp.ones((seq, MLA_NOPE), F32)
    zn = jnp.zeros((seq, MLA_NOPE), F32)
    zt = jnp.zeros((seq, tail), F32)
    qs = (MLA_NOPE + MLA_ROPE) ** -0.5 * LOG2E
    return {
        "cq": jnp.concatenate([ones, cos, cos, zt], axis=1) * qs,
        "sq": jnp.concatenate([zn, -sin, sin, zt], axis=1) * qs,
        "ck": jnp.concatenate([zn, cos, cos, zt], axis=1),
        "sk": jnp.concatenate([zn, -sin, sin, zt], axis=1),
    }


def _bias_table(rel_bias):
    n = CQ + CBAND - 1
    rel = (CBAND - 1) - jnp.arange(n)
    diag = rel_bias.astype(F32)[:, jnp.clip(rel, -(CHUNK - 1), REL_MAX) + (CHUNK - 1)] * LOG2E
    diag = jnp.concatenate([diag, jnp.zeros((diag.shape[0], 1), F32)], axis=1)
    b = jnp.tile(diag, (1, CQ))[:, :CQ * n].reshape(-1, CQ, n)[:, :, CQ - 1:]
    r = jnp.arange(CQ)[:, None]
    c = jnp.arange(CBAND)[None, :]
    dchunk = r // CHUNK - (c // CHUNK - CA_LEFT)
    visible = (dchunk >= 0) & (dchunk <= CA_LEFT)
    exists = (c // CQ)[None] >= (2 - jnp.arange(3))[:, None, None]
    return jnp.where((visible[None] & exists)[:, None], b[None], NEG)


def _layer(x, c, w_ada, b_ada, g_pre_mix, g_post_mix, g_pre_ffn, g_post_ffn, w_in, g_q, w_qb,
           g_kv, w_kvb, rel_bias, w_o, w_router, b_router, w_gu, b_gu, w_down, b_down):
    bsz, seq, d = x.shape
    t = bsz * seq
    mod = _ada(c, w_ada, b_ada).reshape(bsz, 6, 1, d)
    sh1, sc1, gt1, sh2, sc2, gt2 = [mod[:, k] for k in range(6)]
    w = _prep_weights(w_in, g_q, w_qb, g_kv, w_kvb, w_o, w_router, b_router)
    tabs = _rope_tables(seq)
    x2 = x.reshape(t, d)

    bp = bsz // MOE_PARTS
    tp = bp * seq
    bias_t = jnp.swapaxes(_bias_table(rel_bias), 2, 3)
    eids = jnp.arange(N_EXPERTS, dtype=jnp.int32)[:, None, None]
    p_rows = tp * TOP_K + N_EXPERTS * ROWS
    block_start = jnp.arange(p_rows // ROWS, dtype=jnp.int32) * ROWS
    shp = lambda a: a.reshape(bp, seq, a.shape[-1])
    routed = []
    for part in range(MOE_PARTS):
        q, k, v, qc, kc, vc = _proj(x2, sc1, sh1, g_pre_mix.reshape(1, d), w, tabs, seq, part * bp, tp)
        oa = _mla(shp(q), shp(k), v).reshape(tp, -1)
        ob = _chunk_attn(shp(qc), shp(kc), vc, bias_t).reshape(tp, -1)
        x1, h2, top_idx, gates, rank, cnt = _post(oa, ob, x2, gt1, sc2, sh2, g_post_mix.reshape(1, d),
                                                  g_pre_ffn.reshape(1, d), w, seq, part * (tp // TM), tp)
        counts = cnt[:, 0].astype(jnp.int32)
        padded = ((counts + ROWS - 1) // ROWS) * ROWS
        pend = jnp.cumsum(padded)
        pstart = pend - padded
        dest = (jnp.sum(jnp.where(top_idx[None] == eids, pstart[:, None, None], 0), axis=0) + rank).reshape(-1)
        block_exp = jnp.minimum(jnp.sum(pend[None, :] <= block_start[:, None], axis=1),
                                N_EXPERTS - 1).astype(jnp.int32)
        n_used = (pend[-1:] // ROWS).astype(jnp.int32)
        routed.append((x1, gates.T, dest, block_exp, n_used, _dispatch(h2, dest, p_rows)))

    out = None
    for part, (x1, gates_t, dest, block_exp, n_used, xin) in enumerate(routed):
        y = _experts(xin, block_exp, n_used, w_gu, b_gu.reshape(N_EXPERTS, 1, -1),
                     w_down, b_down.reshape(N_EXPERTS, 1, -1))
        yg = _gather_rows(y, dest, tp)
        out = _final(yg, gates_t, x1, gt2, g_post_ffn.reshape(1, d), seq, part * (tp // TM), t, out)
    return out.reshape(bsz, seq, d)


def kernel(x, c, w_ada, b_ada, g_pre_mix, g_post_mix, g_pre_ffn, g_post_ffn, w_in, g_q, w_qb,
           g_kv, w_kvb, rel_bias, w_o, w_router, b_router, w_gu, b_gu, w_down, b_down):
    for l in range(w_ada.shape[0]):
        x = _layer(x, c, w_ada[l], b_ada[l], g_pre_mix[l], g_post_mix[l], g_pre_ffn[l], g_post_ffn[l],
                   w_in[l], g_q[l], w_qb[l], g_kv[l], w_kvb[l], rel_bias[l], w_o[l], w_router[l],
                   b_router[l], w_gu[l], b_gu[l], w_down[l], b_down[l])
    return x
```

```python
import functools
import math

import jax
import jax.numpy as jnp
from jax import lax
from jax.experimental import pallas as pl
from jax.experimental.pallas import tpu as pltpu
from jax.experimental.pallas import tpu_sc as plsc

F32 = jnp.float32
BF16 = jnp.bfloat16
U32 = jnp.uint32

D_MODEL = 1024
CHUNK = 64
EPS = 1e-6
MLA_HEADS = 8
MLA_NOPE = 64
MLA_ROPE = 32
MLA_V = 64
Q_LORA = 256
KV_LORA = 128
ROPE_THETA = 10000.0
CA_HEADS = 8
CA_DIM = 64
CA_LEFT = 8
REL_MAX = 256
N_EXPERTS = 32
TOP_K = 4
D_EXPERT = 1024
SWIGLU_LIMIT = 7.0
SWIGLU_ALPHA = 1.702

LANES = 128
HEAD_PAD = 128
LOG2E = math.log2(math.e)
NEG = -1e30
VMEM_LIMIT = 56 * 1024 * 1024

TM = 1024
POST_SUB = 128
PTM = 1024
PROJ_SUB = 1024
TQ = 512
TKW = 2 * TQ
MLA_HPS = 4
MLA_LEAD = 2
CQ = 256
CBAND = 3 * CQ
CSTEP = 2 * CQ
ROWS = 512
EXPERT_SUB = 256
MOE_PARTS = 2
SC_CORES = 2
SC_WORKERS = SC_CORES * 16
SC_WIN = 128
CA_W = CA_HEADS * CA_DIM
W_IN_COLS = Q_LORA + KV_LORA + LANES + 3 * CA_W


def _cparams(sem, flags=None):
    return pltpu.CompilerParams(dimension_semantics=sem, vmem_limit_bytes=VMEM_LIMIT, flags=flags)


def _nt_dot(a, b):
    return lax.dot_general(a, b, (((1,), (1,)), ((), ())), preferred_element_type=F32)


def _rms(x, g):
    return x * lax.rsqrt(jnp.mean(x * x, axis=-1, keepdims=True) + EPS) * g


def _pack_rows(x):
    n = x.shape[1] // 2
    lo = lax.bitcast_convert_type(x[:, :n].astype(BF16).astype(F32), U32)
    hi = lax.bitcast_convert_type(x[:, n:].astype(BF16).astype(F32), U32)
    return (lo >> 16) | hi


def _unpack_rows(p):
    lo = lax.bitcast_convert_type(p << 16, F32)
    hi = lax.bitcast_convert_type(p & jnp.uint32(0xFFFF0000), F32)
    return lo, hi


def _ada_kernel(c_ref, w_ref, b_ref, o_ref):
    c = c_ref[...]
    a = (c / (1.0 + jnp.exp(-c))).astype(BF16)
    o_ref[...] = jnp.dot(a, w_ref[...].astype(BF16), preferred_element_type=F32) + b_ref[...]


def _ada(c, w, b):
    bsz, d = c.shape
    n = w.shape[1]
    tn = 1024
    return pl.pallas_call(
        _ada_kernel,
        grid=(n // tn,),
        in_specs=[pl.BlockSpec((bsz, d), lambda j: (0, 0)),
                  pl.BlockSpec((d, tn), lambda j: (0, j)),
                  pl.BlockSpec((1, tn), lambda j: (0, j))],
        out_specs=pl.BlockSpec((bsz, tn), lambda j: (0, j)),
        out_shape=jax.ShapeDtypeStruct((bsz, n), F32),
        compiler_params=_cparams(("arbitrary",)),
        name="ada",
    )(c, w, b.reshape(1, n))


def _swap_halves(x):
    width = x.shape[1]
    half = MLA_ROPE // 2
    lane = lax.broadcasted_iota(jnp.int32, x.shape, 1) % HEAD_PAD
    return jnp.where(lane < MLA_NOPE + half, pltpu.roll(x, width - half, axis=1), pltpu.roll(x, half, axis=1))


def _proj_kernel(x_ref, sc_ref, sh_ref, g_ref, win_ref, gq_ref, wq_ref, gkv_ref,
                 wk_ref, wv_ref, vone_ref, cq_ref, sq_ref, ck_ref, sk_ref,
                 q_out, k_out, v_out, qc_out, kc_out, vc_out):
    zs = []
    for rs in range(0, PTM, PROJ_SUB):
        h = _rms(x_ref[rs:rs + PROJ_SUB], g_ref[...]) * (1.0 + sc_ref[0]) + sh_ref[0]
        zs.append(jnp.dot(h.astype(BF16), win_ref[...], preferred_element_type=F32))
    for sub, z in enumerate(zs):
        rows = slice(sub * PROJ_SUB, (sub + 1) * PROJ_SUB)
        o = 0
        cq = z[:, o:o + Q_LORA]; o += Q_LORA
        ckv = z[:, o:o + KV_LORA]; o += KV_LORA
        kr = z[:, o:o + LANES]; o += LANES
        qc = z[:, o:o + CA_W]; o += CA_W
        kc = z[:, o:o + CA_W]; o += CA_W
        vc = z[:, o:o + CA_W]

        cqn = _rms(cq, gq_ref[...]).astype(BF16)
        q = jnp.dot(cqn, wq_ref[...], preferred_element_type=F32)
        ckvn = _rms(ckv, gkv_ref[...]).astype(BF16)
        kn = jnp.dot(ckvn, wk_ref[...], preferred_element_type=F32)
        v_out[0, :, rows] = (_nt_dot(wv_ref[...], ckvn) + vone_ref[...]).astype(BF16)

        q_sw = _swap_halves(q)
        cq_t, sq_t = cq_ref[rows], sq_ref[rows]
        krope = kr * ck_ref[rows] + _swap_halves(kr) * sk_ref[rows]
        for hd in range(MLA_HEADS):
            sl = slice(hd * HEAD_PAD, (hd + 1) * HEAD_PAD)
            q_out[rows, sl] = (q[:, sl] * cq_t + q_sw[:, sl] * sq_t).astype(BF16)
            k_out[rows, sl] = (kn[:, sl] + krope).astype(BF16)

        qc_out[rows] = (qc * (CA_DIM ** -0.5 * LOG2E)).astype(BF16)
        kc_out[rows] = kc.astype(BF16)
        vc_out[0, :, rows] = vc.T.astype(BF16)


def _proj(x2, sc1, sh1, g_pre, w, tabs, seq, batch_off, t):
    d = x2.shape[1]
    tpb = seq // PTM
    row = lambda i: (i, 0)
    full = lambda i: (0, 0)
    per_b = lambda i: (i // tpb + batch_off, 0, 0)
    pos = lambda i: (i % tpb, 0)
    hw = MLA_HEADS * HEAD_PAD
    in_specs = [
        pl.BlockSpec((PTM, d), lambda i: (i + batch_off * tpb, 0)),
        pl.BlockSpec((1, 1, d), per_b), pl.BlockSpec((1, 1, d), per_b),
        pl.BlockSpec((1, d), full),
        pl.BlockSpec((d, W_IN_COLS), full),
        pl.BlockSpec((1, Q_LORA), full),
        pl.BlockSpec((Q_LORA, hw), full),
        pl.BlockSpec((1, KV_LORA), full),
        pl.BlockSpec((KV_LORA, hw), full), pl.BlockSpec((hw, KV_LORA), full), pl.BlockSpec((hw, 1), full),
        pl.BlockSpec((PTM, LANES), pos), pl.BlockSpec((PTM, LANES), pos),
        pl.BlockSpec((PTM, LANES), pos), pl.BlockSpec((PTM, LANES), pos),
    ]
    outs = ((hw, False), (hw, False), (hw, True), (CA_W, False), (CA_W, False), (CA_W, True))
    t_spec = lambda n: pl.BlockSpec((1, n, PTM), lambda i: (i // tpb, 0, i % tpb))
    return pl.pallas_call(
        _proj_kernel,
        grid=(t // PTM,),
        in_specs=in_specs,
        out_specs=[t_spec(n) if tr else pl.BlockSpec((PTM, n), row) for n, tr in outs],
        out_shape=[jax.ShapeDtypeStruct((t // seq, n, seq) if tr else (t, n), BF16) for n, tr in outs],
        compiler_params=_cparams(("arbitrary",)),
        name="proj",
    )(x2, sc1, sh1, g_pre, w["w_in"], w["g_q"], w["wq"], w["g_kv"], w["wk"], w["wv"], w["v_one"],
      tabs["cq"], tabs["sq"], tabs["ck"], tabs["sk"])


def _mla_kernel(q_ref, k_ref, v_ref, o_ref):
    i = pl.program_id(2)
    heads = [slice(hh * HEAD_PAD, (hh + 1) * HEAD_PAD) for hh in range(MLA_HPS)]

    def step(off, width, carry, masked=False):
        if masked:
            kc = lax.broadcasted_iota(jnp.int32, (width, TQ), 0) // CHUNK
            qc = lax.broadcasted_iota(jnp.int32, (width, TQ), 1) // CHUNK
            visible = kc <= qc

        def score(hs):
            s = _nt_dot(k_ref[0, pl.ds(off, width), hs], q_ref[0, :, hs])
            return jnp.where(visible, s, NEG) if masked else s

        scores = [score(hs) for hs in heads[:MLA_LEAD]]
        new = []
        for hh, hs in enumerate(heads):
            m, acc = carry[hh]
            m_new = jnp.maximum(m, jnp.max(scores[hh], axis=0, keepdims=True))
            p = jnp.exp2(scores[hh] - m_new).astype(BF16)
            if hh + MLA_LEAD < MLA_HPS:
                scores.append(score(heads[hh + MLA_LEAD]))
            pv = jnp.dot(v_ref[0, hs, pl.ds(off, width)], p, preferred_element_type=F32)
            new.append((m_new, jnp.exp2(m - m_new) * acc + pv))
        return tuple(new)

    init = tuple((jnp.full((1, TQ), NEG, F32), jnp.zeros((HEAD_PAD, TQ), F32)) for _ in heads)
    carry = lax.fori_loop(0, i // 2, lambda j, c: step(pl.multiple_of(j * TKW, TKW), TKW, c), init)
    carry = lax.fori_loop(0, i % 2, lambda _, c: step(pl.multiple_of((i - 1) * TQ, TQ), TQ, c), carry)
    carry = step(pl.multiple_of(i * TQ, TQ), TQ, carry, True)
    for pp in range(MLA_HPS // 2):
        pair = [acc[:MLA_V] / acc[MLA_V:MLA_V + 1] for _, acc in (carry[2 * pp], carry[2 * pp + 1])]
        o_ref[0, :, pp * LANES:(pp + 1) * LANES] = jnp.concatenate(pair, axis=0).T.astype(BF16)


def _mla(q, k, v):
    bsz, seq, _ = q.shape
    groups = MLA_HEADS // MLA_HPS
    return pl.pallas_call(
        _mla_kernel,
        grid=(bsz, groups, seq // TQ),
        in_specs=[pl.BlockSpec((1, TQ, MLA_HPS * HEAD_PAD), lambda b, p, i: (b, i, p)),
                  pl.BlockSpec((1, seq, MLA_HPS * HEAD_PAD), lambda b, p, i: (b, 0, p)),
                  pl.BlockSpec((1, MLA_HPS * HEAD_PAD, seq), lambda b, p, i: (b, p, 0))],
        out_specs=pl.BlockSpec((1, TQ, MLA_HPS * MLA_V), lambda b, p, i: (b, i, p)),
        out_shape=jax.ShapeDtypeStruct((bsz, seq, MLA_HEADS * MLA_V), BF16),
        compiler_params=_cparams(("arbitrary", "arbitrary", "arbitrary")),
        name="mla",
    )(q, k, v)


def _chunk_kernel(q_ref, kp_ref, kc_ref, vp_ref, vc_ref, bias0_ref, bias1_ref, o_ref):
    lane = lax.broadcasted_iota(jnp.int32, (CQ, LANES), 1)
    lo = lane < CA_DIM
    row = lax.broadcasted_iota(jnp.int32, (LANES, CQ), 0)
    top = row < CA_DIM
    bias_refs = (bias0_ref, bias1_ref)
    subs = CSTEP // CQ

    def key_rows(j):
        return (kp_ref if j < subs else kc_ref), slice((j % subs) * CQ, (j % subs + 1) * CQ)

    def score(sub, head):
        sl = slice((head // 2) * LANES, (head // 2 + 1) * LANES)
        q = q_ref[0, sub * CQ:(sub + 1) * CQ, sl]
        qm = jnp.where(lo if head % 2 == 0 else jnp.logical_not(lo), q, jnp.zeros_like(q))
        parts = []
        for cb in range(3):
            ref, rows = key_rows(sub + cb)
            parts.append(_nt_dot(ref[0, rows, sl], qm))
        return jnp.concatenate(parts, axis=0)

    scores = {(sub, h): score(sub, h) for sub in range(subs) for h in range(CA_HEADS)}
    for sub in range(subs):
        for p in range(CA_HEADS // 2):
            sl = slice(p * LANES, (p + 1) * LANES)
            vts = []
            for cb in range(3):
                ref, cols = key_rows(sub + cb)
                vts.append((vp_ref if ref is kp_ref else vc_ref)[0, sl, cols])
            outs = []
            for hh in range(2):
                mine = top if hh == 0 else jnp.logical_not(top)
                den_row = CA_DIM if hh == 0 else 0
                ones_row = (row == den_row).astype(BF16)
                s = scores[(sub, 2 * p + hh)] + bias_refs[sub][0, 2 * p + hh]
                m = jnp.max(s, axis=0, keepdims=True)
                pb = jnp.exp2(s - m).astype(BF16)
                o = None
                for cb in range(3):
                    part = jnp.dot(jnp.where(mine, vts[cb], ones_row), pb[cb * CQ:(cb + 1) * CQ],
                                   preferred_element_type=F32)
                    o = part if o is None else o + part
                o = o / o[den_row:den_row + 1]
                outs.append(o[:CA_DIM] if hh == 0 else o[CA_DIM:])
            o_ref[0, sub * CQ:(sub + 1) * CQ, sl] = jnp.concatenate(outs, axis=0).T.astype(BF16)


def _chunk_attn(qc, kc, vct, bias):
    bsz, seq, w = qc.shape
    subs = CSTEP // CQ
    prev = lambda i: jnp.maximum(i - 1, 0)
    table = lambda s: pl.BlockSpec((1, CA_HEADS, CBAND, CQ), lambda b, i: (jnp.minimum(subs * i + s, 2), 0, 0, 0))
    return pl.pallas_call(
        _chunk_kernel,
        grid=(bsz, seq // CSTEP),
        in_specs=[pl.BlockSpec((1, CSTEP, w), lambda b, i: (b, i, 0)),
                  pl.BlockSpec((1, CSTEP, w), lambda b, i: (b, prev(i), 0)),
                  pl.BlockSpec((1, CSTEP, w), lambda b, i: (b, i, 0)),
                  pl.BlockSpec((1, w, CSTEP), lambda b, i: (b, 0, prev(i))),
                  pl.BlockSpec((1, w, CSTEP), lambda b, i: (b, 0, i)),
                  table(0), table(1)],
        out_specs=pl.BlockSpec((1, CSTEP, w), lambda b, i: (b, i, 0)),
        out_shape=jax.ShapeDtypeStruct((bsz, seq, w), BF16),
        compiler_params=_cparams(("arbitrary", "arbitrary")),
        name="chunk_attn",
    )(qc, kc, kc, vct, vct, bias, bias)


def _post_kernel(oa_ref, ob_ref, x_ref, gt_ref, sc_ref, sh_ref, gpost_ref, gpre_ref,
                 woa_ref, wob_ref, wr_ref, br_ref,
                 x1_out, h2_out, idx_out, gate_out, rank_out, cnt_out, carry_ref):
    t = pl.program_id(0)

    @pl.when(t == 0)
    def _():
        carry_ref[...] = jnp.zeros_like(carry_ref)

    os = []
    for rs in range(0, TM, POST_SUB):
        rows = slice(rs, rs + POST_SUB)
        o = jnp.dot(oa_ref[rows], woa_ref[...], preferred_element_type=F32)
        os.append(o + jnp.dot(ob_ref[rows], wob_ref[...], preferred_element_type=F32))
    h2s = []
    gain1 = gt_ref[0] * gpost_ref[...]
    gain2 = gpre_ref[...] * (1.0 + sc_ref[0])
    for sub, o in enumerate(os):
        rows = slice(sub * POST_SUB, (sub + 1) * POST_SUB)
        x1 = x_ref[rows] + _rms(o, gain1)
        x1_out[rows] = x1
        h2f = _rms(x1, gain2) + sh_ref[0]
        h2_out[rows] = _pack_rows(h2f)
        h2s.append(h2f.astype(BF16))
    h2 = jnp.concatenate(h2s, axis=0)

    logits = _nt_dot(wr_ref[...], h2) + br_ref[...]
    eid = lax.broadcasted_iota(jnp.int32, (N_EXPERTS, TM), 0)
    vals, idxs = [], []
    work = logits
    for _k in range(TOP_K):
        m = jnp.max(work, axis=0, keepdims=True)
        ix = jnp.min(jnp.where(work == m, eid, N_EXPERTS), axis=0, keepdims=True)
        work = jnp.where(eid == ix, -jnp.inf, work)
        vals.append(m)
        idxs.append(ix)
    es = [jnp.exp(v - vals[0]) for v in vals]
    den = es[0] + es[1] + es[2] + es[3]
    gate_out[...] = jnp.concatenate([e / den for e in es], axis=0)
    idx_out[...] = jnp.concatenate(idxs, axis=0)

    sel = (eid == idxs[0]) | (eid == idxs[1]) | (eid == idxs[2]) | (eid == idxs[3])
    self32 = sel.astype(F32)
    rr = lax.broadcasted_iota(jnp.int32, (TM, TM), 0)
    cc = lax.broadcasted_iota(jnp.int32, (TM, TM), 1)
    upper = (rr < cc).astype(BF16)
    before = jnp.dot(self32.astype(BF16), upper, preferred_element_type=F32)
    before = before + carry_ref[:, 0:1]
    ranks = [jnp.sum(jnp.where(eid == ix, before, 0.0), axis=0, keepdims=True) for ix in idxs]
    rank_out[...] = jnp.concatenate(ranks, axis=0).astype(jnp.int32)
    carry_ref[...] = carry_ref[...] + jnp.sum(self32, axis=1, keepdims=True)
    cnt_out[...] = carry_ref[...]


def _post(oa, ob, x2, gt1, sc2, sh2, g_post, g_pre, w, seq, tile_off, t, run_after=None):
    d = x2.shape[1]
    tpb = seq // TM
    row = lambda i: (i, 0)
    src = lambda i: (i + tile_off, 0)
    col = lambda i: (0, i)
    full = lambda i: (0, 0)
    per_b = lambda i: ((i + tile_off) // tpb, 0, 0)
    hw = oa.shape[1]
    in_specs = [
        pl.BlockSpec((TM, hw), row), pl.BlockSpec((TM, hw), row), pl.BlockSpec((TM, d), src),
        pl.BlockSpec((1, 1, d), per_b), pl.BlockSpec((1, 1, d), per_b), pl.BlockSpec((1, 1, d), per_b),
        pl.BlockSpec((1, d), full), pl.BlockSpec((1, d), full),
        pl.BlockSpec((hw, d), full), pl.BlockSpec((hw, d), full),
        pl.BlockSpec((N_EXPERTS, d), full), pl.BlockSpec((N_EXPERTS, 1), full),
    ]
    out_specs = [
        pl.BlockSpec((TM, d), row), pl.BlockSpec((TM, d // 2), row),
        pl.BlockSpec((TOP_K, TM), col), pl.BlockSpec((TOP_K, TM), col), pl.BlockSpec((TOP_K, TM), col),
        pl.BlockSpec((N_EXPERTS, LANES), full),
    ]
    out_shape = [
        jax.ShapeDtypeStruct((t, d), F32), jax.ShapeDtypeStruct((t, d // 2), U32),
        jax.ShapeDtypeStruct((TOP_K, t), jnp.int32), jax.ShapeDtypeStruct((TOP_K, t), F32),
        jax.ShapeDtypeStruct((TOP_K, t), jnp.int32),
        jax.ShapeDtypeStruct((N_EXPERTS, LANES), F32),
    ]
    args = [oa, ob, x2, gt1, sc2, sh2, g_post, g_pre, w["wo_a"], w["wo_b"], w["wr_t"], w["b_r"]]
    body = _post_kernel
    if run_after is not None:
        n_in = len(args)
        body = lambda *refs: _post_kernel(*refs[:n_in], *refs[n_in + 1:])
        in_specs.append(pl.BlockSpec(memory_space=pl.ANY))
        args.append(run_after)
    return pl.pallas_call(
        body,
        grid=(t // TM,),
        in_specs=in_specs,
        out_specs=out_specs,
        out_shape=out_shape,
        scratch_shapes=[pltpu.VMEM((N_EXPERTS, LANES), F32)],
        compiler_params=_cparams(("arbitrary",)),
        name="post",
    )(*args)


def _expert_kernel(be_ref, nu_ref, x_ref, wgu_ref, bgu_ref, wd_ref, bd_ref, y_ref, wgu_bf, wd_bf):
    j = pl.program_id(0)
    used = j < nu_ref[0]

    @pl.when(used & ((j == 0) | (be_ref[j] != be_ref[jnp.maximum(j - 1, 0)])))
    def _():
        wgu_bf[...] = wgu_ref[0].astype(BF16)
        wd_bf[...] = wd_ref[0].astype(BF16)

    @pl.when(used)
    def _():
        gus = []
        for rs in range(0, ROWS, EXPERT_SUB):
            x_lo, x_hi = _unpack_rows(x_ref[rs:rs + EXPERT_SUB])
            half = x_lo.shape[1]
            gu = jnp.dot(x_lo.astype(BF16), wgu_bf[:half], preferred_element_type=F32)
            gu += jnp.dot(x_hi.astype(BF16), wgu_bf[half:], preferred_element_type=F32)
            gus.append(gu + bgu_ref[0])
        for sub, gu in enumerate(gus):
            rows = slice(sub * EXPERT_SUB, (sub + 1) * EXPERT_SUB)
            gate = jnp.minimum(gu[:, :D_EXPERT], SWIGLU_LIMIT)
            up = jnp.clip(gu[:, D_EXPERT:], -SWIGLU_LIMIT, SWIGLU_LIMIT)
            glu = gate / (1.0 + jnp.exp(-SWIGLU_ALPHA * gate))
            act = ((up + 1.0) * glu).astype(BF16)
            y = jnp.dot(act, wd_bf[...], preferred_element_type=F32) + bd_ref[0]
            y_ref[rows] = _pack_rows(y)

    @pl.when(j >= nu_ref[0])
    def _():
        y_ref[...] = jnp.zeros_like(y_ref)


def _experts(xin, block_exp, n_used, wgu, bgu, wd, bd):
    p_rows, dw = xin.shape
    d = 2 * dw
    nb = p_rows // ROWS
    f2 = wgu.shape[2]
    grid_spec = pltpu.PrefetchScalarGridSpec(
        num_scalar_prefetch=2,
        grid=(nb,),
        in_specs=[
            pl.BlockSpec((ROWS, dw), lambda j, be, nu: (jnp.minimum(j, nu[0] - 1), 0)),
            pl.BlockSpec((1, d, f2), lambda j, be, nu: (be[j], 0, 0)),
            pl.BlockSpec((1, 1, f2), lambda j, be, nu: (be[j], 0, 0)),
            pl.BlockSpec((1, f2 // 2, d), lambda j, be, nu: (be[j], 0, 0)),
            pl.BlockSpec((1, 1, d), lambda j, be, nu: (be[j], 0, 0)),
        ],
        out_specs=pl.BlockSpec((ROWS, dw), lambda j, be, nu: (j, 0)),
        scratch_shapes=[pltpu.VMEM((d, f2), BF16), pltpu.VMEM((f2 // 2, d), BF16)],
    )
    return pl.pallas_call(
        _expert_kernel,
        grid_spec=grid_spec,
        out_shape=jax.ShapeDtypeStruct((p_rows, dw), U32),
        compiler_params=_cparams(("arbitrary",)),
        name="experts",
    )(block_exp, n_used, xin, wgu, bgu, wd, bd)


def _final_kernel(yg_ref, g_ref, x1_ref, gt_ref, gpost_ref, *rest):
    o_ref = rest[-1]
    g = g_ref[...]
    f_lo, f_hi = None, None
    for k in range(TOP_K):
        lo, hi = _unpack_rows(yg_ref[k])
        gk = g[:, k:k + 1]
        f_lo = lo * gk if f_lo is None else f_lo + lo * gk
        f_hi = hi * gk if f_hi is None else f_hi + hi * gk
    f = jnp.concatenate([f_lo, f_hi], axis=1)
    o_ref[...] = x1_ref[...] + gt_ref[0] * _rms(f, gpost_ref[...])


def _final(yg, gates_t, x1, gt2, g_post, seq, tile_off, t_all, prev_out):
    t, d = x1.shape
    tpb = seq // TM
    row = lambda i: (i, 0)
    in_specs = [pl.BlockSpec((TOP_K, TM, d // 2), lambda i: (0, i, 0)),
                pl.BlockSpec((TM, TOP_K), row),
                pl.BlockSpec((TM, d), row),
                pl.BlockSpec((1, 1, d), lambda i: ((i + tile_off) // tpb, 0, 0)),
                pl.BlockSpec((1, d), lambda i: (0, 0))]
    args = [yg, gates_t, x1, gt2, g_post]
    aliases = {}
    if prev_out is not None:
        in_specs.append(pl.BlockSpec(memory_space=pl.ANY))
        args.append(prev_out)
        aliases = {len(args) - 1: 0}
    return pl.pallas_call(
        _final_kernel,
        grid=(t // TM,),
        in_specs=in_specs,
        out_specs=pl.BlockSpec((TM, d), lambda i: (i + tile_off, 0)),
        out_shape=jax.ShapeDtypeStruct((t_all, d), F32),
        input_output_aliases=aliases,
        compiler_params=_cparams(("arbitrary",)),
        name="final",
    )(*args)


def _sc_mesh():
    return plsc.VectorSubcoreMesh(core_axis_name="c", subcore_axis_name="s")


def _sc_worker():
    return lax.axis_index("s") * SC_CORES + lax.axis_index("c")


def _dispatch(h2, dest, p_rows):
    t, dw = h2.shape
    per_w = t // SC_WORKERS
    n_win = per_w // SC_WIN

    @functools.partial(
        pl.kernel, mesh=_sc_mesh(),
        out_type=jax.ShapeDtypeStruct((p_rows, dw), h2.dtype),
        scratch_types=[pltpu.VMEM((TOP_K, SC_WIN), jnp.int32),
                       pltpu.VMEM((SC_WIN, dw), h2.dtype),
                       pltpu.SemaphoreType.DMA],
        name="dispatch",
    )
    def run(h_hbm, d_hbm, o_hbm, idx_v, rows_v, sem):
        wid = _sc_worker()

        @pl.loop(0, n_win)
        def _(wi):
            base = pl.multiple_of(wid * per_w + wi * SC_WIN, SC_WIN)
            for k in range(TOP_K):
                pltpu.sync_copy(d_hbm.at[pl.ds(k * t + base, SC_WIN)], idx_v.at[k])
            pltpu.sync_copy(h_hbm.at[pl.ds(base, SC_WIN)], rows_v)
            for k in range(TOP_K):
                pltpu.async_copy(rows_v, o_hbm.at[idx_v.at[k]], sem).wait()

    return run(h2, dest)


def _gather_rows(y, dest, t):
    _, dw = y.shape
    per_w = t // SC_WORKERS
    n_win = per_w // SC_WIN

    @functools.partial(
        pl.kernel, mesh=_sc_mesh(),
        out_type=jax.ShapeDtypeStruct((TOP_K, t, dw), y.dtype),
        scratch_types=[pltpu.VMEM((SC_WIN,), jnp.int32),
                       pltpu.VMEM((SC_WIN, dw), y.dtype),
                       pltpu.SemaphoreType.DMA],
        name="gather_rows",
    )
    def run(y_hbm, d_hbm, o_hbm, idx_v, rows_v, sem):
        wid = _sc_worker()

        @pl.loop(0, n_win)
        def _(wi):
            base = pl.multiple_of(wid * per_w + wi * SC_WIN, SC_WIN)
            for k in range(TOP_K):
                pltpu.sync_copy(d_hbm.at[pl.ds(k * t + base, SC_WIN)], idx_v)
                pltpu.async_copy(y_hbm.at[idx_v], rows_v, sem).wait()
                pltpu.sync_copy(rows_v, o_hbm.at[k, pl.ds(base, SC_WIN)])

    return run(y, dest)


def _prep_weights(w_in, g_q, w_qb, g_kv, w_kvb, w_o, w_router, b_router):
    d = w_in.shape[0]
    o = 0
    w_cq = w_in[:, o:o + Q_LORA]; o += Q_LORA
    w_ckv = w_in[:, o:o + KV_LORA]; o += KV_LORA
    w_kr = w_in[:, o:o + MLA_ROPE]; o += MLA_ROPE
    w_ca = w_in[:, o:]
    half = MLA_ROPE // 2
    zpad = lambda n: jnp.zeros((d, n), w_in.dtype)
    tail = HEAD_PAD - MLA_NOPE - MLA_ROPE
    w_kr_pad = jnp.concatenate([zpad(MLA_NOPE), w_kr, zpad(tail)], axis=1)
    w_in_all = jnp.concatenate([w_cq, w_ckv, w_kr_pad, w_ca], axis=1).astype(BF16)

    wq = w_qb.reshape(Q_LORA, MLA_HEADS, MLA_NOPE + MLA_ROPE)
    zq = jnp.zeros((Q_LORA, MLA_HEADS, tail), w_qb.dtype)
    wq_pad = jnp.concatenate([wq, zq], axis=-1).reshape(Q_LORA, -1).astype(BF16)

    wkv = w_kvb.reshape(KV_LORA, MLA_HEADS, MLA_NOPE + MLA_V)
    kn = wkv[..., :MLA_NOPE]
    wk = jnp.concatenate([kn, jnp.zeros((KV_LORA, MLA_HEADS, HEAD_PAD - MLA_NOPE), w_kvb.dtype)],
                         axis=-1).reshape(KV_LORA, -1).astype(BF16)
    wv = jnp.concatenate([wkv[..., MLA_NOPE:], jnp.zeros((KV_LORA, MLA_HEADS, HEAD_PAD - MLA_V), w_kvb.dtype)],
                         axis=-1).reshape(KV_LORA, -1).T.astype(BF16)
    v_one = jnp.tile((jnp.arange(HEAD_PAD) == MLA_V).astype(F32), MLA_HEADS).reshape(-1, 1)
    mla_w = MLA_HEADS * MLA_V
    return {
        "w_in": w_in_all, "g_q": g_q.reshape(1, -1), "wq": wq_pad,
        "g_kv": g_kv.reshape(1, -1), "wk": wk, "wv": wv, "v_one": v_one,
        "wo_a": w_o[:mla_w].astype(BF16), "wo_b": w_o[mla_w:].astype(BF16),
        "wr_t": w_router.T.astype(BF16), "b_r": b_router.reshape(-1, 1),
    }


def _rope_tables(seq):
    half = MLA_ROPE // 2
    inv_freq = ROPE_THETA ** (-jnp.arange(half, dtype=F32) / half)
    ang = jnp.arange(seq, dtype=F32)[:, None] * inv_freq[None, :]
    cos, sin = jnp.cos(ang), jnp.sin(ang)
    tail = HEAD_PAD - MLA_NOPE - MLA_ROPE
    ones = jnp.ones((seq, MLA_NOPE), F32)
    zn = jnp.zeros((seq, MLA_NOPE), F32)
    zt = jnp.zeros((seq, tail), F32)
    qs = (MLA_NOPE + MLA_ROPE) ** -0.5 * LOG2E
    return {
        "cq": jnp.concatenate([ones, cos, cos, zt], axis=1) * qs,
        "sq": jnp.concatenate([zn, -sin, sin, zt], axis=1) * qs,
        "ck": jnp.concatenate([zn, cos, cos, zt], axis=1),
        "sk": jnp.concatenate([zn, -sin, sin, zt], axis=1),
    }


def _bias_table(rel_bias):
    n = CQ + CBAND - 1
    rel = (CBAND - 1) - jnp.arange(n)
    diag = rel_bias.astype(F32)[:, jnp.clip(rel, -(CHUNK - 1), REL_MAX) + (CHUNK - 1)] * LOG2E
    diag = jnp.concatenate([diag, jnp.zeros((diag.shape[0], 1), F32)], axis=1)
    b = jnp.tile(diag, (1, CQ))[:, :CQ * n].reshape(-1, CQ, n)[:, :, CQ - 1:]
    r = jnp.arange(CQ)[:, None]
    c = jnp.arange(CBAND)[None, :]
    dchunk = r // CHUNK - (c // CHUNK - CA_LEFT)
    visible = (dchunk >= 0) & (dchunk <= CA_LEFT)
    exists = (c // CQ)[None] >= (2 - jnp.arange(3))[:, None, None]
    return jnp.where((visible[None] & exists)[:, None], b[None], NEG)


def _layer(x, c, w_ada, b_ada, g_pre_mix, g_post_mix, g_pre_ffn, g_post_ffn, w_in, g_q, w_qb,
           g_kv, w_kvb, rel_bias, w_o, w_router, b_router, w_gu, b_gu, w_down, b_down):
    bsz, seq, d = x.shape
    t = bsz * seq
    mod = _ada(c, w_ada, b_ada).reshape(bsz, 6, 1, d)
    sh1, sc1, gt1, sh2, sc2, gt2 = [mod[:, k] for k in range(6)]
    w = _prep_weights(w_in, g_q, w_qb, g_kv, w_kvb, w_o, w_router, b_router)
    tabs = _rope_tables(seq)
    x2 = x.reshape(t, d)

    bp = bsz // MOE_PARTS
    tp = bp * seq
    bias_t = jnp.swapaxes(_bias_table(rel_bias), 2, 3)
    eids = jnp.arange(N_EXPERTS, dtype=jnp.int32)[:, None, None]
    p_rows = tp * TOP_K + N_EXPERTS * ROWS
    block_start = jnp.arange(p_rows // ROWS, dtype=jnp.int32) * ROWS
    shp = lambda a: a.reshape(bp, seq, a.shape[-1])
    projected = [_proj(x2, sc1, sh1, g_pre_mix.reshape(1, d), w, tabs, seq, part * bp, tp)
                 for part in range(MOE_PARTS)]
    routed = []
    for part, (q, k, v, qc, kc, vc) in enumerate(projected):
        oa = _mla(shp(q), shp(k), v).reshape(tp, -1)
        ob = _chunk_attn(shp(qc), shp(kc), vc, bias_t).reshape(tp, -1)
        next_q = projected[part + 1][0] if part + 1 < MOE_PARTS else None
        x1, h2, top_idx, gates, rank, cnt = _post(oa, ob, x2, gt1, sc2, sh2, g_post_mix.reshape(1, d),
                                                  g_pre_ffn.reshape(1, d), w, seq, part * (tp // TM), tp,
                                                  run_after=next_q)
        counts = cnt[:, 0].astype(jnp.int32)
        padded = ((counts + ROWS - 1) // ROWS) * ROWS
        pend = jnp.cumsum(padded)
        pstart = pend - padded
        dest = (jnp.sum(jnp.where(top_idx[None] == eids, pstart[:, None, None], 0), axis=0) + rank).reshape(-1)
        block_exp = jnp.minimum(jnp.sum(pend[None, :] <= block_start[:, None], axis=1),
                                N_EXPERTS - 1).astype(jnp.int32)
        n_used = (pend[-1:] // ROWS).astype(jnp.int32)
        routed.append((x1, gates.T, dest, block_exp, n_used, _dispatch(h2, dest, p_rows)))

    out = None
    for part, (x1, gates_t, dest, block_exp, n_used, xin) in enumerate(routed):
        y = _experts(xin, block_exp, n_used, w_gu, b_gu.reshape(N_EXPERTS, 1, -1),
                     w_down, b_down.reshape(N_EXPERTS, 1, -1))
        yg = _gather_rows(y, dest, tp)
        out = _final(yg, gates_t, x1, gt2, g_post_ffn.reshape(1, d), seq, part * (tp // TM), t, out)
    return out.reshape(bsz, seq, d)


def kernel(x, c, w_ada, b_ada, g_pre_mix, g_post_mix, g_pre_ffn, g_post_ffn, w_in, g_q, w_qb,
           g_kv, w_kvb, rel_bias, w_o, w_router, b_router, w_gu, b_gu, w_down, b_down):
    for l in range(w_ada.shape[0]):
        x = _layer(x, c, w_ada[l], b_ada[l], g_pre_mix[l], g_post_mix[l], g_pre_ffn[l], g_post_ffn[l],
                   w_in[l], g_q[l], w_qb[l], g_kv[l], w_kvb[l], rel_bias[l], w_o[l], w_router[l],
                   b_router[l], w_gu[l], b_gu[l], w_down[l], b_down[l])
    return x
```

```python
import functools
import math

import jax
import jax.numpy as jnp
import numpy as np
from jax import lax
from jax.experimental import pallas as pl
from jax.experimental.pallas import tpu as pltpu
from jax.experimental.pallas import tpu_sc as plsc

F32 = jnp.float32
BF16 = jnp.bfloat16
U32 = jnp.uint32

D_MODEL = 1024
CHUNK = 64
EPS = 1e-6
MLA_HEADS = 8
MLA_NOPE = 64
MLA_ROPE = 32
MLA_V = 64
Q_LORA = 256
KV_LORA = 128
ROPE_THETA = 10000.0
CA_HEADS = 8
CA_DIM = 64
CA_LEFT = 8
REL_MAX = 256
N_EXPERTS = 32
TOP_K = 4
D_EXPERT = 1024
SWIGLU_LIMIT = 7.0
SWIGLU_ALPHA = 1.702

LANES = 128
HEAD_PAD = 128
LOG2E = math.log2(math.e)
NEG = -1e30
VMEM_LIMIT = 56 * 1024 * 1024

TM = 1024
POST_SUB = 128
PTM = 1024
PROJ_SUB = 1024
TQ = 512
TKW = 2 * TQ
MLA_HPS = 4
MLA_LEAD = 2
CQ = 256
CBAND = 3 * CQ
CSTEP = 2 * CQ
ROWS = 512
EXPERT_SUB = 256
MOE_PARTS = 2
SC_CORES = 2
SC_WORKERS = SC_CORES * 16
SC_WIN = 128
CA_W = CA_HEADS * CA_DIM
W_IN_COLS = Q_LORA + KV_LORA + LANES + 3 * CA_W


def _cparams(sem, flags=None):
    return pltpu.CompilerParams(dimension_semantics=sem, vmem_limit_bytes=VMEM_LIMIT, flags=flags)


def _nt_dot(a, b):
    return lax.dot_general(a, b, (((1,), (1,)), ((), ())), preferred_element_type=F32)


def _rms(x, g):
    return x * lax.rsqrt(jnp.mean(x * x, axis=-1, keepdims=True) + EPS) * g


def _pack_rows(x):
    n = x.shape[1] // 2
    lo = lax.bitcast_convert_type(x[:, :n].astype(BF16).astype(F32), U32)
    hi = lax.bitcast_convert_type(x[:, n:].astype(BF16).astype(F32), U32)
    return (lo >> 16) | hi


def _unpack_rows(p):
    lo = lax.bitcast_convert_type(p << 16, F32)
    hi = lax.bitcast_convert_type(p & jnp.uint32(0xFFFF0000), F32)
    return lo, hi


def _ada_kernel(c_ref, w_ref, b_ref, o_ref):
    c = c_ref[...]
    a = (c / (1.0 + jnp.exp(-c))).astype(BF16)
    o_ref[...] = jnp.dot(a, w_ref[...].astype(BF16), preferred_element_type=F32) + b_ref[...]


def _ada(c, w, b):
    bsz, d = c.shape
    n = w.shape[1]
    tn = 1024
    return pl.pallas_call(
        _ada_kernel,
        grid=(n // tn,),
        in_specs=[pl.BlockSpec((bsz, d), lambda j: (0, 0)),
                  pl.BlockSpec((d, tn), lambda j: (0, j)),
                  pl.BlockSpec((1, tn), lambda j: (0, j))],
        out_specs=pl.BlockSpec((bsz, tn), lambda j: (0, j)),
        out_shape=jax.ShapeDtypeStruct((bsz, n), F32),
        compiler_params=_cparams(("arbitrary",)),
        name="ada",
    )(c, w, b.reshape(1, n))


def _swap_halves(x):
    width = x.shape[1]
    half = MLA_ROPE // 2
    lane = lax.broadcasted_iota(jnp.int32, x.shape, 1) % HEAD_PAD
    return jnp.where(lane < MLA_NOPE + half, pltpu.roll(x, width - half, axis=1), pltpu.roll(x, half, axis=1))


def _proj_kernel(x_ref, sc_ref, sh_ref, g_ref, win_ref, gq_ref, wq_ref, gkv_ref,
                 wk_ref, wv_ref, vone_ref, cq_ref, sq_ref, ck_ref, sk_ref,
                 q_out, k_out, v_out, qc_out, kc_out, vc_out):
    zs = []
    for rs in range(0, PTM, PROJ_SUB):
        h = _rms(x_ref[rs:rs + PROJ_SUB], g_ref[...]) * (1.0 + sc_ref[0]) + sh_ref[0]
        zs.append(jnp.dot(h.astype(BF16), win_ref[...], preferred_element_type=F32))
    for sub, z in enumerate(zs):
        rows = slice(sub * PROJ_SUB, (sub + 1) * PROJ_SUB)
        o = 0
        cq = z[:, o:o + Q_LORA]; o += Q_LORA
        ckv = z[:, o:o + KV_LORA]; o += KV_LORA
        kr = z[:, o:o + LANES]; o += LANES
        qc = z[:, o:o + CA_W]; o += CA_W
        kc = z[:, o:o + CA_W]; o += CA_W
        vc = z[:, o:o + CA_W]

        cqn = _rms(cq, gq_ref[...]).astype(BF16)
        q = jnp.dot(cqn, wq_ref[...], preferred_element_type=F32)
        ckvn = _rms(ckv, gkv_ref[...]).astype(BF16)
        kn = jnp.dot(ckvn, wk_ref[...], preferred_element_type=F32)
        v_out[0, :, rows] = (_nt_dot(wv_ref[...], ckvn) + vone_ref[...]).astype(BF16)

        q_sw = _swap_halves(q)
        cq_t, sq_t = cq_ref[rows], sq_ref[rows]
        krope = kr * ck_ref[rows] + _swap_halves(kr) * sk_ref[rows]
        for hd in range(MLA_HEADS):
            sl = slice(hd * HEAD_PAD, (hd + 1) * HEAD_PAD)
            q_out[rows, sl] = (q[:, sl] * cq_t + q_sw[:, sl] * sq_t).astype(BF16)
            k_out[rows, sl] = (kn[:, sl] + krope).astype(BF16)

        qc_out[rows] = (qc * (CA_DIM ** -0.5 * LOG2E)).astype(BF16)
        kc_out[rows] = kc.astype(BF16)
        vc_out[0, :, rows] = vc.T.astype(BF16)


def _proj(x2, sc1, sh1, g_pre, w, tabs, seq, batch_off, t):
    d = x2.shape[1]
    tpb = seq // PTM
    row = lambda i: (i, 0)
    full = lambda i: (0, 0)
    per_b = lambda i: (i // tpb + batch_off, 0, 0)
    pos = lambda i: (i % tpb, 0)
    hw = MLA_HEADS * HEAD_PAD
    in_specs = [
        pl.BlockSpec((PTM, d), lambda i: (i + batch_off * tpb, 0)),
        pl.BlockSpec((1, 1, d), per_b), pl.BlockSpec((1, 1, d), per_b),
        pl.BlockSpec((1, d), full),
        pl.BlockSpec((d, W_IN_COLS), full),
        pl.BlockSpec((1, Q_LORA), full),
        pl.BlockSpec((Q_LORA, hw), full),
        pl.BlockSpec((1, KV_LORA), full),
        pl.BlockSpec((KV_LORA, hw), full), pl.BlockSpec((hw, KV_LORA), full), pl.BlockSpec((hw, 1), full),
        pl.BlockSpec((PTM, LANES), pos), pl.BlockSpec((PTM, LANES), pos),
        pl.BlockSpec((PTM, LANES), pos), pl.BlockSpec((PTM, LANES), pos),
    ]
    outs = ((hw, False), (hw, False), (hw, True), (CA_W, False), (CA_W, False), (CA_W, True))
    t_spec = lambda n: pl.BlockSpec((1, n, PTM), lambda i: (i // tpb, 0, i % tpb))
    return pl.pallas_call(
        _proj_kernel,
        grid=(t // PTM,),
        in_specs=in_specs,
        out_specs=[t_spec(n) if tr else pl.BlockSpec((PTM, n), row) for n, tr in outs],
        out_shape=[jax.ShapeDtypeStruct((t // seq, n, seq) if tr else (t, n), BF16) for n, tr in outs],
        compiler_params=_cparams(("arbitrary",)),
        name="proj",
    )(x2, sc1, sh1, g_pre, w["w_in"], w["g_q"], w["wq"], w["g_kv"], w["wk"], w["wv"], w["v_one"],
      tabs["cq"], tabs["sq"], tabs["ck"], tabs["sk"])


def _mla_kernel(q_ref, k_ref, v_ref, o_ref):
    i = pl.program_id(2)
    heads = [slice(hh * HEAD_PAD, (hh + 1) * HEAD_PAD) for hh in range(MLA_HPS)]

    def step(off, width, carry, masked=False):
        if masked:
            kc = lax.broadcasted_iota(jnp.int32, (width, TQ), 0) // CHUNK
            qc = lax.broadcasted_iota(jnp.int32, (width, TQ), 1) // CHUNK
            visible = kc <= qc

        def score(hs):
            s = _nt_dot(k_ref[0, pl.ds(off, width), hs], q_ref[0, :, hs])
            return jnp.where(visible, s, NEG) if masked else s

        scores = [score(hs) for hs in heads[:MLA_LEAD]]
        new = []
        for hh, hs in enumerate(heads):
            m, acc = carry[hh]
            m_new = jnp.maximum(m, jnp.max(scores[hh], axis=0, keepdims=True))
            p = jnp.exp2(scores[hh] - m_new).astype(BF16)
            if hh + MLA_LEAD < MLA_HPS:
                scores.append(score(heads[hh + MLA_LEAD]))
            pv = jnp.dot(v_ref[0, hs, pl.ds(off, width)], p, preferred_element_type=F32)
            new.append((m_new, jnp.exp2(m - m_new) * acc + pv))
        return tuple(new)

    init = tuple((jnp.full((1, TQ), NEG, F32), jnp.zeros((HEAD_PAD, TQ), F32)) for _ in heads)
    carry = lax.fori_loop(0, i // 2, lambda j, c: step(pl.multiple_of(j * TKW, TKW), TKW, c), init)
    carry = lax.fori_loop(0, i % 2, lambda _, c: step(pl.multiple_of((i - 1) * TQ, TQ), TQ, c), carry)
    carry = step(pl.multiple_of(i * TQ, TQ), TQ, carry, True)
    for pp in range(MLA_HPS // 2):
        pair = [acc[:MLA_V] / acc[MLA_V:MLA_V + 1] for _, acc in (carry[2 * pp], carry[2 * pp + 1])]
        o_ref[0, :, pp * LANES:(pp + 1) * LANES] = jnp.concatenate(pair, axis=0).T.astype(BF16)


def _mla(q, k, v):
    bsz, seq, _ = q.shape
    groups = MLA_HEADS // MLA_HPS
    return pl.pallas_call(
        _mla_kernel,
        grid=(bsz, groups, seq // TQ),
        in_specs=[pl.BlockSpec((1, TQ, MLA_HPS * HEAD_PAD), lambda b, p, i: (b, i, p)),
                  pl.BlockSpec((1, seq, MLA_HPS * HEAD_PAD), lambda b, p, i: (b, 0, p)),
                  pl.BlockSpec((1, MLA_HPS * HEAD_PAD, seq), lambda b, p, i: (b, p, 0))],
        out_specs=pl.BlockSpec((1, TQ, MLA_HPS * MLA_V), lambda b, p, i: (b, i, p)),
        out_shape=jax.ShapeDtypeStruct((bsz, seq, MLA_HEADS * MLA_V), BF16),
        compiler_params=_cparams(("arbitrary", "arbitrary", "arbitrary")),
        name="mla",
    )(q, k, v)


def _chunk_kernel(q_ref, kp_ref, kc_ref, vp_ref, vc_ref, bias0_ref, bias1_ref, o_ref):
    lane = lax.broadcasted_iota(jnp.int32, (CQ, LANES), 1)
    lo = lane < CA_DIM
    row = lax.broadcasted_iota(jnp.int32, (LANES, CQ), 0)
    top = row < CA_DIM
    bias_refs = (bias0_ref, bias1_ref)
    subs = CSTEP // CQ

    def key_rows(j):
        return (kp_ref if j < subs else kc_ref), slice((j % subs) * CQ, (j % subs + 1) * CQ)

    def score(sub, head):
        sl = slice((head // 2) * LANES, (head // 2 + 1) * LANES)
        q = q_ref[0, sub * CQ:(sub + 1) * CQ, sl]
        qm = jnp.where(lo if head % 2 == 0 else jnp.logical_not(lo), q, jnp.zeros_like(q))
        parts = []
        for cb in range(3):
            ref, rows = key_rows(sub + cb)
            parts.append(_nt_dot(ref[0, rows, sl], qm))
        return jnp.concatenate(parts, axis=0)

    scores = {(sub, h): score(sub, h) for sub in range(subs) for h in range(CA_HEADS)}
    for sub in range(subs):
        for p in range(CA_HEADS // 2):
            sl = slice(p * LANES, (p + 1) * LANES)
            vts = []
            for cb in range(3):
                ref, cols = key_rows(sub + cb)
                vts.append((vp_ref if ref is kp_ref else vc_ref)[0, sl, cols])
            outs = []
            for hh in range(2):
                mine = top if hh == 0 else jnp.logical_not(top)
                den_row = CA_DIM if hh == 0 else 0
                ones_row = (row == den_row).astype(BF16)
                s = scores[(sub, 2 * p + hh)] + bias_refs[sub][0, 2 * p + hh]
                m = jnp.max(s, axis=0, keepdims=True)
                pb = jnp.exp2(s - m).astype(BF16)
                o = None
                for cb in range(3):
                    part = jnp.dot(jnp.where(mine, vts[cb], ones_row), pb[cb * CQ:(cb + 1) * CQ],
                                   preferred_element_type=F32)
                    o = part if o is None else o + part
                o = o / o[den_row:den_row + 1]
                outs.append(o[:CA_DIM] if hh == 0 else o[CA_DIM:])
            o_ref[0, sub * CQ:(sub + 1) * CQ, sl] = jnp.concatenate(outs, axis=0).T.astype(BF16)


def _chunk_attn(qc, kc, vct, bias):
    bsz, seq, w = qc.shape
    subs = CSTEP // CQ
    prev = lambda i: jnp.maximum(i - 1, 0)
    table = lambda s: pl.BlockSpec((1, CA_HEADS, CBAND, CQ), lambda b, i: (jnp.minimum(subs * i + s, 2), 0, 0, 0))
    return pl.pallas_call(
        _chunk_kernel,
        grid=(bsz, seq // CSTEP),
        in_specs=[pl.BlockSpec((1, CSTEP, w), lambda b, i: (b, i, 0)),
                  pl.BlockSpec((1, CSTEP, w), lambda b, i: (b, prev(i), 0)),
                  pl.BlockSpec((1, CSTEP, w), lambda b, i: (b, i, 0)),
                  pl.BlockSpec((1, w, CSTEP), lambda b, i: (b, 0, prev(i))),
                  pl.BlockSpec((1, w, CSTEP), lambda b, i: (b, 0, i)),
                  table(0), table(1)],
        out_specs=pl.BlockSpec((1, CSTEP, w), lambda b, i: (b, i, 0)),
        out_shape=jax.ShapeDtypeStruct((bsz, seq, w), BF16),
        compiler_params=_cparams(("arbitrary", "arbitrary")),
        name="chunk_attn",
    )(qc, kc, kc, vct, vct, bias, bias)


def _post_kernel(oa_ref, ob_ref, x_ref, gt_ref, sc_ref, sh_ref, gpost_ref, gpre_ref,
                 woa_ref, wob_ref, wr_ref, br_ref,
                 x1_out, h2_out, idx_out, gate_out, rank_out, cnt_out, carry_ref):
    t = pl.program_id(0)

    @pl.when(t == 0)
    def _():
        carry_ref[...] = jnp.zeros_like(carry_ref)

    os = []
    for rs in range(0, TM, POST_SUB):
        rows = slice(rs, rs + POST_SUB)
        o = jnp.dot(oa_ref[rows], woa_ref[...], preferred_element_type=F32)
        os.append(o + jnp.dot(ob_ref[rows], wob_ref[...], preferred_element_type=F32))
    h2s = []
    gain1 = gt_ref[0] * gpost_ref[...]
    gain2 = gpre_ref[...] * (1.0 + sc_ref[0])
    for sub, o in enumerate(os):
        rows = slice(sub * POST_SUB, (sub + 1) * POST_SUB)
        x1 = x_ref[rows] + _rms(o, gain1)
        x1_out[rows] = x1
        h2f = _rms(x1, gain2) + sh_ref[0]
        h2_out[rows] = _pack_rows(h2f)
        h2s.append(h2f.astype(BF16))
    h2 = jnp.concatenate(h2s, axis=0)

    logits = _nt_dot(wr_ref[...], h2) + br_ref[...]
    eid = lax.broadcasted_iota(jnp.int32, (N_EXPERTS, TM), 0)
    vals, idxs = [], []
    work = logits
    for _k in range(TOP_K):
        m = jnp.max(work, axis=0, keepdims=True)
        ix = jnp.min(jnp.where(work == m, eid, N_EXPERTS), axis=0, keepdims=True)
        work = jnp.where(eid == ix, -jnp.inf, work)
        vals.append(m)
        idxs.append(ix)
    es = [jnp.exp(v - vals[0]) for v in vals]
    den = es[0] + es[1] + es[2] + es[3]
    gates8 = jnp.concatenate([e / den for e in es] + [jnp.zeros((8 - TOP_K, TM), F32)], axis=0)
    gate_out[...] = gates8.T[:, :TOP_K]
    idx_out[...] = jnp.concatenate(idxs, axis=0)

    sel = (eid == idxs[0]) | (eid == idxs[1]) | (eid == idxs[2]) | (eid == idxs[3])
    self32 = sel.astype(F32)
    rr = lax.broadcasted_iota(jnp.int32, (TM, TM), 0)
    cc = lax.broadcasted_iota(jnp.int32, (TM, TM), 1)
    upper = (rr < cc).astype(BF16)
    before = jnp.dot(self32.astype(BF16), upper, preferred_element_type=F32)
    before = before + carry_ref[:, 0:1]
    ranks = [jnp.sum(jnp.where(eid == ix, before, 0.0), axis=0, keepdims=True) for ix in idxs]
    rank_out[...] = jnp.concatenate(ranks, axis=0).astype(jnp.int32)
    carry_ref[...] = carry_ref[...] + jnp.sum(self32, axis=1, keepdims=True)
    cnt_out[...] = carry_ref[...]


def _post(oa, ob, x2, gt1, sc2, sh2, g_post, g_pre, w, seq, tile_off, t, run_after=None):
    d = x2.shape[1]
    tpb = seq // TM
    row = lambda i: (i, 0)
    src = lambda i: (i + tile_off, 0)
    col = lambda i: (0, i)
    full = lambda i: (0, 0)
    per_b = lambda i: ((i + tile_off) // tpb, 0, 0)
    hw = oa.shape[1]
    in_specs = [
        pl.BlockSpec((TM, hw), row), pl.BlockSpec((TM, hw), row), pl.BlockSpec((TM, d), src),
        pl.BlockSpec((1, 1, d), per_b), pl.BlockSpec((1, 1, d), per_b), pl.BlockSpec((1, 1, d), per_b),
        pl.BlockSpec((1, d), full), pl.BlockSpec((1, d), full),
        pl.BlockSpec((hw, d), full), pl.BlockSpec((hw, d), full),
        pl.BlockSpec((N_EXPERTS, d), full), pl.BlockSpec((N_EXPERTS, 1), full),
    ]
    out_specs = [
        pl.BlockSpec((TM, d), row), pl.BlockSpec((TM, d // 2), row),
        pl.BlockSpec((TOP_K, TM), col), pl.BlockSpec((TM, TOP_K), row), pl.BlockSpec((TOP_K, TM), col),
        pl.BlockSpec((N_EXPERTS, LANES), full),
    ]
    out_shape = [
        jax.ShapeDtypeStruct((t, d), F32), jax.ShapeDtypeStruct((t, d // 2), U32),
        jax.ShapeDtypeStruct((TOP_K, t), jnp.int32), jax.ShapeDtypeStruct((t, TOP_K), F32),
        jax.ShapeDtypeStruct((TOP_K, t), jnp.int32),
        jax.ShapeDtypeStruct((N_EXPERTS, LANES), F32),
    ]
    args = [oa, ob, x2, gt1, sc2, sh2, g_post, g_pre, w["wo_a"], w["wo_b"], w["wr_t"], w["b_r"]]
    body = _post_kernel
    if run_after is not None:
        n_in = len(args)
        body = lambda *refs: _post_kernel(*refs[:n_in], *refs[n_in + 1:])
        in_specs.append(pl.BlockSpec(memory_space=pl.ANY))
        args.append(run_after)
    return pl.pallas_call(
        body,
        grid=(t // TM,),
        in_specs=in_specs,
        out_specs=out_specs,
        out_shape=out_shape,
        scratch_shapes=[pltpu.VMEM((N_EXPERTS, LANES), F32)],
        compiler_params=_cparams(("arbitrary",)),
        name="post",
    )(*args)


def _expert_kernel(be_ref, nu_ref, x_ref, wgu_ref, bgu_ref, wd_ref, bd_ref, y_ref, wgu_bf, wd_bf):
    j = pl.program_id(0)
    used = j < nu_ref[0]

    @pl.when(used & ((j == 0) | (be_ref[j] != be_ref[jnp.maximum(j - 1, 0)])))
    def _():
        wgu_bf[...] = wgu_ref[0].astype(BF16)
        wd_bf[...] = wd_ref[0].astype(BF16)

    @pl.when(used)
    def _():
        gus = []
        for rs in range(0, ROWS, EXPERT_SUB):
            x_lo, x_hi = _unpack_rows(x_ref[rs:rs + EXPERT_SUB])
            half = x_lo.shape[1]
            gu = jnp.dot(x_lo.astype(BF16), wgu_bf[:half], preferred_element_type=F32)
            gu += jnp.dot(x_hi.astype(BF16), wgu_bf[half:], preferred_element_type=F32)
            gus.append(gu + bgu_ref[0])
        for sub, gu in enumerate(gus):
            rows = slice(sub * EXPERT_SUB, (sub + 1) * EXPERT_SUB)
            gate = jnp.minimum(gu[:, :D_EXPERT], SWIGLU_LIMIT)
            up = jnp.clip(gu[:, D_EXPERT:], -SWIGLU_LIMIT, SWIGLU_LIMIT)
            glu = gate / (1.0 + jnp.exp(-SWIGLU_ALPHA * gate))
            act = ((up + 1.0) * glu).astype(BF16)
            y = jnp.dot(act, wd_bf[...], preferred_element_type=F32) + bd_ref[0]
            y_ref[rows] = _pack_rows(y)

    @pl.when(j >= nu_ref[0])
    def _():
        y_ref[...] = jnp.zeros_like(y_ref)


def _experts(xin, block_exp, n_used, wgu, bgu, wd, bd):
    p_rows, dw = xin.shape
    d = 2 * dw
    nb = p_rows // ROWS
    f2 = wgu.shape[2]
    grid_spec = pltpu.PrefetchScalarGridSpec(
        num_scalar_prefetch=2,
        grid=(nb,),
        in_specs=[
            pl.BlockSpec((ROWS, dw), lambda j, be, nu: (jnp.minimum(j, nu[0] - 1), 0)),
            pl.BlockSpec((1, d, f2), lambda j, be, nu: (be[j], 0, 0)),
            pl.BlockSpec((1, 1, f2), lambda j, be, nu: (be[j], 0, 0)),
            pl.BlockSpec((1, f2 // 2, d), lambda j, be, nu: (be[j], 0, 0)),
            pl.BlockSpec((1, 1, d), lambda j, be, nu: (be[j], 0, 0)),
        ],
        out_specs=pl.BlockSpec((ROWS, dw), lambda j, be, nu: (j, 0)),
        scratch_shapes=[pltpu.VMEM((d, f2), BF16), pltpu.VMEM((f2 // 2, d), BF16)],
    )
    return pl.pallas_call(
        _expert_kernel,
        grid_spec=grid_spec,
        out_shape=jax.ShapeDtypeStruct((p_rows, dw), U32),
        compiler_params=_cparams(("arbitrary",)),
        name="experts",
    )(block_exp, n_used, xin, wgu, bgu, wd, bd)


def _final_kernel(yg_ref, g_ref, x1_ref, gt_ref, gpost_ref, *rest):
    o_ref = rest[-1]
    g = g_ref[...]
    f_lo, f_hi = None, None
    for k in range(TOP_K):
        lo, hi = _unpack_rows(yg_ref[k])
        gk = g[:, k:k + 1]
        f_lo = lo * gk if f_lo is None else f_lo + lo * gk
        f_hi = hi * gk if f_hi is None else f_hi + hi * gk
    f = jnp.concatenate([f_lo, f_hi], axis=1)
    o_ref[...] = x1_ref[...] + gt_ref[0] * _rms(f, gpost_ref[...])


def _final(yg, gates_t, x1, gt2, g_post, seq, tile_off, t_all, prev_out):
    t, d = x1.shape
    tpb = seq // TM
    row = lambda i: (i, 0)
    in_specs = [pl.BlockSpec((TOP_K, TM, d // 2), lambda i: (0, i, 0)),
                pl.BlockSpec((TM, TOP_K), row),
                pl.BlockSpec((TM, d), row),
                pl.BlockSpec((1, 1, d), lambda i: ((i + tile_off) // tpb, 0, 0)),
                pl.BlockSpec((1, d), lambda i: (0, 0))]
    args = [yg, gates_t, x1, gt2, g_post]
    aliases = {}
    if prev_out is not None:
        in_specs.append(pl.BlockSpec(memory_space=pl.ANY))
        args.append(prev_out)
        aliases = {len(args) - 1: 0}
    return pl.pallas_call(
        _final_kernel,
        grid=(t // TM,),
        in_specs=in_specs,
        out_specs=pl.BlockSpec((TM, d), lambda i: (i + tile_off, 0)),
        out_shape=jax.ShapeDtypeStruct((t_all, d), F32),
        input_output_aliases=aliases,
        compiler_params=_cparams(("arbitrary",)),
        name="final",
    )(*args)


def _sc_mesh():
    return plsc.VectorSubcoreMesh(core_axis_name="c", subcore_axis_name="s")


def _sc_worker():
    return lax.axis_index("s") * SC_CORES + lax.axis_index("c")


def _dispatch(h2, dest, p_rows):
    t, dw = h2.shape
    per_w = t // SC_WORKERS
    n_win = per_w // SC_WIN

    @functools.partial(
        pl.kernel, mesh=_sc_mesh(),
        out_type=jax.ShapeDtypeStruct((p_rows, dw), h2.dtype),
        scratch_types=[pltpu.VMEM((TOP_K, SC_WIN), jnp.int32),
                       pltpu.VMEM((SC_WIN, dw), h2.dtype),
                       pltpu.SemaphoreType.DMA],
        name="dispatch",
    )
    def run(h_hbm, d_hbm, o_hbm, idx_v, rows_v, sem):
        wid = _sc_worker()

        @pl.loop(0, n_win)
        def _(wi):
            base = pl.multiple_of(wid * per_w + wi * SC_WIN, SC_WIN)
            for k in range(TOP_K):
                pltpu.sync_copy(d_hbm.at[pl.ds(k * t + base, SC_WIN)], idx_v.at[k])
            pltpu.sync_copy(h_hbm.at[pl.ds(base, SC_WIN)], rows_v)
            for k in range(TOP_K):
                pltpu.async_copy(rows_v, o_hbm.at[idx_v.at[k]], sem).wait()

    return run(h2, dest)


def _gather_rows(y, dest, t):
    _, dw = y.shape
    per_w = t // SC_WORKERS
    n_win = per_w // SC_WIN

    @functools.partial(
        pl.kernel, mesh=_sc_mesh(),
        out_type=jax.ShapeDtypeStruct((TOP_K, t, dw), y.dtype),
        scratch_types=[pltpu.VMEM((SC_WIN,), jnp.int32),
                       pltpu.VMEM((SC_WIN, dw), y.dtype),
                       pltpu.SemaphoreType.DMA],
        name="gather_rows",
    )
    def run(y_hbm, d_hbm, o_hbm, idx_v, rows_v, sem):
        wid = _sc_worker()

        @pl.loop(0, n_win)
        def _(wi):
            base = pl.multiple_of(wid * per_w + wi * SC_WIN, SC_WIN)
            for k in range(TOP_K):
                pltpu.sync_copy(d_hbm.at[pl.ds(k * t + base, SC_WIN)], idx_v)
                pltpu.async_copy(y_hbm.at[idx_v], rows_v, sem).wait()
                pltpu.sync_copy(rows_v, o_hbm.at[k, pl.ds(base, SC_WIN)])

    return run(y, dest)


def _prep_weights(w_in, g_q, w_qb, g_kv, w_kvb, w_o, w_router, b_router):
    d = w_in.shape[0]
    o = 0
    w_cq = w_in[:, o:o + Q_LORA]; o += Q_LORA
    w_ckv = w_in[:, o:o + KV_LORA]; o += KV_LORA
    w_kr = w_in[:, o:o + MLA_ROPE]; o += MLA_ROPE
    w_ca = w_in[:, o:]
    half = MLA_ROPE // 2
    zpad = lambda n: jnp.zeros((d, n), w_in.dtype)
    tail = HEAD_PAD - MLA_NOPE - MLA_ROPE
    w_kr_pad = jnp.concatenate([zpad(MLA_NOPE), w_kr, zpad(tail)], axis=1)
    w_in_all = jnp.concatenate([w_cq, w_ckv, w_kr_pad, w_ca], axis=1).astype(BF16)

    wq = w_qb.reshape(Q_LORA, MLA_HEADS, MLA_NOPE + MLA_ROPE)
    zq = jnp.zeros((Q_LORA, MLA_HEADS, tail), w_qb.dtype)
    wq_pad = jnp.concatenate([wq, zq], axis=-1).reshape(Q_LORA, -1).astype(BF16)

    wkv = w_kvb.reshape(KV_LORA, MLA_HEADS, MLA_NOPE + MLA_V)
    kn = wkv[..., :MLA_NOPE]
    wk = jnp.concatenate([kn, jnp.zeros((KV_LORA, MLA_HEADS, HEAD_PAD - MLA_NOPE), w_kvb.dtype)],
                         axis=-1).reshape(KV_LORA, -1).astype(BF16)
    wv = jnp.concatenate([wkv[..., MLA_NOPE:], jnp.zeros((KV_LORA, MLA_HEADS, HEAD_PAD - MLA_V), w_kvb.dtype)],
                         axis=-1).reshape(KV_LORA, -1).T.astype(BF16)
    v_one = jnp.tile((jnp.arange(HEAD_PAD) == MLA_V).astype(F32), MLA_HEADS).reshape(-1, 1)
    mla_w = MLA_HEADS * MLA_V
    return {
        "w_in": w_in_all, "g_q": g_q.reshape(1, -1), "wq": wq_pad,
        "g_kv": g_kv.reshape(1, -1), "wk": wk, "wv": wv, "v_one": v_one,
        "wo_a": w_o[:mla_w].astype(BF16), "wo_b": w_o[mla_w:].astype(BF16),
        "wr_t": w_router.T.astype(BF16), "b_r": b_router.reshape(-1, 1),
    }


def _rope_tables(seq):
    half = MLA_ROPE // 2
    inv_freq = np.float32(ROPE_THETA) ** (-np.arange(half, dtype=np.float32) / np.float32(half))
    ang = np.arange(seq, dtype=np.float32)[:, None] * inv_freq[None, :]
    cos, sin = jnp.asarray(np.cos(ang), F32), jnp.asarray(np.sin(ang), F32)
    tail = HEAD_PAD - MLA_NOPE - MLA_ROPE
    ones = jnp.ones((seq, MLA_NOPE), F32)
    zn = jnp.zeros((seq, MLA_NOPE), F32)
    zt = jnp.zeros((seq, tail), F32)
    qs = (MLA_NOPE + MLA_ROPE) ** -0.5 * LOG2E
    return {
        "cq": jnp.concatenate([ones, cos, cos, zt], axis=1) * qs,
        "sq": jnp.concatenate([zn, -sin, sin, zt], axis=1) * qs,
        "ck": jnp.concatenate([zn, cos, cos, zt], axis=1),
        "sk": jnp.concatenate([zn, -sin, sin, zt], axis=1),
    }


def _bias_table(rel_bias):
    n = CQ + CBAND - 1
    rel = (CBAND - 1) - jnp.arange(n)
    diag = rel_bias.astype(F32)[:, jnp.clip(rel, -(CHUNK - 1), REL_MAX) + (CHUNK - 1)] * LOG2E
    diag = jnp.concatenate([diag, jnp.zeros((diag.shape[0], 1), F32)], axis=1)
    b = jnp.tile(diag, (1, CQ))[:, :CQ * n].reshape(-1, CQ, n)[:, :, CQ - 1:]
    r = jnp.arange(CQ)[:, None]
    c = jnp.arange(CBAND)[None, :]
    dchunk = r // CHUNK - (c // CHUNK - CA_LEFT)
    visible = (dchunk >= 0) & (dchunk <= CA_LEFT)
    exists = (c // CQ)[None] >= (2 - jnp.arange(3))[:, None, None]
    return jnp.where((visible[None] & exists)[:, None], b[None], NEG)


def _layer(x, c, w_ada, b_ada, g_pre_mix, g_post_mix, g_pre_ffn, g_post_ffn, w_in, g_q, w_qb,
           g_kv, w_kvb, rel_bias, w_o, w_router, b_router, w_gu, b_gu, w_down, b_down):
    bsz, seq, d = x.shape
    t = bsz * seq
    mod = _ada(c, w_ada, b_ada).reshape(bsz, 6, 1, d)
    sh1, sc1, gt1, sh2, sc2, gt2 = [mod[:, k] for k in range(6)]
    w = _prep_weights(w_in, g_q, w_qb, g_kv, w_kvb, w_o, w_router, b_router)
    tabs = _rope_tables(seq)
    x2 = x.reshape(t, d)

    bp = bsz // MOE_PARTS
    tp = bp * seq
    bias_t = jnp.swapaxes(_bias_table(rel_bias), 2, 3)
    eids = jnp.arange(N_EXPERTS, dtype=jnp.int32)[:, None, None]
    p_rows = tp * TOP_K + N_EXPERTS * ROWS
    block_start = jnp.arange(p_rows // ROWS, dtype=jnp.int32) * ROWS
    shp = lambda a: a.reshape(bp, seq, a.shape[-1])
    projected = [_proj(x2, sc1, sh1, g_pre_mix.reshape(1, d), w, tabs, seq, part * bp, tp)
                 for part in range(MOE_PARTS)]
    routed = []
    for part, (q, k, v, qc, kc, vc) in enumerate(projected):
        oa = _mla(shp(q), shp(k), v).reshape(tp, -1)
        ob = _chunk_attn(shp(qc), shp(kc), vc, bias_t).reshape(tp, -1)
        next_q = projected[part + 1][0] if part + 1 < MOE_PARTS else None
        x1, h2, top_idx, gates, rank, cnt = _post(oa, ob, x2, gt1, sc2, sh2, g_post_mix.reshape(1, d),
                                                  g_pre_ffn.reshape(1, d), w, seq, part * (tp // TM), tp,
                                                  run_after=next_q)
        counts = cnt[:, 0].astype(jnp.int32)
        padded = ((counts + ROWS - 1) // ROWS) * ROWS
        pend = jnp.cumsum(padded)
        pstart = pend - padded
        dest = (jnp.sum(jnp.where(top_idx[None] == eids, pstart[:, None, None], 0), axis=0) + rank).reshape(-1)
        block_exp = jnp.minimum(jnp.sum(pend[None, :] <= block_start[:, None], axis=1),
                                N_EXPERTS - 1).astype(jnp.int32)
        n_used = (pend[-1:] // ROWS).astype(jnp.int32)
        routed.append((x1, gates, dest, block_exp, n_used, _dispatch(h2, dest, p_rows)))

    out = None
    for part, (x1, gates_t, dest, block_exp, n_used, xin) in enumerate(routed):
        y = _experts(xin, block_exp, n_used, w_gu, b_gu.reshape(N_EXPERTS, 1, -1),
                     w_down, b_down.reshape(N_EXPERTS, 1, -1))
        yg = _gather_rows(y, dest, tp)
        out = _final(yg, gates_t, x1, gt2, g_post_ffn.reshape(1, d), seq, part * (tp // TM), t, out)
    return out.reshape(bsz, seq, d)


def kernel(x, c, w_ada, b_ada, g_pre_mix, g_post_mix, g_pre_ffn, g_post_ffn, w_in, g_q, w_qb,
           g_kv, w_kvb, rel_bias, w_o, w_router, b_router, w_gu, b_gu, w_down, b_down):
    for l in range(w_ada.shape[0]):
        x = _layer(x, c, w_ada[l], b_ada[l], g_pre_mix[l], g_post_mix[l], g_pre_ffn[l], g_post_ffn[l],
                   w_in[l], g_q[l], w_qb[l], g_kv[l], w_kvb[l], rel_bias[l], w_o[l], w_router[l],
                   b_router[l], w_gu[l], b_gu[l], w_down[l], b_down[l])
    return x
```

```python
import functools
import math

import jax
import jax.numpy as jnp
from jax import lax
from jax.experimental import pallas as pl
from jax.experimental.pallas import tpu as pltpu
from jax.experimental.pallas import tpu_sc as plsc

F32 = jnp.float32
BF16 = jnp.bfloat16
U32 = jnp.uint32

D_MODEL = 1024
CHUNK = 64
EPS = 1e-6
MLA_HEADS = 8
MLA_NOPE = 64
MLA_ROPE = 32
MLA_V = 64
Q_LORA = 256
KV_LORA = 128
ROPE_THETA = 10000.0
CA_HEADS = 8
CA_DIM = 64
CA_LEFT = 8
REL_MAX = 256
N_EXPERTS = 32
TOP_K = 4
D_EXPERT = 1024
SWIGLU_LIMIT = 7.0
SWIGLU_ALPHA = 1.702

LANES = 128
HEAD_PAD = 128
LOG2E = math.log2(math.e)
NEG = -1e30
VMEM_LIMIT = 56 * 1024 * 1024

TM = 1024
POST_SUB = 128
PTM = 1024
PROJ_SUB = 1024
TQ = 512
TKW = 2 * TQ
MLA_HPS = 8
MLA_LEAD = 2
CQ = 256
CBAND = 3 * CQ
CSTEP = 2 * CQ
ROWS = 512
EXPERT_SUB = 256
MOE_PARTS = 2
SC_CORES = 2
SC_WORKERS = SC_CORES * 16
SC_WIN = 128
CA_W = CA_HEADS * CA_DIM
W_IN_COLS = Q_LORA + KV_LORA + LANES + 3 * CA_W


def _cparams(sem, flags=None):
    return pltpu.CompilerParams(dimension_semantics=sem, vmem_limit_bytes=VMEM_LIMIT, flags=flags)


def _nt_dot(a, b):
    return lax.dot_general(a, b, (((1,), (1,)), ((), ())), preferred_element_type=F32)


def _rms(x, g):
    return x * lax.rsqrt(jnp.mean(x * x, axis=-1, keepdims=True) + EPS) * g


def _pack_rows(x):
    n = x.shape[1] // 2
    lo = lax.bitcast_convert_type(x[:, :n].astype(BF16).astype(F32), U32)
    hi = lax.bitcast_convert_type(x[:, n:].astype(BF16).astype(F32), U32)
    return (lo >> 16) | hi


def _unpack_rows(p):
    lo = lax.bitcast_convert_type(p << 16, F32)
    hi = lax.bitcast_convert_type(p & jnp.uint32(0xFFFF0000), F32)
    return lo, hi


def _ada_kernel(c_ref, w_ref, b_ref, o_ref):
    c = c_ref[...]
    a = (c / (1.0 + jnp.exp(-c))).astype(BF16)
    o_ref[...] = jnp.dot(a, w_ref[...].astype(BF16), preferred_element_type=F32) + b_ref[...]


def _ada(c, w, b):
    bsz, d = c.shape
    n = w.shape[1]
    tn = 1024
    return pl.pallas_call(
        _ada_kernel,
        grid=(n // tn,),
        in_specs=[pl.BlockSpec((bsz, d), lambda j: (0, 0)),
                  pl.BlockSpec((d, tn), lambda j: (0, j)),
                  pl.BlockSpec((1, tn), lambda j: (0, j))],
        out_specs=pl.BlockSpec((bsz, tn), lambda j: (0, j)),
        out_shape=jax.ShapeDtypeStruct((bsz, n), F32),
        compiler_params=_cparams(("arbitrary",)),
        name="ada",
    )(c, w, b.reshape(1, n))


def _swap_halves(x):
    width = x.shape[1]
    half = MLA_ROPE // 2
    lane = lax.broadcasted_iota(jnp.int32, x.shape, 1) % HEAD_PAD
    return jnp.where(lane < MLA_NOPE + half, pltpu.roll(x, width - half, axis=1), pltpu.roll(x, half, axis=1))


def _proj_kernel(x_ref, sc_ref, sh_ref, g_ref, win_ref, gq_ref, wq_ref, gkv_ref,
                 wk_ref, wv_ref, vone_ref, cq_ref, sq_ref, ck_ref, sk_ref,
                 q_out, k_out, v_out, qc_out, kc_out, vc_out):
    zs = []
    for rs in range(0, PTM, PROJ_SUB):
        h = _rms(x_ref[rs:rs + PROJ_SUB], g_ref[...]) * (1.0 + sc_ref[0]) + sh_ref[0]
        zs.append(jnp.dot(h.astype(BF16), win_ref[...], preferred_element_type=F32))
    for sub, z in enumerate(zs):
        rows = slice(sub * PROJ_SUB, (sub + 1) * PROJ_SUB)
        o = 0
        cq = z[:, o:o + Q_LORA]; o += Q_LORA
        ckv = z[:, o:o + KV_LORA]; o += KV_LORA
        kr = z[:, o:o + LANES]; o += LANES
        qc = z[:, o:o + CA_W]; o += CA_W
        kc = z[:, o:o + CA_W]; o += CA_W
        vc = z[:, o:o + CA_W]

        cqn = _rms(cq, gq_ref[...]).astype(BF16)
        q = jnp.dot(cqn, wq_ref[...], preferred_element_type=F32)
        ckvn = _rms(ckv, gkv_ref[...]).astype(BF16)
        kn = jnp.dot(ckvn, wk_ref[...], preferred_element_type=F32)
        v_out[0, :, rows] = (_nt_dot(wv_ref[...], ckvn) + vone_ref[...]).astype(BF16)

        q_sw = _swap_halves(q)
        cq_t, sq_t = cq_ref[rows], sq_ref[rows]
        krope = kr * ck_ref[rows] + _swap_halves(kr) * sk_ref[rows]
        for hd in range(MLA_HEADS):
            sl = slice(hd * HEAD_PAD, (hd + 1) * HEAD_PAD)
            q_out[rows, sl] = (q[:, sl] * cq_t + q_sw[:, sl] * sq_t).astype(BF16)
            k_out[rows, sl] = (kn[:, sl] + krope).astype(BF16)

        qc_out[rows] = (qc * (CA_DIM ** -0.5 * LOG2E)).astype(BF16)
        kc_out[rows] = kc.astype(BF16)
        vc_out[0, :, rows] = vc.T.astype(BF16)


def _proj(x2, sc1, sh1, g_pre, w, tabs, seq, batch_off, t):
    d = x2.shape[1]
    tpb = seq // PTM
    row = lambda i: (i, 0)
    full = lambda i: (0, 0)
    per_b = lambda i: (i // tpb + batch_off, 0, 0)
    pos = lambda i: (i % tpb, 0)
    hw = MLA_HEADS * HEAD_PAD
    in_specs = [
        pl.BlockSpec((PTM, d), lambda i: (i + batch_off * tpb, 0)),
        pl.BlockSpec((1, 1, d), per_b), pl.BlockSpec((1, 1, d), per_b),
        pl.BlockSpec((1, d), full),
        pl.BlockSpec((d, W_IN_COLS), full),
        pl.BlockSpec((1, Q_LORA), full),
        pl.BlockSpec((Q_LORA, hw), full),
        pl.BlockSpec((1, KV_LORA), full),
        pl.BlockSpec((KV_LORA, hw), full), pl.BlockSpec((hw, KV_LORA), full), pl.BlockSpec((hw, 1), full),
        pl.BlockSpec((PTM, LANES), pos), pl.BlockSpec((PTM, LANES), pos),
        pl.BlockSpec((PTM, LANES), pos), pl.BlockSpec((PTM, LANES), pos),
    ]
    outs = ((hw, False), (hw, False), (hw, True), (CA_W, False), (CA_W, False), (CA_W, True))
    t_spec = lambda n: pl.BlockSpec((1, n, PTM), lambda i: (i // tpb, 0, i % tpb))
    return pl.pallas_call(
        _proj_kernel,
        grid=(t // PTM,),
        in_specs=in_specs,
        out_specs=[t_spec(n) if tr else pl.BlockSpec((PTM, n), row) for n, tr in outs],
        out_shape=[jax.ShapeDtypeStruct((t // seq, n, seq) if tr else (t, n), BF16) for n, tr in outs],
        compiler_params=_cparams(("arbitrary",)),
        name="proj",
    )(x2, sc1, sh1, g_pre, w["w_in"], w["g_q"], w["wq"], w["g_kv"], w["wk"], w["wv"], w["v_one"],
      tabs["cq"], tabs["sq"], tabs["ck"], tabs["sk"])


def _mla_kernel(q_ref, k_ref, v_ref, o_ref):
    i = pl.program_id(2)
    heads = [slice(hh * HEAD_PAD, (hh + 1) * HEAD_PAD) for hh in range(MLA_HPS)]

    def step(off, width, carry, masked=False):
        if masked:
            kc = lax.broadcasted_iota(jnp.int32, (width, TQ), 0) // CHUNK
            qc = lax.broadcasted_iota(jnp.int32, (width, TQ), 1) // CHUNK
            visible = kc <= qc

        def score(hs):
            s = _nt_dot(k_ref[0, pl.ds(off, width), hs], q_ref[0, :, hs])
            return jnp.where(visible, s, NEG) if masked else s

        scores = [score(hs) for hs in heads[:MLA_LEAD]]
        new = []
        for hh, hs in enumerate(heads):
            m, acc = carry[hh]
            m_new = jnp.maximum(m, jnp.max(scores[hh], axis=0, keepdims=True))
            p = jnp.exp2(scores[hh] - m_new).astype(BF16)
            if hh + MLA_LEAD < MLA_HPS:
                scores.append(score(heads[hh + MLA_LEAD]))
            pv = jnp.dot(v_ref[0, hs, pl.ds(off, width)], p, preferred_element_type=F32)
            new.append((m_new, jnp.exp2(m - m_new) * acc + pv))
        return tuple(new)

    init = tuple((jnp.full((1, TQ), NEG, F32), jnp.zeros((HEAD_PAD, TQ), F32)) for _ in heads)
    carry = lax.fori_loop(0, i // 2, lambda j, c: step(pl.multiple_of(j * TKW, TKW), TKW, c), init)
    carry = lax.fori_loop(0, i % 2, lambda _, c: step(pl.multiple_of((i - 1) * TQ, TQ), TQ, c), carry)
    carry = step(pl.multiple_of(i * TQ, TQ), TQ, carry, True)
    for pp in range(MLA_HPS // 2):
        pair = [acc[:MLA_V] / acc[MLA_V:MLA_V + 1] for _, acc in (carry[2 * pp], carry[2 * pp + 1])]
        o_ref[0, :, pp * LANES:(pp + 1) * LANES] = jnp.concatenate(pair, axis=0).T.astype(BF16)


def _mla(q, k, v):
    bsz, seq, _ = q.shape
    groups = MLA_HEADS // MLA_HPS
    return pl.pallas_call(
        _mla_kernel,
        grid=(bsz, groups, seq // TQ),
        in_specs=[pl.BlockSpec((1, TQ, MLA_HPS * HEAD_PAD), lambda b, p, i: (b, i, p)),
                  pl.BlockSpec((1, seq, MLA_HPS * HEAD_PAD), lambda b, p, i: (b, 0, p),
                               pipeline_mode=pl.Buffered(1)),
                  pl.BlockSpec((1, MLA_HPS * HEAD_PAD, seq), lambda b, p, i: (b, p, 0),
                               pipeline_mode=pl.Buffered(1))],
        out_specs=pl.BlockSpec((1, TQ, MLA_HPS * MLA_V), lambda b, p, i: (b, i, p)),
        out_shape=jax.ShapeDtypeStruct((bsz, seq, MLA_HEADS * MLA_V), BF16),
        compiler_params=_cparams(("arbitrary", "arbitrary", "arbitrary")),
        name="mla",
    )(q, k, v)


def _chunk_kernel(q_ref, kp_ref, kc_ref, vp_ref, vc_ref, bias0_ref, bias1_ref, o_ref):
    lane = lax.broadcasted_iota(jnp.int32, (CQ, LANES), 1)
    lo = lane < CA_DIM
    row = lax.broadcasted_iota(jnp.int32, (LANES, CQ), 0)
    top = row < CA_DIM
    bias_refs = (bias0_ref, bias1_ref)
    subs = CSTEP // CQ

    def key_rows(j):
        return (kp_ref if j < subs else kc_ref), slice((j % subs) * CQ, (j % subs + 1) * CQ)

    def score(sub, head):
        sl = slice((head // 2) * LANES, (head // 2 + 1) * LANES)
        q = q_ref[0, sub * CQ:(sub + 1) * CQ, sl]
        qm = jnp.where(lo if head % 2 == 0 else jnp.logical_not(lo), q, jnp.zeros_like(q))
        parts = []
        for cb in range(3):
            ref, rows = key_rows(sub + cb)
            parts.append(_nt_dot(ref[0, rows, sl], qm))
        return jnp.concatenate(parts, axis=0)

    scores = {(sub, h): score(sub, h) for sub in range(subs) for h in range(CA_HEADS)}
    for sub in range(subs):
        for p in range(CA_HEADS // 2):
            sl = slice(p * LANES, (p + 1) * LANES)
            vts = []
            for cb in range(3):
                ref, cols = key_rows(sub + cb)
                vts.append((vp_ref if ref is kp_ref else vc_ref)[0, sl, cols])
            outs = []
            for hh in range(2):
                mine = top if hh == 0 else jnp.logical_not(top)
                den_row = CA_DIM if hh == 0 else 0
                ones_row = (row == den_row).astype(BF16)
                s = scores[(sub, 2 * p + hh)] + bias_refs[sub][0, 2 * p + hh]
                m = jnp.max(s, axis=0, keepdims=True)
                pb = jnp.exp2(s - m).astype(BF16)
                o = None
                for cb in range(3):
                    part = jnp.dot(jnp.where(mine, vts[cb], ones_row), pb[cb * CQ:(cb + 1) * CQ],
                                   preferred_element_type=F32)
                    o = part if o is None else o + part
                o = o / o[den_row:den_row + 1]
                outs.append(o[:CA_DIM] if hh == 0 else o[CA_DIM:])
            o_ref[0, sub * CQ:(sub + 1) * CQ, sl] = jnp.concatenate(outs, axis=0).T.astype(BF16)


def _chunk_attn(qc, kc, vct, bias):
    bsz, seq, w = qc.shape
    subs = CSTEP // CQ
    prev = lambda i: jnp.maximum(i - 1, 0)
    table = lambda s: pl.BlockSpec((1, CA_HEADS, CBAND, CQ), lambda b, i: (jnp.minimum(subs * i + s, 2), 0, 0, 0))
    return pl.pallas_call(
        _chunk_kernel,
        grid=(bsz, seq // CSTEP),
        in_specs=[pl.BlockSpec((1, CSTEP, w), lambda b, i: (b, i, 0)),
                  pl.BlockSpec((1, CSTEP, w), lambda b, i: (b, prev(i), 0)),
                  pl.BlockSpec((1, CSTEP, w), lambda b, i: (b, i, 0)),
                  pl.BlockSpec((1, w, CSTEP), lambda b, i: (b, 0, prev(i))),
                  pl.BlockSpec((1, w, CSTEP), lambda b, i: (b, 0, i)),
                  table(0), table(1)],
        out_specs=pl.BlockSpec((1, CSTEP, w), lambda b, i: (b, i, 0)),
        out_shape=jax.ShapeDtypeStruct((bsz, seq, w), BF16),
        compiler_params=_cparams(("arbitrary", "arbitrary")),
        name="chunk_attn",
    )(qc, kc, kc, vct, vct, bias, bias)


def _post_kernel(oa_ref, ob_ref, x_ref, gt_ref, sc_ref, sh_ref, gpost_ref, gpre_ref,
                 woa_ref, wob_ref, wr_ref, br_ref,
                 x1_out, h2_out, idx_out, gate_out, rank_out, cnt_out, carry_ref):
    t = pl.program_id(0)

    @pl.when(t == 0)
    def _():
        carry_ref[...] = jnp.zeros_like(carry_ref)

    os = []
    for rs in range(0, TM, POST_SUB):
        rows = slice(rs, rs + POST_SUB)
        o = jnp.dot(oa_ref[rows], woa_ref[...], preferred_element_type=F32)
        os.append(o + jnp.dot(ob_ref[rows], wob_ref[...], preferred_element_type=F32))
    h2s = []
    gain1 = gt_ref[0] * gpost_ref[...]
    gain2 = gpre_ref[...] * (1.0 + sc_ref[0])
    for sub, o in enumerate(os):
        rows = slice(sub * POST_SUB, (sub + 1) * POST_SUB)
        x1 = x_ref[rows] + _rms(o, gain1)
        x1_out[rows] = x1
        h2f = _rms(x1, gain2) + sh_ref[0]
        h2_out[rows] = _pack_rows(h2f)
        h2s.append(h2f.astype(BF16))
    h2 = jnp.concatenate(h2s, axis=0)

    logits = _nt_dot(wr_ref[...], h2) + br_ref[...]
    eid = lax.broadcasted_iota(jnp.int32, (N_EXPERTS, TM), 0)
    vals, idxs = [], []
    work = logits
    for _k in range(TOP_K):
        m = jnp.max(work, axis=0, keepdims=True)
        ix = jnp.min(jnp.where(work == m, eid, N_EXPERTS), axis=0, keepdims=True)
        work = jnp.where(eid == ix, -jnp.inf, work)
        vals.append(m)
        idxs.append(ix)
    es = [jnp.exp(v - vals[0]) for v in vals]
    den = es[0] + es[1] + es[2] + es[3]
    gate_out[...] = jnp.concatenate([e / den for e in es], axis=0)
    idx_out[...] = jnp.concatenate(idxs, axis=0)

    sel = (eid == idxs[0]) | (eid == idxs[1]) | (eid == idxs[2]) | (eid == idxs[3])
    self32 = sel.astype(F32)
    rr = lax.broadcasted_iota(jnp.int32, (TM, TM), 0)
    cc = lax.broadcasted_iota(jnp.int32, (TM, TM), 1)
    upper = (rr < cc).astype(BF16)
    before = jnp.dot(self32.astype(BF16), upper, preferred_element_type=F32)
    before = before + carry_ref[:, 0:1]
    ranks = [jnp.sum(jnp.where(eid == ix, before, 0.0), axis=0, keepdims=True) for ix in idxs]
    rank_out[...] = jnp.concatenate(ranks, axis=0).astype(jnp.int32)
    carry_ref[...] = carry_ref[...] + jnp.sum(self32, axis=1, keepdims=True)
    cnt_out[...] = carry_ref[...]


def _post(oa, ob, x2, gt1, sc2, sh2, g_post, g_pre, w, seq, tile_off, t, run_after=None):
    d = x2.shape[1]
    tpb = seq // TM
    row = lambda i: (i, 0)
    src = lambda i: (i + tile_off, 0)
    col = lambda i: (0, i)
    full = lambda i: (0, 0)
    per_b = lambda i: ((i + tile_off) // tpb, 0, 0)
    hw = oa.shape[1]
    in_specs = [
        pl.BlockSpec((TM, hw), row), pl.BlockSpec((TM, hw), row), pl.BlockSpec((TM, d), src),
        pl.BlockSpec((1, 1, d), per_b), pl.BlockSpec((1, 1, d), per_b), pl.BlockSpec((1, 1, d), per_b),
        pl.BlockSpec((1, d), full), pl.BlockSpec((1, d), full),
        pl.BlockSpec((hw, d), full), pl.BlockSpec((hw, d), full),
        pl.BlockSpec((N_EXPERTS, d), full), pl.BlockSpec((N_EXPERTS, 1), full),
    ]
    out_specs = [
        pl.BlockSpec((TM, d), row), pl.BlockSpec((TM, d // 2), row),
        pl.BlockSpec((TOP_K, TM), col), pl.BlockSpec((TOP_K, TM), col), pl.BlockSpec((TOP_K, TM), col),
        pl.BlockSpec((N_EXPERTS, LANES), full),
    ]
    out_shape = [
        jax.ShapeDtypeStruct((t, d), F32), jax.ShapeDtypeStruct((t, d // 2), U32),
        jax.ShapeDtypeStruct((TOP_K, t), jnp.int32), jax.ShapeDtypeStruct((TOP_K, t), F32),
        jax.ShapeDtypeStruct((TOP_K, t), jnp.int32),
        jax.ShapeDtypeStruct((N_EXPERTS, LANES), F32),
    ]
    args = [oa, ob, x2, gt1, sc2, sh2, g_post, g_pre, w["wo_a"], w["wo_b"], w["wr_t"], w["b_r"]]
    body = _post_kernel
    if run_after is not None:
        n_in = len(args)
        body = lambda *refs: _post_kernel(*refs[:n_in], *refs[n_in + 1:])
        in_specs.append(pl.BlockSpec(memory_space=pl.ANY))
        args.append(run_after)
    return pl.pallas_call(
        body,
        grid=(t // TM,),
        in_specs=in_specs,
        out_specs=out_specs,
        out_shape=out_shape,
        scratch_shapes=[pltpu.VMEM((N_EXPERTS, LANES), F32)],
        compiler_params=_cparams(("arbitrary",)),
        name="post",
    )(*args)


def _expert_kernel(be_ref, nu_ref, x_ref, wgu_ref, bgu_ref, wd_ref, bd_ref, y_ref, wgu_bf, wd_bf):
    j = pl.program_id(0)
    used = j < nu_ref[0]

    @pl.when(used & ((j == 0) | (be_ref[j] != be_ref[jnp.maximum(j - 1, 0)])))
    def _():
        wgu_bf[...] = wgu_ref[0].astype(BF16)
        wd_bf[...] = wd_ref[0].astype(BF16)

    @pl.when(used)
    def _():
        gus = []
        for rs in range(0, ROWS, EXPERT_SUB):
            x_lo, x_hi = _unpack_rows(x_ref[rs:rs + EXPERT_SUB])
            half = x_lo.shape[1]
            gu = jnp.dot(x_lo.astype(BF16), wgu_bf[:half], preferred_element_type=F32)
            gu += jnp.dot(x_hi.astype(BF16), wgu_bf[half:], preferred_element_type=F32)
            gus.append(gu + bgu_ref[0])
        for sub, gu in enumerate(gus):
            rows = slice(sub * EXPERT_SUB, (sub + 1) * EXPERT_SUB)
            gate = jnp.minimum(gu[:, :D_EXPERT], SWIGLU_LIMIT)
            up = jnp.clip(gu[:, D_EXPERT:], -SWIGLU_LIMIT, SWIGLU_LIMIT)
            glu = gate / (1.0 + jnp.exp(-SWIGLU_ALPHA * gate))
            act = ((up + 1.0) * glu).astype(BF16)
            y = jnp.dot(act, wd_bf[...], preferred_element_type=F32) + bd_ref[0]
            y_ref[rows] = _pack_rows(y)

    @pl.when(j >= nu_ref[0])
    def _():
        y_ref[...] = jnp.zeros_like(y_ref)


def _experts(xin, block_exp, n_used, wgu, bgu, wd, bd):
    p_rows, dw = xin.shape
    d = 2 * dw
    nb = p_rows // ROWS
    f2 = wgu.shape[2]
    grid_spec = pltpu.PrefetchScalarGridSpec(
        num_scalar_prefetch=2,
        grid=(nb,),
        in_specs=[
            pl.BlockSpec((ROWS, dw), lambda j, be, nu: (jnp.minimum(j, nu[0] - 1), 0)),
            pl.BlockSpec((1, d, f2), lambda j, be, nu: (be[j], 0, 0)),
            pl.BlockSpec((1, 1, f2), lambda j, be, nu: (be[j], 0, 0)),
            pl.BlockSpec((1, f2 // 2, d), lambda j, be, nu: (be[j], 0, 0)),
            pl.BlockSpec((1, 1, d), lambda j, be, nu: (be[j], 0, 0)),
        ],
        out_specs=pl.BlockSpec((ROWS, dw), lambda j, be, nu: (j, 0)),
        scratch_shapes=[pltpu.VMEM((d, f2), BF16), pltpu.VMEM((f2 // 2, d), BF16)],
    )
    return pl.pallas_call(
        _expert_kernel,
        grid_spec=grid_spec,
        out_shape=jax.ShapeDtypeStruct((p_rows, dw), U32),
        compiler_params=_cparams(("arbitrary",)),
        name="experts",
    )(block_exp, n_used, xin, wgu, bgu, wd, bd)


def _final_kernel(yg_ref, g_ref, x1_ref, gt_ref, gpost_ref, *rest):
    o_ref = rest[-1]
    g = g_ref[...]
    f_lo, f_hi = None, None
    for k in range(TOP_K):
        lo, hi = _unpack_rows(yg_ref[k])
        gk = g[:, k:k + 1]
        f_lo = lo * gk if f_lo is None else f_lo + lo * gk
        f_hi = hi * gk if f_hi is None else f_hi + hi * gk
    f = jnp.concatenate([f_lo, f_hi], axis=1)
    o_ref[...] = x1_ref[...] + gt_ref[0] * _rms(f, gpost_ref[...])


def _final(yg, gates_t, x1, gt2, g_post, seq, tile_off, t_all, prev_out):
    t, d = x1.shape
    tpb = seq // TM
    row = lambda i: (i, 0)
    in_specs = [pl.BlockSpec((TOP_K, TM, d // 2), lambda i: (0, i, 0)),
                pl.BlockSpec((TM, TOP_K), row),
                pl.BlockSpec((TM, d), row),
                pl.BlockSpec((1, 1, d), lambda i: ((i + tile_off) // tpb, 0, 0)),
                pl.BlockSpec((1, d), lambda i: (0, 0))]
    args = [yg, gates_t, x1, gt2, g_post]
    aliases = {}
    if prev_out is not None:
        in_specs.append(pl.BlockSpec(memory_space=pl.ANY))
        args.append(prev_out)
        aliases = {len(args) - 1: 0}
    return pl.pallas_call(
        _final_kernel,
        grid=(t // TM,),
        in_specs=in_specs,
        out_specs=pl.BlockSpec((TM, d), lambda i: (i + tile_off, 0)),
        out_shape=jax.ShapeDtypeStruct((t_all, d), F32),
        input_output_aliases=aliases,
        compiler_params=_cparams(("arbitrary",)),
        name="final",
    )(*args)


def _sc_mesh():
    return plsc.VectorSubcoreMesh(core_axis_name="c", subcore_axis_name="s")


def _sc_worker():
    return lax.axis_index("s") * SC_CORES + lax.axis_index("c")


def _dispatch(h2, dest, p_rows):
    t, dw = h2.shape
    per_w = t // SC_WORKERS
    n_win = per_w // SC_WIN

    @functools.partial(
        pl.kernel, mesh=_sc_mesh(),
        out_type=jax.ShapeDtypeStruct((p_rows, dw), h2.dtype),
        scratch_types=[pltpu.VMEM((TOP_K, SC_WIN), jnp.int32),
                       pltpu.VMEM((SC_WIN, dw), h2.dtype),
                       pltpu.SemaphoreType.DMA],
        name="dispatch",
    )
    def run(h_hbm, d_hbm, o_hbm, idx_v, rows_v, sem):
        wid = _sc_worker()

        @pl.loop(0, n_win)
        def _(wi):
            base = pl.multiple_of(wid * per_w + wi * SC_WIN, SC_WIN)
            for k in range(TOP_K):
                pltpu.sync_copy(d_hbm.at[pl.ds(k * t + base, SC_WIN)], idx_v.at[k])
            pltpu.sync_copy(h_hbm.at[pl.ds(base, SC_WIN)], rows_v)
            for k in range(TOP_K):
                pltpu.async_copy(rows_v, o_hbm.at[idx_v.at[k]], sem).wait()

    return run(h2, dest)


def _gather_rows(y, dest, t):
    _, dw = y.shape
    per_w = t // SC_WORKERS
    n_win = per_w // SC_WIN

    @functools.partial(
        pl.kernel, mesh=_sc_mesh(),
        out_type=jax.ShapeDtypeStruct((TOP_K, t, dw), y.dtype),
        scratch_types=[pltpu.VMEM((SC_WIN,), jnp.int32),
                       pltpu.VMEM((SC_WIN, dw), y.dtype),
                       pltpu.SemaphoreType.DMA],
        name="gather_rows",
    )
    def run(y_hbm, d_hbm, o_hbm, idx_v, rows_v, sem):
        wid = _sc_worker()

        @pl.loop(0, n_win)
        def _(wi):
            base = pl.multiple_of(wid * per_w + wi * SC_WIN, SC_WIN)
            for k in range(TOP_K):
                pltpu.sync_copy(d_hbm.at[pl.ds(k * t + base, SC_WIN)], idx_v)
                pltpu.async_copy(y_hbm.at[idx_v], rows_v, sem).wait()
                pltpu.sync_copy(rows_v, o_hbm.at[k, pl.ds(base, SC_WIN)])

    return run(y, dest)


def _prep_weights(w_in, g_q, w_qb, g_kv, w_kvb, w_o, w_router, b_router):
    d = w_in.shape[0]
    o = 0
    w_cq = w_in[:, o:o + Q_LORA]; o += Q_LORA
    w_ckv = w_in[:, o:o + KV_LORA]; o += KV_LORA
    w_kr = w_in[:, o:o + MLA_ROPE]; o += MLA_ROPE
    w_ca = w_in[:, o:]
    half = MLA_ROPE // 2
    zpad = lambda n: jnp.zeros((d, n), w_in.dtype)
    tail = HEAD_PAD - MLA_NOPE - MLA_ROPE
    w_kr_pad = jnp.concatenate([zpad(MLA_NOPE), w_kr, zpad(tail)], axis=1)
    w_in_all = jnp.concatenate([w_cq, w_ckv, w_kr_pad, w_ca], axis=1).astype(BF16)

    wq = w_qb.reshape(Q_LORA, MLA_HEADS, MLA_NOPE + MLA_ROPE)
    zq = jnp.zeros((Q_LORA, MLA_HEADS, tail), w_qb.dtype)
    wq_pad = jnp.concatenate([wq, zq], axis=-1).reshape(Q_LORA, -1).astype(BF16)

    wkv = w_kvb.reshape(KV_LORA, MLA_HEADS, MLA_NOPE + MLA_V)
    kn = wkv[..., :MLA_NOPE]
    wk = jnp.concatenate([kn, jnp.zeros((KV_LORA, MLA_HEADS, HEAD_PAD - MLA_NOPE), w_kvb.dtype)],
                         axis=-1).reshape(KV_LORA, -1).astype(BF16)
    wv = jnp.concatenate([wkv[..., MLA_NOPE:], jnp.zeros((KV_LORA, MLA_HEADS, HEAD_PAD - MLA_V), w_kvb.dtype)],
                         axis=-1).reshape(KV_LORA, -1).T.astype(BF16)
    v_one = jnp.tile((jnp.arange(HEAD_PAD) == MLA_V).astype(F32), MLA_HEADS).reshape(-1, 1)
    mla_w = MLA_HEADS * MLA_V
    return {
        "w_in": w_in_all, "g_q": g_q.reshape(1, -1), "wq": wq_pad,
        "g_kv": g_kv.reshape(1, -1), "wk": wk, "wv": wv, "v_one": v_one,
        "wo_a": w_o[:mla_w].astype(BF16), "wo_b": w_o[mla_w:].astype(BF16),
        "wr_t": w_router.T.astype(BF16), "b_r": b_router.reshape(-1, 1),
    }


def _rope_tables(seq):
    half = MLA_ROPE // 2
    inv_freq = ROPE_THETA ** (-jnp.arange(half, dtype=F32) / half)
    ang = jnp.arange(seq, dtype=F32)[:, None] * inv_freq[None, :]
    cos, sin = jnp.cos(ang), jnp.sin(ang)
    tail = HEAD_PAD - MLA_NOPE - MLA_ROPE
    ones = jnp.ones((seq, MLA_NOPE), F32)
    zn = jnp.zeros((seq, MLA_NOPE), F32)
    zt = jnp.zeros((seq, tail), F32)
    qs = (MLA_NOPE + MLA_ROPE) ** -0.5 * LOG2E
    return {
        "cq": jnp.concatenate([ones, cos, cos, zt], axis=1) * qs,
        "sq": jnp.concatenate([zn, -sin, sin, zt], axis=1) * qs,
        "ck": jnp.concatenate([zn, cos, cos, zt], axis=1),
        "sk": jnp.concatenate([zn, -sin, sin, zt], axis=1),
    }


def _bias_table(rel_bias):
    n = CQ + CBAND - 1
    rel = (CBAND - 1) - jnp.arange(n)
    diag = rel_bias.astype(F32)[:, jnp.clip(rel, -(CHUNK - 1), REL_MAX) + (CHUNK - 1)] * LOG2E
    diag = jnp.concatenate([diag, jnp.zeros((diag.shape[0], 1), F32)], axis=1)
    b = jnp.tile(diag, (1, CQ))[:, :CQ * n].reshape(-1, CQ, n)[:, :, CQ - 1:]
    r = jnp.arange(CQ)[:, None]
    c = jnp.arange(CBAND)[None, :]
    dchunk = r // CHUNK - (c // CHUNK - CA_LEFT)
    visible = (dchunk >= 0) & (dchunk <= CA_LEFT)
    exists = (c // CQ)[None] >= (2 - jnp.arange(3))[:, None, None]
    return jnp.where((visible[None] & exists)[:, None], b[None], NEG)


def _layer(x, c, w_ada, b_ada, g_pre_mix, g_post_mix, g_pre_ffn, g_post_ffn, w_in, g_q, w_qb,
           g_kv, w_kvb, rel_bias, w_o, w_router, b_router, w_gu, b_gu, w_down, b_down):
    bsz, seq, d = x.shape
    t = bsz * seq
    mod = _ada(c, w_ada, b_ada).reshape(bsz, 6, 1, d)
    sh1, sc1, gt1, sh2, sc2, gt2 = [mod[:, k] for k in range(6)]
    w = _prep_weights(w_in, g_q, w_qb, g_kv, w_kvb, w_o, w_router, b_router)
    tabs = _rope_tables(seq)
    x2 = x.reshape(t, d)

    bp = bsz // MOE_PARTS
    tp = bp * seq
    bias_t = jnp.swapaxes(_bias_table(rel_bias), 2, 3)
    eids = jnp.arange(N_EXPERTS, dtype=jnp.int32)[:, None, None]
    p_rows = tp * TOP_K + N_EXPERTS * ROWS
    block_start = jnp.arange(p_rows // ROWS, dtype=jnp.int32) * ROWS
    shp = lambda a: a.reshape(bp, seq, a.shape[-1])
    projected = [_proj(x2, sc1, sh1, g_pre_mix.reshape(1, d), w, tabs, seq, part * bp, tp)
                 for part in range(MOE_PARTS)]
    routed = []
    for part, (q, k, v, qc, kc, vc) in enumerate(projected):
        oa = _mla(shp(q), shp(k), v).reshape(tp, -1)
        ob = _chunk_attn(shp(qc), shp(kc), vc, bias_t).reshape(tp, -1)
        next_q = projected[part + 1][0] if part + 1 < MOE_PARTS else None
        x1, h2, top_idx, gates, rank, cnt = _post(oa, ob, x2, gt1, sc2, sh2, g_post_mix.reshape(1, d),
                                                  g_pre_ffn.reshape(1, d), w, seq, part * (tp // TM), tp,
                                                  run_after=next_q)
        counts = cnt[:, 0].astype(jnp.int32)
        padded = ((counts + ROWS - 1) // ROWS) * ROWS
        pend = jnp.cumsum(padded)
        pstart = pend - padded
        dest = (jnp.sum(jnp.where(top_idx[None] == eids, pstart[:, None, None], 0), axis=0) + rank).reshape(-1)
        block_exp = jnp.minimum(jnp.sum(pend[None, :] <= block_start[:, None], axis=1),
                                N_EXPERTS - 1).astype(jnp.int32)
        n_used = (pend[-1:] // ROWS).astype(jnp.int32)
        routed.append((x1, gates.T, dest, block_exp, n_used, _dispatch(h2, dest, p_rows)))

    out = None
    for part, (x1, gates_t, dest, block_exp, n_used, xin) in enumerate(routed):
        y = _experts(xin, block_exp, n_used, w_gu, b_gu.reshape(N_EXPERTS, 1, -1),
                     w_down, b_down.reshape(N_EXPERTS, 1, -1))
        yg = _gather_rows(y, dest, tp)
        out = _final(yg, gates_t, x1, gt2, g_post_ffn.reshape(1, d), seq, part * (tp // TM), t, out)
    return out.reshape(bsz, seq, d)


def kernel(x, c, w_ada, b_ada, g_pre_mix, g_post_mix, g_pre_ffn, g_post_ffn, w_in, g_q, w_qb,
           g_kv, w_kvb, rel_bias, w_o, w_router, b_router, w_gu, b_gu, w_down, b_down):
    for l in range(w_ada.shape[0]):
        x = _layer(x, c, w_ada[l], b_ada[l], g_pre_mix[l], g_post_mix[l], g_pre_ffn[l], g_post_ffn[l],
                   w_in[l], g_q[l], w_qb[l], g_kv[l], w_kvb[l], rel_bias[l], w_o[l], w_router[l],
                   b_router[l], w_gu[l], b_gu[l], w_down[l], b_down[l])
    return x
```

```python
import functools
import math

import jax
import jax.numpy as jnp
from jax import lax
from jax.experimental import pallas as pl
from jax.experimental.pallas import tpu as pltpu
from jax.experimental.pallas import tpu_sc as plsc

F32 = jnp.float32
BF16 = jnp.bfloat16
U32 = jnp.uint32

D_MODEL = 1024
CHUNK = 64
EPS = 1e-6
MLA_HEADS = 8
MLA_NOPE = 64
MLA_ROPE = 32
MLA_V = 64
Q_LORA = 256
KV_LORA = 128
ROPE_THETA = 10000.0
CA_HEADS = 8
CA_DIM = 64
CA_LEFT = 8
REL_MAX = 256
N_EXPERTS = 32
TOP_K = 4
D_EXPERT = 1024
SWIGLU_LIMIT = 7.0
SWIGLU_ALPHA = 1.702

LANES = 128
HEAD_PAD = 128
LOG2E = math.log2(math.e)
NEG = -1e30
VMEM_LIMIT = 56 * 1024 * 1024

TM = 1024
POST_SUB = 128
PTM = 1024
PROJ_SUB = 1024
TQ = 512
TKW = 2 * TQ
MLA_HPS = 8
MLA_LEAD = 2
CQ = 256
CBAND = 3 * CQ
CSTEP = 2 * CQ
ROWS = 512
EXPERT_SUB = 256
MOE_PARTS = 2
SC_CORES = 2
SC_WORKERS = SC_CORES * 16
SC_WIN = 128
CA_W = CA_HEADS * CA_DIM
W_IN_COLS = Q_LORA + KV_LORA + LANES + 3 * CA_W


def _cparams(sem, flags=None):
    return pltpu.CompilerParams(dimension_semantics=sem, vmem_limit_bytes=VMEM_LIMIT, flags=flags)


def _nt_dot(a, b):
    return lax.dot_general(a, b, (((1,), (1,)), ((), ())), preferred_element_type=F32)


def _rms(x, g):
    return x * lax.rsqrt(jnp.mean(x * x, axis=-1, keepdims=True) + EPS) * g


def _pack_rows(x):
    n = x.shape[1] // 2
    lo = lax.bitcast_convert_type(x[:, :n].astype(BF16).astype(F32), U32)
    hi = lax.bitcast_convert_type(x[:, n:].astype(BF16).astype(F32), U32)
    return (lo >> 16) | hi


def _unpack_rows(p):
    lo = lax.bitcast_convert_type(p << 16, F32)
    hi = lax.bitcast_convert_type(p & jnp.uint32(0xFFFF0000), F32)
    return lo, hi


def _ada_kernel(c_ref, w_ref, b_ref, o_ref):
    c = c_ref[...]
    a = (c / (1.0 + jnp.exp(-c))).astype(BF16)
    o_ref[...] = jnp.dot(a, w_ref[...].astype(BF16), preferred_element_type=F32) + b_ref[...]


def _ada(c, w, b):
    bsz, d = c.shape
    n = w.shape[1]
    tn = 1024
    return pl.pallas_call(
        _ada_kernel,
        grid=(n // tn,),
        in_specs=[pl.BlockSpec((bsz, d), lambda j: (0, 0)),
                  pl.BlockSpec((d, tn), lambda j: (0, j)),
                  pl.BlockSpec((1, tn), lambda j: (0, j))],
        out_specs=pl.BlockSpec((bsz, tn), lambda j: (0, j)),
        out_shape=jax.ShapeDtypeStruct((bsz, n), F32),
        compiler_params=_cparams(("arbitrary",)),
        name="ada",
    )(c, w, b.reshape(1, n))


def _swap_halves(x):
    width = x.shape[1]
    half = MLA_ROPE // 2
    lane = lax.broadcasted_iota(jnp.int32, x.shape, 1) % HEAD_PAD
    return jnp.where(lane < MLA_NOPE + half, pltpu.roll(x, width - half, axis=1), pltpu.roll(x, half, axis=1))


def _proj_kernel(x_ref, sc_ref, sh_ref, g_ref, win_ref, gq_ref, wq_ref, gkv_ref,
                 wk_ref, wv_ref, vone_ref, cq_ref, sq_ref, ck_ref, sk_ref,
                 q_out, k_out, v_out, qc_out, kc_out, vc_out):
    zs = []
    for rs in range(0, PTM, PROJ_SUB):
        h = _rms(x_ref[rs:rs + PROJ_SUB], g_ref[...]) * (1.0 + sc_ref[0]) + sh_ref[0]
        zs.append(jnp.dot(h.astype(BF16), win_ref[...], preferred_element_type=F32))
    for sub, z in enumerate(zs):
        rows = slice(sub * PROJ_SUB, (sub + 1) * PROJ_SUB)
        o = 0
        cq = z[:, o:o + Q_LORA]; o += Q_LORA
        ckv = z[:, o:o + KV_LORA]; o += KV_LORA
        kr = z[:, o:o + LANES]; o += LANES
        qc = z[:, o:o + CA_W]; o += CA_W
        kc = z[:, o:o + CA_W]; o += CA_W
        vc = z[:, o:o + CA_W]

        cqn = _rms(cq, gq_ref[...]).astype(BF16)
        q = jnp.dot(cqn, wq_ref[...], preferred_element_type=F32)
        ckvn = _rms(ckv, gkv_ref[...]).astype(BF16)
        kn = jnp.dot(ckvn, wk_ref[...], preferred_element_type=F32)
        v_out[0, :, rows] = (_nt_dot(wv_ref[...], ckvn) + vone_ref[...]).astype(BF16)

        q_sw = _swap_halves(q)
        cq_t, sq_t = cq_ref[rows], sq_ref[rows]
        krope = kr * ck_ref[rows] + _swap_halves(kr) * sk_ref[rows]
        for hd in range(MLA_HEADS):
            sl = slice(hd * HEAD_PAD, (hd + 1) * HEAD_PAD)
            q_out[rows, sl] = (q[:, sl] * cq_t + q_sw[:, sl] * sq_t).astype(BF16)
            k_out[rows, sl] = (kn[:, sl] + krope).astype(BF16)

        qc_out[rows] = (qc * (CA_DIM ** -0.5 * LOG2E)).astype(BF16)
        kc_out[rows] = kc.astype(BF16)
        vc_out[0, :, rows] = vc.T.astype(BF16)


def _proj(x2, sc1, sh1, g_pre, w, tabs, seq, batch_off, t):
    d = x2.shape[1]
    tpb = seq // PTM
    row = lambda i: (i, 0)
    full = lambda i: (0, 0)
    per_b = lambda i: (i // tpb + batch_off, 0, 0)
    pos = lambda i: (i % tpb, 0)
    hw = MLA_HEADS * HEAD_PAD
    in_specs = [
        pl.BlockSpec((PTM, d), lambda i: (i + batch_off * tpb, 0)),
        pl.BlockSpec((1, 1, d), per_b), pl.BlockSpec((1, 1, d), per_b),
        pl.BlockSpec((1, d), full),
        pl.BlockSpec((d, W_IN_COLS), full),
        pl.BlockSpec((1, Q_LORA), full),
        pl.BlockSpec((Q_LORA, hw), full),
        pl.BlockSpec((1, KV_LORA), full),
        pl.BlockSpec((KV_LORA, hw), full), pl.BlockSpec((hw, KV_LORA), full), pl.BlockSpec((hw, 1), full),
        pl.BlockSpec((PTM, LANES), pos), pl.BlockSpec((PTM, LANES), pos),
        pl.BlockSpec((PTM, LANES), pos), pl.BlockSpec((PTM, LANES), pos),
    ]
    outs = ((hw, False), (hw, False), (hw, True), (CA_W, False), (CA_W, False), (CA_W, True))
    t_spec = lambda n: pl.BlockSpec((1, n, PTM), lambda i: (i // tpb, 0, i % tpb))
    return pl.pallas_call(
        _proj_kernel,
        grid=(t // PTM,),
        in_specs=in_specs,
        out_specs=[t_spec(n) if tr else pl.BlockSpec((PTM, n), row) for n, tr in outs],
        out_shape=[jax.ShapeDtypeStruct((t // seq, n, seq) if tr else (t, n), BF16) for n, tr in outs],
        compiler_params=_cparams(("arbitrary",)),
        name="proj",
    )(x2, sc1, sh1, g_pre, w["w_in"], w["g_q"], w["wq"], w["g_kv"], w["wk"], w["wv"], w["v_one"],
      tabs["cq"], tabs["sq"], tabs["ck"], tabs["sk"])


def _mla_kernel(q_ref, k_ref, v_ref, o_ref):
    i = pl.program_id(2)
    heads = [slice(hh * HEAD_PAD, (hh + 1) * HEAD_PAD) for hh in range(MLA_HPS)]

    def step(off, width, carry, masked=False):
        if masked:
            kc = lax.broadcasted_iota(jnp.int32, (width, TQ), 0) // CHUNK
            qc = lax.broadcasted_iota(jnp.int32, (width, TQ), 1) // CHUNK
            visible = kc <= qc

        def score(hs):
            s = _nt_dot(k_ref[0, pl.ds(off, width), hs], q_ref[0, :, hs])
            return jnp.where(visible, s, NEG) if masked else s

        scores = [score(hs) for hs in heads[:MLA_LEAD]]
        new = []
        for hh, hs in enumerate(heads):
            m, acc = carry[hh]
            m_new = jnp.maximum(m, jnp.max(scores[hh], axis=0, keepdims=True))
            p = jnp.exp2(scores[hh] - m_new).astype(BF16)
            if hh + MLA_LEAD < MLA_HPS:
                scores.append(score(heads[hh + MLA_LEAD]))
            pv = jnp.dot(v_ref[0, hs, pl.ds(off, width)], p, preferred_element_type=F32)
            new.append((m_new, jnp.exp2(m - m_new) * acc + pv))
        return tuple(new)

    init = tuple((jnp.full((1, TQ), NEG, F32), jnp.zeros((HEAD_PAD, TQ), F32)) for _ in heads)
    carry = lax.fori_loop(0, i // 2, lambda j, c: step(pl.multiple_of(j * TKW, TKW), TKW, c), init)
    carry = lax.fori_loop(0, i % 2, lambda _, c: step(pl.multiple_of((i - 1) * TQ, TQ), TQ, c), carry)
    carry = step(pl.multiple_of(i * TQ, TQ), TQ, carry, True)
    for pp in range(MLA_HPS // 2):
        pair = [acc[:MLA_V] / acc[MLA_V:MLA_V + 1] for _, acc in (carry[2 * pp], carry[2 * pp + 1])]
        o_ref[0, :, pp * LANES:(pp + 1) * LANES] = jnp.concatenate(pair, axis=0).T.astype(BF16)


def _mla(q, k, v):
    bsz, seq, _ = q.shape
    groups = MLA_HEADS // MLA_HPS
    return pl.pallas_call(
        _mla_kernel,
        grid=(bsz, groups, seq // TQ),
        in_specs=[pl.BlockSpec((1, TQ, MLA_HPS * HEAD_PAD), lambda b, p, i: (b, i, p)),
                  pl.BlockSpec((1, seq, MLA_HPS * HEAD_PAD), lambda b, p, i: (b, 0, p),
                               pipeline_mode=pl.Buffered(1)),
                  pl.BlockSpec((1, MLA_HPS * HEAD_PAD, seq), lambda b, p, i: (b, p, 0),
                               pipeline_mode=pl.Buffered(1))],
        out_specs=pl.BlockSpec((1, TQ, MLA_HPS * MLA_V), lambda b, p, i: (b, i, p)),
        out_shape=jax.ShapeDtypeStruct((bsz, seq, MLA_HEADS * MLA_V), BF16),
        compiler_params=_cparams(("arbitrary", "arbitrary", "arbitrary")),
        name="mla",
    )(q, k, v)


def _chunk_kernel(q_ref, kp_ref, kc_ref, vp_ref, vc_ref, bias0_ref, bias1_ref, o_ref):
    lane = lax.broadcasted_iota(jnp.int32, (CQ, LANES), 1)
    lo = lane < CA_DIM
    row = lax.broadcasted_iota(jnp.int32, (LANES, CQ), 0)
    top = row < CA_DIM
    bias_refs = (bias0_ref, bias1_ref)
    subs = CSTEP // CQ

    def key_rows(j):
        return (kp_ref if j < subs else kc_ref), slice((j % subs) * CQ, (j % subs + 1) * CQ)

    def score(sub, head):
        sl = slice((head // 2) * LANES, (head // 2 + 1) * LANES)
        q = q_ref[0, sub * CQ:(sub + 1) * CQ, sl]
        qm = jnp.where(lo if head % 2 == 0 else jnp.logical_not(lo), q, jnp.zeros_like(q))
        parts = []
        for cb in range(3):
            ref, rows = key_rows(sub + cb)
            parts.append(_nt_dot(ref[0, rows, sl], qm))
        return jnp.concatenate(parts, axis=0)

    scores = {(sub, h): score(sub, h) for sub in range(subs) for h in range(CA_HEADS)}
    for sub in range(subs):
        for p in range(CA_HEADS // 2):
            sl = slice(p * LANES, (p + 1) * LANES)
            vts = []
            for cb in range(3):
                ref, cols = key_rows(sub + cb)
                vts.append((vp_ref if ref is kp_ref else vc_ref)[0, sl, cols])
            outs = []
            for hh in range(2):
                mine = top if hh == 0 else jnp.logical_not(top)
                den_row = CA_DIM if hh == 0 else 0
                ones_row = (row == den_row).astype(BF16)
                s = scores[(sub, 2 * p + hh)] + bias_refs[sub][0, 2 * p + hh]
                m = jnp.max(s, axis=0, keepdims=True)
                pb = jnp.exp2(s - m).astype(BF16)
                o = None
                for cb in range(3):
                    part = jnp.dot(jnp.where(mine, vts[cb], ones_row), pb[cb * CQ:(cb + 1) * CQ],
                                   preferred_element_type=F32)
                    o = part if o is None else o + part
                o = o / o[den_row:den_row + 1]
                outs.append(o[:CA_DIM] if hh == 0 else o[CA_DIM:])
            o_ref[0, sub * CQ:(sub + 1) * CQ, sl] = jnp.concatenate(outs, axis=0).T.astype(BF16)


def _chunk_attn(qc, kc, vct, bias):
    bsz, seq, w = qc.shape
    subs = CSTEP // CQ
    prev = lambda i: jnp.maximum(i - 1, 0)
    table = lambda s: pl.BlockSpec((1, CA_HEADS, CBAND, CQ), lambda b, i: (jnp.minimum(subs * i + s, 2), 0, 0, 0))
    return pl.pallas_call(
        _chunk_kernel,
        grid=(bsz, seq // CSTEP),
        in_specs=[pl.BlockSpec((1, CSTEP, w), lambda b, i: (b, i, 0)),
                  pl.BlockSpec((1, CSTEP, w), lambda b, i: (b, prev(i), 0)),
                  pl.BlockSpec((1, CSTEP, w), lambda b, i: (b, i, 0)),
                  pl.BlockSpec((1, w, CSTEP), lambda b, i: (b, 0, prev(i))),
                  pl.BlockSpec((1, w, CSTEP), lambda b, i: (b, 0, i)),
                  table(0), table(1)],
        out_specs=pl.BlockSpec((1, CSTEP, w), lambda b, i: (b, i, 0)),
        out_shape=jax.ShapeDtypeStruct((bsz, seq, w), BF16),
        compiler_params=_cparams(("arbitrary", "arbitrary")),
        name="chunk_attn",
    )(qc, kc, kc, vct, vct, bias, bias)


def _post_kernel(oa_ref, ob_ref, x_ref, gt_ref, sc_ref, sh_ref, gpost_ref, gpre_ref,
                 woa_ref, wob_ref, wr_ref, br_ref,
                 x1_out, h2_out, idx_out, gate_out, rank_out, cnt_out, carry_ref):
    t = pl.program_id(0)

    @pl.when(t == 0)
    def _():
        carry_ref[...] = jnp.zeros_like(carry_ref)

    os = []
    for rs in range(0, TM, POST_SUB):
        rows = slice(rs, rs + POST_SUB)
        o = jnp.dot(oa_ref[rows], woa_ref[...], preferred_element_type=F32)
        os.append(o + jnp.dot(ob_ref[rows], wob_ref[...], preferred_element_type=F32))
    h2s = []
    gain1 = gt_ref[0] * gpost_ref[...]
    gain2 = gpre_ref[...] * (1.0 + sc_ref[0])
    for sub, o in enumerate(os):
        rows = slice(sub * POST_SUB, (sub + 1) * POST_SUB)
        x1 = x_ref[rows] + _rms(o, gain1)
        x1_out[rows] = x1
        h2f = _rms(x1, gain2) + sh_ref[0]
        h2_out[rows] = _pack_rows(h2f)
        h2s.append(h2f.astype(BF16))
    h2 = jnp.concatenate(h2s, axis=0)

    logits = _nt_dot(wr_ref[...], h2) + br_ref[...]
    eid = lax.broadcasted_iota(jnp.int32, (N_EXPERTS, TM), 0)
    vals, idxs = [], []
    work = logits
    for _k in range(TOP_K):
        m = jnp.max(work, axis=0, keepdims=True)
        ix = jnp.min(jnp.where(work == m, eid, N_EXPERTS), axis=0, keepdims=True)
        work = jnp.where(eid == ix, -jnp.inf, work)
        vals.append(m)
        idxs.append(ix)
    es = [jnp.exp(v - vals[0]) for v in vals]
    den = es[0] + es[1] + es[2] + es[3]
    gate_out[...] = jnp.concatenate([e / den for e in es], axis=0)
    idx_out[...] = jnp.concatenate(idxs, axis=0)

    sel = (eid == idxs[0]) | (eid == idxs[1]) | (eid == idxs[2]) | (eid == idxs[3])
    self32 = sel.astype(F32)
    rr = lax.broadcasted_iota(jnp.int32, (TM, TM), 0)
    cc = lax.broadcasted_iota(jnp.int32, (TM, TM), 1)
    upper = (rr < cc).astype(BF16)
    before = jnp.dot(self32.astype(BF16), upper, preferred_element_type=F32)
    before = before + carry_ref[:, 0:1]
    ranks = [jnp.sum(jnp.where(eid == ix, before, 0.0), axis=0, keepdims=True) for ix in idxs]
    rank_out[...] = jnp.concatenate(ranks, axis=0).astype(jnp.int32)
    carry_ref[...] = carry_ref[...] + jnp.sum(self32, axis=1, keepdims=True)
    cnt_out[...] = carry_ref[...]


def _post(oa, ob, x2, gt1, sc2, sh2, g_post, g_pre, w, seq, tile_off, t, run_after=None):
    d = x2.shape[1]
    tpb = seq // TM
    row = lambda i: (i, 0)
    src = lambda i: (i + tile_off, 0)
    col = lambda i: (0, i)
    full = lambda i: (0, 0)
    per_b = lambda i: ((i + tile_off) // tpb, 0, 0)
    hw = oa.shape[1]
    in_specs = [
        pl.BlockSpec((TM, hw), row), pl.BlockSpec((TM, hw), row), pl.BlockSpec((TM, d), src),
        pl.BlockSpec((1, 1, d), per_b), pl.BlockSpec((1, 1, d), per_b), pl.BlockSpec((1, 1, d), per_b),
        pl.BlockSpec((1, d), full), pl.BlockSpec((1, d), full),
        pl.BlockSpec((hw, d), full), pl.BlockSpec((hw, d), full),
        pl.BlockSpec((N_EXPERTS, d), full), pl.BlockSpec((N_EXPERTS, 1), full),
    ]
    out_specs = [
        pl.BlockSpec((TM, d), row), pl.BlockSpec((TM, d // 2), row),
        pl.BlockSpec((TOP_K, TM), col), pl.BlockSpec((TOP_K, TM), col), pl.BlockSpec((TOP_K, TM), col),
        pl.BlockSpec((N_EXPERTS, LANES), full),
    ]
    out_shape = [
        jax.ShapeDtypeStruct((t, d), F32), jax.ShapeDtypeStruct((t, d // 2), U32),
        jax.ShapeDtypeStruct((TOP_K, t), jnp.int32), jax.ShapeDtypeStruct((TOP_K, t), F32),
        jax.ShapeDtypeStruct((TOP_K, t), jnp.int32),
        jax.ShapeDtypeStruct((N_EXPERTS, LANES), F32),
    ]
    args = [oa, ob, x2, gt1, sc2, sh2, g_post, g_pre, w["wo_a"], w["wo_b"], w["wr_t"], w["b_r"]]
    body = _post_kernel
    if run_after is not None:
        n_in = len(args)
        body = lambda *refs: _post_kernel(*refs[:n_in], *refs[n_in + 1:])
        in_specs.append(pl.BlockSpec(memory_space=pl.ANY))
        args.append(run_after)
    return pl.pallas_call(
        body,
        grid=(t // TM,),
        in_specs=in_specs,
        out_specs=out_specs,
        out_shape=out_shape,
        scratch_shapes=[pltpu.VMEM((N_EXPERTS, LANES), F32)],
        compiler_params=_cparams(("arbitrary",)),
        name="post",
    )(*args)


def _expert_kernel(be_ref, nu_ref, x_ref, wgu_ref, bgu_ref, wd_ref, bd_ref, y_ref, wgu_bf, wd_bf):
    j = pl.program_id(0)
    used = j < nu_ref[0]

    @pl.when(used & ((j == 0) | (be_ref[j] != be_ref[jnp.maximum(j - 1, 0)])))
    def _():
        wgu_bf[...] = wgu_ref[0].astype(BF16)
        wd_bf[...] = wd_ref[0].astype(BF16)

    @pl.when(used)
    def _():
        gus = []
        for rs in range(0, ROWS, EXPERT_SUB):
            x_lo, x_hi = _unpack_rows(x_ref[rs:rs + EXPERT_SUB])
            half = x_lo.shape[1]
            gu = jnp.dot(x_lo.astype(BF16), wgu_bf[:half], preferred_element_type=F32)
            gu += jnp.dot(x_hi.astype(BF16), wgu_bf[half:], preferred_element_type=F32)
            gus.append(gu + bgu_ref[0])
        for sub, gu in enumerate(gus):
            rows = slice(sub * EXPERT_SUB, (sub + 1) * EXPERT_SUB)
            gate = jnp.minimum(gu[:, :D_EXPERT], SWIGLU_LIMIT)
            up = jnp.clip(gu[:, D_EXPERT:], -SWIGLU_LIMIT, SWIGLU_LIMIT)
            glu = gate / (1.0 + jnp.exp(-SWIGLU_ALPHA * gate))
            act = ((up + 1.0) * glu).astype(BF16)
            y = jnp.dot(act, wd_bf[...], preferred_element_type=F32) + bd_ref[0]
            y_ref[rows] = _pack_rows(y)

    @pl.when(j >= nu_ref[0])
    def _():
        y_ref[...] = jnp.zeros_like(y_ref)


def _experts(xin, block_exp, n_used, wgu, bgu, wd, bd):
    p_rows, dw = xin.shape
    d = 2 * dw
    nb = p_rows // ROWS
    f2 = wgu.shape[2]
    grid_spec = pltpu.PrefetchScalarGridSpec(
        num_scalar_prefetch=2,
        grid=(nb,),
        in_specs=[
            pl.BlockSpec((ROWS, dw), lambda j, be, nu: (jnp.minimum(j, nu[0] - 1), 0)),
            pl.BlockSpec((1, d, f2), lambda j, be, nu: (be[j], 0, 0)),
            pl.BlockSpec((1, 1, f2), lambda j, be, nu: (be[j], 0, 0)),
            pl.BlockSpec((1, f2 // 2, d), lambda j, be, nu: (be[j], 0, 0)),
            pl.BlockSpec((1, 1, d), lambda j, be, nu: (be[j], 0, 0)),
        ],
        out_specs=pl.BlockSpec((ROWS, dw), lambda j, be, nu: (j, 0)),
        scratch_shapes=[pltpu.VMEM((d, f2), BF16), pltpu.VMEM((f2 // 2, d), BF16)],
    )
    return pl.pallas_call(
        _expert_kernel,
        grid_spec=grid_spec,
        out_shape=jax.ShapeDtypeStruct((p_rows, dw), U32),
        compiler_params=_cparams(("arbitrary",)),
        name="experts",
    )(block_exp, n_used, xin, wgu, bgu, wd, bd)


def _final_kernel(yg_ref, g_ref, x1_ref, gt_ref, gpost_ref, *rest):
    o_ref = rest[-1]
    g = g_ref[...]
    f_lo, f_hi = None, None
    for k in range(TOP_K):
        lo, hi = _unpack_rows(yg_ref[k])
        gk = g[:, k:k + 1]
        f_lo = lo * gk if f_lo is None else f_lo + lo * gk
        f_hi = hi * gk if f_hi is None else f_hi + hi * gk
    f = jnp.concatenate([f_lo, f_hi], axis=1)
    o_ref[...] = x1_ref[...] + gt_ref[0] * _rms(f, gpost_ref[...])


def _final(yg, gates_t, x1, gt2, g_post, seq, tile_off, t_all, prev_out):
    t, d = x1.shape
    tpb = seq // TM
    row = lambda i: (i, 0)
    in_specs = [pl.BlockSpec((TOP_K, TM, d // 2), lambda i: (0, i, 0), pipeline_mode=pl.Buffered(3)),
                pl.BlockSpec((TM, TOP_K), row),
                pl.BlockSpec((TM, d), row, pipeline_mode=pl.Buffered(3)),
                pl.BlockSpec((1, 1, d), lambda i: ((i + tile_off) // tpb, 0, 0)),
                pl.BlockSpec((1, d), lambda i: (0, 0))]
    out_spec = pl.BlockSpec((TM, d), lambda i: (i + tile_off, 0))
    args = [yg, gates_t, x1, gt2, g_post]
    aliases = {}
    if prev_out is not None:
        args.append(prev_out)
        aliases = {len(args) - 1: 0}

    def streamed(*refs):
        pltpu.emit_pipeline(_final_kernel, grid=(t // TM,), in_specs=in_specs, out_specs=[out_spec])(
            *refs[:5], refs[-1])

    return pl.pallas_call(
        streamed,
        in_specs=[pl.BlockSpec(memory_space=pl.ANY) for _ in args],
        out_specs=pl.BlockSpec(memory_space=pl.ANY),
        out_shape=jax.ShapeDtypeStruct((t_all, d), F32),
        input_output_aliases=aliases,
        compiler_params=pltpu.CompilerParams(vmem_limit_bytes=VMEM_LIMIT),
        name="final",
    )(*args)


def _sc_mesh():
    return plsc.VectorSubcoreMesh(core_axis_name="c", subcore_axis_name="s")


def _sc_worker():
    return lax.axis_index("s") * SC_CORES + lax.axis_index("c")


def _dispatch(h2, dest, p_rows):
    t, dw = h2.shape
    per_w = t // SC_WORKERS
    n_win = per_w // SC_WIN

    @functools.partial(
        pl.kernel, mesh=_sc_mesh(),
        out_type=jax.ShapeDtypeStruct((p_rows, dw), h2.dtype),
        scratch_types=[pltpu.VMEM((TOP_K, SC_WIN), jnp.int32),
                       pltpu.VMEM((SC_WIN, dw), h2.dtype),
                       pltpu.SemaphoreType.DMA],
        name="dispatch",
    )
    def run(h_hbm, d_hbm, o_hbm, idx_v, rows_v, sem):
        wid = _sc_worker()

        @pl.loop(0, n_win)
        def _(wi):
            base = pl.multiple_of(wid * per_w + wi * SC_WIN, SC_WIN)
            for k in range(TOP_K):
                pltpu.sync_copy(d_hbm.at[pl.ds(k * t + base, SC_WIN)], idx_v.at[k])
            pltpu.sync_copy(h_hbm.at[pl.ds(base, SC_WIN)], rows_v)
            for k in range(TOP_K):
                pltpu.async_copy(rows_v, o_hbm.at[idx_v.at[k]], sem).wait()

    return run(h2, dest)


def _gather_rows(y, dest, t):
    _, dw = y.shape
    per_w = t // SC_WORKERS
    n_win = per_w // SC_WIN

    @functools.partial(
        pl.kernel, mesh=_sc_mesh(),
        out_type=jax.ShapeDtypeStruct((TOP_K, t, dw), y.dtype),
        scratch_types=[pltpu.VMEM((SC_WIN,), jnp.int32),
                       pltpu.VMEM((SC_WIN, dw), y.dtype),
                       pltpu.SemaphoreType.DMA],
        name="gather_rows",
    )
    def run(y_hbm, d_hbm, o_hbm, idx_v, rows_v, sem):
        wid = _sc_worker()

        @pl.loop(0, n_win)
        def _(wi):
            base = pl.multiple_of(wid * per_w + wi * SC_WIN, SC_WIN)
            for k in range(TOP_K):
                pltpu.sync_copy(d_hbm.at[pl.ds(k * t + base, SC_WIN)], idx_v)
                pltpu.async_copy(y_hbm.at[idx_v], rows_v, sem).wait()
                pltpu.sync_copy(rows_v, o_hbm.at[k, pl.ds(base, SC_WIN)])

    return run(y, dest)


def _prep_weights(w_in, g_q, w_qb, g_kv, w_kvb, w_o, w_router, b_router):
    d = w_in.shape[0]
    o = 0
    w_cq = w_in[:, o:o + Q_LORA]; o += Q_LORA
    w_ckv = w_in[:, o:o + KV_LORA]; o += KV_LORA
    w_kr = w_in[:, o:o + MLA_ROPE]; o += MLA_ROPE
    w_ca = w_in[:, o:]
    half = MLA_ROPE // 2
    zpad = lambda n: jnp.zeros((d, n), w_in.dtype)
    tail = HEAD_PAD - MLA_NOPE - MLA_ROPE
    w_kr_pad = jnp.concatenate([zpad(MLA_NOPE), w_kr, zpad(tail)], axis=1)
    w_in_all = jnp.concatenate([w_cq, w_ckv, w_kr_pad, w_ca], axis=1).astype(BF16)

    wq = w_qb.reshape(Q_LORA, MLA_HEADS, MLA_NOPE + MLA_ROPE)
    zq = jnp.zeros((Q_LORA, MLA_HEADS, tail), w_qb.dtype)
    wq_pad = jnp.concatenate([wq, zq], axis=-1).reshape(Q_LORA, -1).astype(BF16)

    wkv = w_kvb.reshape(KV_LORA, MLA_HEADS, MLA_NOPE + MLA_V)
    kn = wkv[..., :MLA_NOPE]
    wk = jnp.concatenate([kn, jnp.zeros((KV_LORA, MLA_HEADS, HEAD_PAD - MLA_NOPE), w_kvb.dtype)],
                         axis=-1).reshape(KV_LORA, -1).astype(BF16)
    wv = jnp.concatenate([wkv[..., MLA_NOPE:], jnp.zeros((KV_LORA, MLA_HEADS, HEAD_PAD - MLA_V), w_kvb.dtype)],
                         axis=-1).reshape(KV_LORA, -1).T.astype(BF16)
    v_one = jnp.tile((jnp.arange(HEAD_PAD) == MLA_V).astype(F32), MLA_HEADS).reshape(-1, 1)
    mla_w = MLA_HEADS * MLA_V
    return {
        "w_in": w_in_all, "g_q": g_q.reshape(1, -1), "wq": wq_pad,
        "g_kv": g_kv.reshape(1, -1), "wk": wk, "wv": wv, "v_one": v_one,
        "wo_a": w_o[:mla_w].astype(BF16), "wo_b": w_o[mla_w:].astype(BF16),
        "wr_t": w_router.T.astype(BF16), "b_r": b_router.reshape(-1, 1),
    }


def _rope_tables(seq):
    half = MLA_ROPE // 2
    inv_freq = ROPE_THETA ** (-jnp.arange(half, dtype=F32) / half)
    ang = jnp.arange(seq, dtype=F32)[:, None] * inv_freq[None, :]
    cos, sin = jnp.cos(ang), jnp.sin(ang)
    tail = HEAD_PAD - MLA_NOPE - MLA_ROPE
    ones = jnp.ones((seq, MLA_NOPE), F32)
    zn = jnp.zeros((seq, MLA_NOPE), F32)
    zt = jnp.zeros((seq, tail), F32)
    qs = (MLA_NOPE + MLA_ROPE) ** -0.5 * LOG2E
    return {
        "cq": jnp.concatenate([ones, cos, cos, zt], axis=1) * qs,
        "sq": jnp.concatenate([zn, -sin, sin, zt], axis=1) * qs,
        "ck": jnp.concatenate([zn, cos, cos, zt], axis=1),
        "sk": jnp.concatenate([zn, -sin, sin, zt], axis=1),
    }


def _bias_table(rel_bias):
    n = CQ + CBAND - 1
    rel = (CBAND - 1) - jnp.arange(n)
    diag = rel_bias.astype(F32)[:, jnp.clip(rel, -(CHUNK - 1), REL_MAX) + (CHUNK - 1)] * LOG2E
    diag = jnp.concatenate([diag, jnp.zeros((diag.shape[0], 1), F32)], axis=1)
    b = jnp.tile(diag, (1, CQ))[:, :CQ * n].reshape(-1, CQ, n)[:, :, CQ - 1:]
    r = jnp.arange(CQ)[:, None]
    c = jnp.arange(CBAND)[None, :]
    dchunk = r // CHUNK - (c // CHUNK - CA_LEFT)
    visible = (dchunk >= 0) & (dchunk <= CA_LEFT)
    exists = (c // CQ)[None] >= (2 - jnp.arange(3))[:, None, None]
    return jnp.where((visible[None] & exists)[:, None], b[None], NEG)


def _layer(x, c, w_ada, b_ada, g_pre_mix, g_post_mix, g_pre_ffn, g_post_ffn, w_in, g_q, w_qb,
           g_kv, w_kvb, rel_bias, w_o, w_router, b_router, w_gu, b_gu, w_down, b_down):
    bsz, seq, d = x.shape
    t = bsz * seq
    mod = _ada(c, w_ada, b_ada).reshape(bsz, 6, 1, d)
    sh1, sc1, gt1, sh2, sc2, gt2 = [mod[:, k] for k in range(6)]
    w = _prep_weights(w_in, g_q, w_qb, g_kv, w_kvb, w_o, w_router, b_router)
    tabs = _rope_tables(seq)
    x2 = x.reshape(t, d)

    bp = bsz // MOE_PARTS
    tp = bp * seq
    bias_t = jnp.swapaxes(_bias_table(rel_bias), 2, 3)
    eids = jnp.arange(N_EXPERTS, dtype=jnp.int32)[:, None, None]
    p_rows = tp * TOP_K + N_EXPERTS * ROWS
    block_start = jnp.arange(p_rows // ROWS, dtype=jnp.int32) * ROWS
    shp = lambda a: a.reshape(bp, seq, a.shape[-1])
    projected = [_proj(x2, sc1, sh1, g_pre_mix.reshape(1, d), w, tabs, seq, part * bp, tp)
                 for part in range(MOE_PARTS)]
    routed = []
    for part, (q, k, v, qc, kc, vc) in enumerate(projected):
        oa = _mla(shp(q), shp(k), v).reshape(tp, -1)
        ob = _chunk_attn(shp(qc), shp(kc), vc, bias_t).reshape(tp, -1)
        next_q = projected[part + 1][0] if part + 1 < MOE_PARTS else None
        x1, h2, top_idx, gates, rank, cnt = _post(oa, ob, x2, gt1, sc2, sh2, g_post_mix.reshape(1, d),
                                                  g_pre_ffn.reshape(1, d), w, seq, part * (tp // TM), tp,
                                                  run_after=next_q)
        counts = cnt[:, 0].astype(jnp.int32)
        padded = ((counts + ROWS - 1) // ROWS) * ROWS
        pend = jnp.cumsum(padded)
        pstart = pend - padded
        dest = (jnp.sum(jnp.where(top_idx[None] == eids, pstart[:, None, None], 0), axis=0) + rank).reshape(-1)
        block_exp = jnp.minimum(jnp.sum(pend[None, :] <= block_start[:, None], axis=1),
                                N_EXPERTS - 1).astype(jnp.int32)
        n_used = (pend[-1:] // ROWS).astype(jnp.int32)
        routed.append((x1, gates.T, dest, block_exp, n_used, _dispatch(h2, dest, p_rows)))

    out = None
    for part, (x1, gates_t, dest, block_exp, n_used, xin) in enumerate(routed):
        y = _experts(xin, block_exp, n_used, w_gu, b_gu.reshape(N_EXPERTS, 1, -1),
                     w_down, b_down.reshape(N_EXPERTS, 1, -1))
        yg = _gather_rows(y, dest, tp)
        out = _final(yg, gates_t, x1, gt2, g_post_ffn.reshape(1, d), seq, part * (tp // TM), t, out)
    return out.reshape(bsz, seq, d)


def kernel(x, c, w_ada, b_ada, g_pre_mix, g_post_mix, g_pre_ffn, g_post_ffn, w_in, g_q, w_qb,
           g_kv, w_kvb, rel_bias, w_o, w_router, b_router, w_gu, b_gu, w_down, b_down):
    for l in range(w_ada.shape[0]):
        x = _layer(x, c, w_ada[l], b_ada[l], g_pre_mix[l], g_post_mix[l], g_pre_ffn[l], g_post_ffn[l],
                   w_in[l], g_q[l], w_qb[l], g_kv[l], w_kvb[l], rel_bias[l], w_o[l], w_router[l],
                   b_router[l], w_gu[l], b_gu[l], w_down[l], b_down[l])
    return x
```
